```python
import math
import jax
import jax.numpy as jnp
from jax import lax
import numpy as np

D_MODEL = 2048
BATCH = 4
SEQ = 4096
DEPTH = 2

GRID_W = 64
CTX_LEN = 256
HEAD_DIM = 128
A_HEADS = D_MODEL // (2 * HEAD_DIM)
A_KV_HEADS = A_HEADS // 4
B_HEADS = D_MODEL // (4 * HEAD_DIM)
C_HEADS = D_MODEL // (4 * HEAD_DIM)
A_Q = A_HEADS * HEAD_DIM
A_KV = A_KV_HEADS * HEAD_DIM
B_W = B_HEADS * HEAD_DIM
C_W = C_HEADS * HEAD_DIM
MIX_W = A_Q + B_W + C_W
Q_BLOCK = 128
ROPE_THETA = 10000.0
NA_WIN_H = 8
NA_WIN_W = 16
CONV_K = 3
DN_CHUNK = 64
EPS = 1e-6
ALPHA = (2 * DEPTH) ** 0.25
OUT_INIT = (8 * DEPTH) ** -0.25
IN_SPLITS = (A_Q, A_KV, A_KV, A_Q, B_W, B_W, B_W, B_W, 3 * C_W, C_W, 4 * C_HEADS)
IN_W = sum(IN_SPLITS)

kernel_name = 'hybrid_grid_flow_block'


def _layernorm(x, g=None, b=None):
    xf = x.astype(jnp.float32)
    xc = xf - jnp.mean(xf, -1, keepdims=True)
    y = xc * lax.rsqrt(jnp.mean(xc * xc, -1, keepdims=True) + EPS)
    if g is not None:
        y = y * g.astype(jnp.float32) + b.astype(jnp.float32)
    return y.astype(x.dtype)


def _rmsnorm(x, g):
    xf = x.astype(jnp.float32)
    y = xf * lax.rsqrt(jnp.mean(xf * xf, -1, keepdims=True) + EPS) * g.astype(jnp.float32)
    return y.astype(x.dtype)


def _l2norm(x):
    return x * lax.rsqrt(jnp.sum(x * x, -1, keepdims=True) + EPS)


def _heads(t, n):
    return t.reshape(t.shape[:-1] + (n, HEAD_DIM))


def _split_in(p):
    return jnp.split(p, np.cumsum(IN_SPLITS)[:-1].tolist(), axis=-1)


def _axial_rope_tables(n_tokens):
    t = jnp.arange(n_tokens, dtype=jnp.int32)
    row = (t // GRID_W).astype(jnp.float32)
    col = (t % GRID_W).astype(jnp.float32)
    half = HEAD_DIM // 2
    inv_freq = ROPE_THETA ** (-jnp.arange(0, half, 2, dtype=jnp.float32) / half)
    ang = jnp.concatenate([row[:, None] * inv_freq, col[:, None] * inv_freq], -1)
    return jnp.cos(ang), jnp.sin(ang)


def _apply_axial_rope(x, cos, sin):
    B, T, H, D = x.shape
    xf = x.astype(jnp.float32).reshape(B, T, H, 2, 2, D // 4)
    x1, x2 = xf[..., 0, :], xf[..., 1, :]
    c = cos.reshape(T, 1, 2, D // 4)
    s = sin.reshape(T, 1, 2, D // 4)
    out = jnp.stack([x1 * c - x2 * s, x2 * c + x1 * s], axis=-2)
    return out.reshape(B, T, H, D).astype(x.dtype)


def _dense_attention(q, k, v):
    B, T, Hq, D = q.shape
    Hkv = k.shape[2]
    qg = q.reshape(B, T, Hkv, Hq // Hkv, D)
    s = jnp.einsum('bqkgd,bskd->bkgqs', qg, k, preferred_element_type=jnp.float32) * D ** -0.5
    p = jax.nn.softmax(s, axis=-1).astype(v.dtype)
    return jnp.einsum('bkgqs,bskd->bqkgd', p, v).reshape(B, T, Hq * D)


def _grid_attention(q, k, v, k_ctx, v_ctx):
    B, S, _, D = q.shape
    G = A_HEADS // A_KV_HEADS
    k_all = jnp.concatenate([k_ctx, k], axis=1)
    v_all = jnp.concatenate([v_ctx, v], axis=1)
    nblk = S // Q_BLOCK
    qb = q.reshape(B, nblk, Q_BLOCK, A_KV_HEADS, G, D).transpose(1, 0, 2, 3, 4, 5)
    scale = D ** -0.5

    def block(qi):
        s = jnp.einsum('bqkgd,bskd->bkgqs', qi, k_all, preferred_element_type=jnp.float32) * scale
        p = jax.nn.softmax(s, axis=-1).astype(v_all.dtype)
        return jnp.einsum('bkgqs,bskd->bqkgd', p, v_all)

    o = lax.map(block, qb)
    return o.transpose(1, 0, 2, 3, 4, 5).reshape(B, S, A_HEADS * D)


def _neighbourhood_attention(q, k, v, k_ctx, v_ctx, rpb, rows):
    B, S, H, D = q.shape
    wh = min(NA_WIN_H, rows)
    qg = q.reshape(B, rows, GRID_W, H, D).transpose(1, 0, 2, 3, 4)
    kg = k.reshape(B, rows, GRID_W, H, D)
    vg = v.reshape(B, rows, GRID_W, H, D)
    r = jnp.arange(rows, dtype=jnp.int32)
    r0 = jnp.clip(r - wh // 2, 0, rows - wh)
    cq = jnp.arange(GRID_W, dtype=jnp.int32)
    c0 = jnp.clip(cq - NA_WIN_W // 2, 0, GRID_W - NA_WIN_W)
    col_idx = c0[:, None] + jnp.arange(NA_WIN_W, dtype=jnp.int32)
    bias_c = col_idx - cq[:, None] + (NA_WIN_W - 1)
    nw = wh * NA_WIN_W
    scale = D ** -0.5

    def row_block(args):
        q_row, r_q, r_start = args
        k_win = lax.dynamic_slice_in_dim(kg, r_start, wh, axis=1)[:, :, col_idx]
        v_win = lax.dynamic_slice_in_dim(vg, r_start, wh, axis=1)[:, :, col_idx]
        bias_r = r_start + jnp.arange(wh, dtype=jnp.int32) - r_q + (NA_WIN_H - 1)
        bias = rpb[:, bias_r[None, :, None], bias_c[:, None, :]]
        s_win = jnp.einsum('bchd,bicjhd->bhcij', q_row, k_win,
                           preferred_element_type=jnp.float32) * scale + bias.astype(jnp.float32)
        s_ctx = jnp.einsum('bchd,bshd->bhcs', q_row, k_ctx, preferred_element_type=jnp.float32) * scale
        s = jnp.concatenate([s_win.reshape(B, H, GRID_W, nw), s_ctx], axis=-1)
        p = jax.nn.softmax(s, axis=-1).astype(v.dtype)
        p_win = p[..., :nw].reshape(B, H, GRID_W, wh, NA_WIN_W)
        return (jnp.einsum('bhcij,bicjhd->bchd', p_win, v_win)
                + jnp.einsum('bhcs,bshd->bchd', p[..., nw:], v_ctx))

    o = lax.map(row_block, (qg, r, r0))
    return o.transpose(1, 0, 2, 3, 4).reshape(B, S, H * D)


def _short_conv(x, w):
    y = lax.conv_general_dilated(x, w[:, None, :].astype(x.dtype), window_strides=(1,),
                                 padding=[(CONV_K // 2, CONV_K // 2)],
                                 dimension_numbers=('NWC', 'WIO', 'NWC'),
                                 feature_group_count=x.shape[-1])
    return jax.nn.silu(y)


def _dn_qkv(qkv, conv_w):
    y = _short_conv(qkv, conv_w).astype(jnp.float32)
    q, k, v = jnp.split(y, 3, axis=-1)
    q = _l2norm(_heads(q, C_HEADS)) * HEAD_DIM ** -0.5
    k = _l2norm(_heads(k, C_HEADS))
    return q, k, _heads(v, C_HEADS)


def _dn_gates(ab, a_log, dt_bias):
    abf = ab.astype(jnp.float32).reshape(ab.shape[:-1] + (2, 2, C_HEADS))
    beta = jax.nn.sigmoid(abf[..., 0, :])
    g = -jnp.exp(a_log.astype(jnp.float32)) * jax.nn.softplus(abf[..., 1, :] + dt_bias.astype(jnp.float32))
    return beta, g


def _gdn_chunked(q, k, v, g, beta, s0):
    B, T, H, Dk = q.shape
    C = DN_CHUNK
    N = T // C

    def chunks(t):
        return t.reshape(B, N, C, H, t.shape[-1]).transpose(1, 0, 3, 2, 4)

    qc, kc, vc = chunks(q), chunks(k), chunks(v)
    gc = g.reshape(B, N, C, H).transpose(1, 0, 3, 2)
    bc = beta.reshape(B, N, C, H).transpose(1, 0, 3, 2)
    gcum = jnp.cumsum(gc, axis=-1)
    lower = jnp.tril(jnp.ones((C, C), bool))
    strict = jnp.tril(jnp.ones((C, C), bool), -1)
    diff = gcum[..., :, None] - gcum[..., None, :]
    decay = jnp.where(lower, jnp.exp(jnp.where(lower, diff, 0.0)), 0.0)
    kb = kc * bc[..., None]
    lmat = jnp.where(strict, jnp.einsum('nbhid,nbhjd->nbhij', kb, kc) * decay, 0.0)
    a = lmat + jnp.eye(C, dtype=jnp.float32)
    u = lax.linalg.triangular_solve(a, vc * bc[..., None], left_side=True, lower=True, unit_diagonal=True)
    w = lax.linalg.triangular_solve(a, kb * jnp.exp(gcum)[..., None], left_side=True, lower=True,
                                    unit_diagonal=True)
    qk = jnp.einsum('nbhid,nbhjd->nbhij', qc, kc) * decay

    def step(s, xs):
        q_i, k_i, u_i, w_i, g_i, qk_i = xs
        v_new = u_i - jnp.einsum('bhcd,bhde->bhce', w_i, s)
        o = (jnp.einsum('bhcd,bhde->bhce', q_i * jnp.exp(g_i)[..., None], s)
             + jnp.einsum('bhij,bhje->bhie', qk_i, v_new))
        g_last = g_i[..., -1:]
        s = s * jnp.exp(g_last)[..., None] + jnp.einsum(
            'bhcd,bhce->bhde', k_i * jnp.exp(g_last - g_i)[..., None], v_new)
        return s, o

    s_fin, o = lax.scan(step, s0, (qc, kc, u, w, gcum, qk))
    return o.transpose(1, 0, 3, 2, 4).reshape(B, T, H, v.shape[-1]), s_fin


def _bidir_gated_deltanet(qkv, qkv_x, ab, ab_x, conv_w, a_log, dt_bias, with_ctx_out):
    q, k, v = _dn_qkv(qkv, conv_w)
    q_x, k_x, v_x = _dn_qkv(qkv_x, conv_w)
    beta, g = _dn_gates(ab, a_log, dt_bias)
    beta_x, g_x = _dn_gates(ab_x, a_log, dt_bias)
    s0 = jnp.zeros((q.shape[0], C_HEADS, HEAD_DIM, HEAD_DIM), jnp.float32)

    def rev(t):
        return t[:, ::-1]

    oxf, sxf = _gdn_chunked(q_x, k_x, v_x, g_x[:, :, 0], beta_x[:, :, 0], s0)
    of, _ = _gdn_chunked(q, k, v, g[:, :, 0], beta[:, :, 0], sxf)
    oxb, sxb = _gdn_chunked(rev(q_x), rev(k_x), rev(v_x), rev(g_x[:, :, 1]), rev(beta_x[:, :, 1]), s0)
    ob, _ = _gdn_chunked(rev(q), rev(k), rev(v), rev(g[:, :, 1]), rev(beta[:, :, 1]), sxb)
    o_lat = of + rev(ob)
    o_ctx = oxf + rev(oxb) if with_ctx_out else None
    return o_lat, o_ctx


def _layer(x, ctx, mod_lat, mod_ctx, w_in, q_norm, k_norm, rpb, conv_w, a_log, dt_bias, o_norm,
           w_out, ln_g, ln_b, rope_cos, rope_sin, rows, with_ctx_out):
    B, S, _ = x.shape
    L = ctx.shape[1]
    shift, scale, gate = jnp.split(mod_lat, 3, axis=-1)
    shift_x, scale_x, gate_x = jnp.split(mod_ctx, 3, axis=-1)
    h = _layernorm(x) * (1 + scale[:, None, :]) + shift[:, None, :]
    hx = _layernorm(ctx) * (1 + scale_x) + shift_x
    qa, ka, va, za, qb, kb, vb, zb, qkv_c, zc, ab = _split_in(h @ w_in)
    qa_x, ka_x, va_x, za_x, qb_x, kb_x, vb_x, zb_x, qkv_cx, zc_x, ab_x = _split_in(hx @ w_in)

    qa = _apply_axial_rope(_rmsnorm(_heads(qa, A_HEADS), q_norm), rope_cos, rope_sin)
    ka = _apply_axial_rope(_rmsnorm(_heads(ka, A_KV_HEADS), k_norm), rope_cos, rope_sin)
    ka_x = _rmsnorm(_heads(ka_x, A_KV_HEADS), k_norm)
    va_x = _heads(va_x, A_KV_HEADS)
    ya = _grid_attention(qa, ka, _heads(va, A_KV_HEADS), ka_x, va_x) * jax.nn.silu(za)

    kb_x, vb_x = _heads(kb_x, B_HEADS), _heads(vb_x, B_HEADS)
    yb = _neighbourhood_attention(_heads(qb, B_HEADS), _heads(kb, B_HEADS), _heads(vb, B_HEADS),
                                  kb_x, vb_x, rpb, rows) * jax.nn.silu(zb)

    oc, oc_x = _bidir_gated_deltanet(qkv_c, qkv_cx, ab, ab_x, conv_w, a_log, dt_bias, with_ctx_out)
    yc = _rmsnorm(oc, o_norm).astype(h.dtype).reshape(B, S, C_W) * jax.nn.silu(zc)

    y = jnp.concatenate([ya, yb, yc], axis=-1) @ w_out
    x_new = _layernorm(ALPHA * x + gate[:, None, :] * y, ln_g, ln_b)
    if not with_ctx_out:
        return x_new, ctx

    qa_x = _rmsnorm(_heads(qa_x, A_HEADS), q_norm)
    ya_x = _dense_attention(qa_x, ka_x, va_x) * jax.nn.silu(za_x)
    yb_x = _dense_attention(_heads(qb_x, B_HEADS), kb_x, vb_x) * jax.nn.silu(zb_x)
    yc_x = _rmsnorm(oc_x, o_norm).astype(hx.dtype).reshape(B, L, C_W) * jax.nn.silu(zc_x)
    y_x = jnp.concatenate([ya_x, yb_x, yc_x], axis=-1) @ w_out
    ctx_new = _layernorm(ALPHA * ctx + gate_x * y_x, ln_g, ln_b)
    return x_new, ctx_new


def setup_inputs(seed: int = 0) -> dict:
    key = jax.random.key(seed)
    ks = jax.random.split(key, 17)
    nrm = jax.random.normal
    x = nrm(ks[0], (BATCH, SEQ, D_MODEL), jnp.float32)
    c = nrm(ks[1], (BATCH, D_MODEL), jnp.float32)
    ctx = nrm(ks[2], (BATCH, CTX_LEN, D_MODEL), jnp.float32)
    c_ctx = nrm(ks[3], (D_MODEL,), jnp.float32)
    w_mod = nrm(ks[4], (DEPTH, D_MODEL, 3 * D_MODEL), jnp.float32) * D_MODEL ** -0.5
    b_mod = 0.02 * nrm(ks[5], (DEPTH, 3 * D_MODEL), jnp.float32)
    w_in = nrm(ks[6], (DEPTH, D_MODEL, IN_W), jnp.float32) * D_MODEL ** -0.5
    q_norm = 1.0 + 0.02 * nrm(ks[7], (DEPTH, HEAD_DIM), jnp.float32)
    k_norm = 1.0 + 0.02 * nrm(ks[8], (DEPTH, HEAD_DIM), jnp.float32)
    rpb = 0.02 * nrm(ks[9], (DEPTH, B_HEADS, 2 * NA_WIN_H - 1, 2 * NA_WIN_W - 1), jnp.float32)
    conv_w = nrm(ks[10], (DEPTH, CONV_K, 3 * C_W), jnp.float32) * CONV_K ** -0.5
    a_log = jnp.log(jax.random.uniform(ks[11], (DEPTH, 2, C_HEADS), jnp.float32, 1.0, 16.0))
    u = jax.random.uniform(ks[12], (DEPTH, 2, C_HEADS), jnp.float32)
    dt = jnp.exp(u * (math.log(0.1) - math.log(0.001)) + math.log(0.001))
    dt_bias = dt + jnp.log(-jnp.expm1(-dt))
    o_norm = 1.0 + 0.02 * nrm(ks[13], (DEPTH, HEAD_DIM), jnp.float32)
    w_out = nrm(ks[14], (DEPTH, MIX_W, D_MODEL), jnp.float32) * (MIX_W ** -0.5 * OUT_INIT)
    ln_g = 1.0 + 0.02 * nrm(ks[15], (DEPTH, D_MODEL), jnp.float32)
    ln_b = 0.02 * nrm(ks[16], (DEPTH, D_MODEL), jnp.float32)
    return {'x': x, 'c': c, 'ctx': ctx, 'c_ctx': c_ctx, 'w_mod': w_mod, 'b_mod': b_mod,
            'w_in': w_in, 'q_norm': q_norm, 'k_norm': k_norm, 'rpb': rpb, 'conv_w': conv_w,
            'a_log': a_log, 'dt_bias': dt_bias, 'o_norm': o_norm, 'w_out': w_out,
            'ln_g': ln_g, 'ln_b': ln_b}


def reference(x, c, ctx, c_ctx, w_mod, b_mod, w_in, q_norm, k_norm, rpb, conv_w, a_log, dt_bias,
              o_norm, w_out, ln_g, ln_b):
    n_lat = x.shape[1]
    rows = n_lat // GRID_W
    rope_cos, rope_sin = _axial_rope_tables(n_lat)
    for l in range(DEPTH):
        mod_lat = jax.nn.silu(c) @ w_mod[l] + b_mod[l]
        mod_ctx = jax.nn.silu(c_ctx) @ w_mod[l] + b_mod[l]
        x, ctx = _layer(x, ctx, mod_lat, mod_ctx, w_in[l], q_norm[l], k_norm[l], rpb[l], conv_w[l],
                        a_log[l], dt_bias[l], o_norm[l], w_out[l], ln_g[l], ln_b[l],
                        rope_cos, rope_sin, rows, l < DEPTH - 1)
    return x
```

```python
import functools

import jax
import jax.numpy as jnp
from jax import lax
from jax.experimental import pallas as pl
from jax.experimental.pallas import tpu as pltpu

F32 = jnp.float32
BF16 = jnp.bfloat16

HEAD_DIM = 128
GRID_W = 64
ROPE_THETA = 10000.0
NA_WIN_H = 8
NA_WIN_W = 16
CONV_K = 3
DN_CHUNK = 64
EPS = 1e-6
NEG = -1e30
MOD_ROWS = 8
NBR_Q_ROWS = 8
NBR_K_ROWS = 16
VMEM_LIMIT = 56 * 1024 * 1024

HIGHEST = lax.Precision.HIGHEST


def _silu(z):
    return z * jax.nn.sigmoid(z)


def _softplus(z):
    return jnp.maximum(z, 0.0) + jnp.log(1.0 + jnp.exp(-jnp.abs(z)))


def _dot(a, b, precision=None):
    return jnp.dot(a, b, preferred_element_type=F32, precision=precision)


def _dot_nt(a, b, precision=None):
    return lax.dot_general(a, b, (((1,), (1,)), ((), ())), preferred_element_type=F32,
                           precision=precision)


def _params(sem):
    return pltpu.CompilerParams(dimension_semantics=sem, vmem_limit_bytes=VMEM_LIMIT)


def _mod_kernel(c_ref, w_ref, b_ref, o_ref):
    s = _silu(c_ref[...]).astype(BF16)
    o_ref[0] = _dot(s, w_ref[0].astype(BF16)) + b_ref[0]


def _mod_call(cc, w_mod, b_mod):
    depth, d, n = w_mod.shape
    tn = 768
    return pl.pallas_call(
        _mod_kernel,
        grid=(depth, n // tn),
        in_specs=[pl.BlockSpec((MOD_ROWS, d), lambda l, j: (0, 0)),
                  pl.BlockSpec((1, d, tn), lambda l, j: (l, 0, j)),
                  pl.BlockSpec((1, 1, tn), lambda l, j: (l, 0, j))],
        out_specs=pl.BlockSpec((1, MOD_ROWS, tn), lambda l, j: (l, 0, j)),
        out_shape=jax.ShapeDtypeStruct((depth, MOD_ROWS, n), F32),
        compiler_params=_params(("arbitrary", "arbitrary")),
        name="mod",
    )(cc, w_mod, b_mod.reshape(depth, 1, n))


def _inproj_kernel(x_ref, shift_ref, scale_ref, w_ref, wab_ref, wabt_ref, o_ref, oab_ref, oabt_ref,
                   h_ref):
    @pl.when(pl.program_id(1) == 0)
    def _():
        x = x_ref[...]
        xc = x - jnp.mean(x, axis=-1, keepdims=True)
        y = xc * lax.rsqrt(jnp.mean(xc * xc, axis=-1, keepdims=True) + EPS)
        h = (y * (1.0 + scale_ref[0]) + shift_ref[0]).astype(BF16)
        h_ref[...] = h
        oab_ref[...] = _dot(h, wab_ref[...])
        oabt_ref[...] = _dot_nt(wabt_ref[...], h)

    o_ref[...] = _dot(h_ref[...], w_ref[...]).astype(o_ref.dtype)


def _inproj_call(x2d, mod_l, w_main, w_ab, w_abt, *, tiles_per_batch, ctx_row, tm, tn, out_dtype):
    m, d = x2d.shape
    n = w_main.shape[1]
    nab = w_abt.shape[0]
    if ctx_row is None:
        row = lambda i: i // tiles_per_batch
    else:
        row = lambda i: ctx_row
    return pl.pallas_call(
        _inproj_kernel,
        grid=(m // tm, n // tn),
        in_specs=[pl.BlockSpec((tm, d), lambda i, j: (i, 0)),
                  pl.BlockSpec((1, 1, d), lambda i, j: (row(i), 0, 0)),
                  pl.BlockSpec((1, 1, d), lambda i, j: (row(i), 0, 1)),
                  pl.BlockSpec((d, tn), lambda i, j: (0, j)),
                  pl.BlockSpec((d, HEAD_DIM), lambda i, j: (0, 0)),
                  pl.BlockSpec((nab, d), lambda i, j: (0, 0))],
        out_specs=[pl.BlockSpec((tm, tn), lambda i, j: (i, j)),
                   pl.BlockSpec((tm, HEAD_DIM), lambda i, j: (i, 0)),
                   pl.BlockSpec((nab, tm), lambda i, j: (0, i))],
        out_shape=[jax.ShapeDtypeStruct((m, n), out_dtype),
                   jax.ShapeDtypeStruct((m, HEAD_DIM), F32),
                   jax.ShapeDtypeStruct((nab, m), F32)],
        scratch_shapes=[pltpu.VMEM((tm, d), BF16)],
        compiler_params=_params(("arbitrary", "arbitrary")),
        name="inproj",
    )(x2d, mod_l, mod_l, w_main, w_ab, w_abt)


def _swap_quarters(x):
    lane = lax.broadcasted_iota(jnp.int32, x.shape, 1)
    fwd = pltpu.roll(x, 3 * HEAD_DIM // 4, axis=1)
    bwd = pltpu.roll(x, HEAD_DIM // 4, axis=1)
    return jnp.where((lane % (HEAD_DIM // 2)) < HEAD_DIM // 4, fwd, bwd)


def _aprep_kernel(*refs, nq, nk, rope):
    if rope:
        q_ref, k_ref, qw_ref, kw_ref, cos_ref, sin_ref, qo_ref, ko_ref = refs
        cos, sin = cos_ref[...], sin_ref[...]
    else:
        q_ref, k_ref, qw_ref, kw_ref, qo_ref, ko_ref = refs

    def norm(x, w):
        y = x * lax.rsqrt(jnp.mean(x * x, axis=-1, keepdims=True) + EPS) * w
        if rope:
            y = y * cos + _swap_quarters(y) * sin
        return y

    for h in range(nq):
        sl = slice(h * HEAD_DIM, (h + 1) * HEAD_DIM)
        y = norm(q_ref[:, sl].astype(F32), qw_ref[...])
        qo_ref[:, sl] = (y * HEAD_DIM ** -0.5).astype(qo_ref.dtype)
    for h in range(nk):
        sl = slice(h * HEAD_DIM, (h + 1) * HEAD_DIM)
        ko_ref[:, sl] = norm(k_ref[:, sl].astype(F32), kw_ref[...]).astype(ko_ref.dtype)


def _aprep_call(main, q_norm, k_norm, tables, *, col_q, col_k, nq, nk, tm, tiles_per_batch):
    m = main.shape[0]
    wq, wk = nq * HEAD_DIM, nk * HEAD_DIM
    in_specs = [pl.BlockSpec((tm, wq), lambda i: (i, col_q // wq)),
                pl.BlockSpec((tm, wk), lambda i: (i, col_k // wk)),
                pl.BlockSpec((1, HEAD_DIM), lambda i: (0, 0)),
                pl.BlockSpec((1, HEAD_DIM), lambda i: (0, 0))]
    args = [main, main, q_norm.reshape(1, HEAD_DIM), k_norm.reshape(1, HEAD_DIM)]
    if tables is not None:
        in_specs += [pl.BlockSpec((tm, HEAD_DIM), lambda i: (i % tiles_per_batch, 0))] * 2
        args += list(tables)
    return pl.pallas_call(
        functools.partial(_aprep_kernel, nq=nq, nk=nk, rope=tables is not None),
        grid=(m // tm,),
        in_specs=in_specs,
        out_specs=[pl.BlockSpec((tm, wq), lambda i: (i, 0)),
                   pl.BlockSpec((tm, wk), lambda i: (i, 0))],
        out_shape=[jax.ShapeDtypeStruct((m, wq), BF16), jax.ShapeDtypeStruct((m, wk), BF16)],
        compiler_params=_params(("arbitrary",)),
        name="aprep",
    )(*args)


def _rope_tables(n_tokens):
    t = jnp.arange(n_tokens, dtype=jnp.int32)
    row = (t // GRID_W).astype(F32)
    col = (t % GRID_W).astype(F32)
    half = HEAD_DIM // 2
    inv_freq = ROPE_THETA ** (-jnp.arange(0, half, 2, dtype=F32) / half)
    ar, ac = row[:, None] * inv_freq, col[:, None] * inv_freq
    cos = jnp.concatenate([jnp.cos(ar), jnp.cos(ar), jnp.cos(ac), jnp.cos(ac)], -1)
    sin = jnp.concatenate([-jnp.sin(ar), jnp.sin(ar), -jnp.sin(ac), jnp.sin(ac)], -1)
    return cos, sin


def _flash_kernel(*refs, group, tq, sources, q_scale):
    q_ref = refs[0]
    o_ref = refs[1 + 2 * len(sources)]
    q = jnp.concatenate([q_ref[0, :, g * HEAD_DIM:(g + 1) * HEAD_DIM] for g in range(group)], axis=0)
    if q_scale is not None:
        q = q.astype(F32) * q_scale
    q = q.astype(BF16)
    rows = group * tq

    def step(carry, k, v):
        m, l, acc = carry
        s = _dot_nt(q, k.astype(BF16))
        m_new = jnp.maximum(m, jnp.max(s, axis=-1, keepdims=True))
        alpha = jnp.exp(m - m_new)
        p = jnp.exp(s - m_new)
        l = alpha * l + jnp.sum(p, axis=-1, keepdims=True)
        acc = alpha * acc + _dot(p.astype(BF16), v.astype(BF16))
        return m_new, l, acc

    carry = (jnp.full((rows, 1), -jnp.inf, F32), jnp.zeros((rows, 1), F32),
             jnp.zeros((rows, HEAD_DIM), F32))
    for si, (length, tk) in enumerate(sources):
        k_ref, v_ref = refs[1 + 2 * si], refs[2 + 2 * si]
        if length == tk:
            carry = step(carry, k_ref[0], v_ref[0])
        else:
            def body(i, c, k_ref=k_ref, v_ref=v_ref, tk=tk):
                st = pl.multiple_of(i * tk, tk)
                return step(c, k_ref[0, pl.ds(st, tk), :], v_ref[0, pl.ds(st, tk), :])
            carry = lax.fori_loop(0, length // tk, body, carry)
    _, l, acc = carry
    o = acc / l
    for g in range(group):
        o_ref[0, :, g * HEAD_DIM:(g + 1) * HEAD_DIM] = o[g * tq:(g + 1) * tq].astype(o_ref.dtype)


def _flash_call(q, kv, *, n_kv_heads, group, q_col, tq, tk, q_scale, out_dtype):
    b, t_q, _ = q.shape
    wq = group * HEAD_DIM
    in_specs = [pl.BlockSpec((1, tq, wq), lambda bi, g, qi: (bi, qi, q_col // wq + g))]
    args = [q]
    sources = []
    for k_arr, k_col, v_arr, v_col in kv:
        t_k = k_arr.shape[1]
        in_specs.append(pl.BlockSpec((1, t_k, HEAD_DIM),
                                     lambda bi, g, qi, c=k_col // HEAD_DIM: (bi, 0, c + g)))
        in_specs.append(pl.BlockSpec((1, t_k, HEAD_DIM),
                                     lambda bi, g, qi, c=v_col // HEAD_DIM: (bi, 0, c + g)))
        args += [k_arr, v_arr]
        sources.append((t_k, min(tk, t_k)))
    return pl.pallas_call(
        functools.partial(_flash_kernel, group=group, tq=tq, sources=tuple(sources), q_scale=q_scale),
        grid=(b, n_kv_heads, t_q // tq),
        in_specs=in_specs,
        out_specs=pl.BlockSpec((1, tq, wq), lambda bi, g, qi: (bi, qi, g)),
        out_shape=jax.ShapeDtypeStruct((b, t_q, n_kv_heads * wq), out_dtype),
        compiler_params=_params(("arbitrary", "arbitrary", "arbitrary")),
        name="flash",
    )(*args)


def _nbr_window(cls, qr, kr):
    if cls == 0:
        lo, dr = max(qr - NA_WIN_H // 2, 0), kr - qr
    elif cls == 1:
        lo, dr = qr, kr - NA_WIN_H // 2 - qr
    else:
        lo, dr = min(NA_WIN_H // 2 + qr, NBR_K_ROWS - NA_WIN_H), kr - NBR_Q_ROWS - qr
    if lo <= kr < lo + NA_WIN_H:
        assert -NA_WIN_H < dr < NA_WIN_H
        return dr + NA_WIN_H - 1
    return None


def _bias_kernel(rpb_ref, o_ref, *, n_heads):
    nrow, ncol = 2 * NA_WIN_H - 1, 2 * NA_WIN_W - 1
    base = (pl.program_id(0) * n_heads + pl.program_id(1)) * (nrow * ncol)
    shape = (GRID_W, 2 * GRID_W)
    c = lax.broadcasted_iota(jnp.int32, shape, 0)
    lane = lax.broadcasted_iota(jnp.int32, shape, 1)
    right = lane >= GRID_W
    kc = jnp.where(right, lane - GRID_W, lane)
    c0 = jnp.clip(c - NA_WIN_W // 2, 0, GRID_W - NA_WIN_W)
    col_ok = (kc >= c0) & (kc < c0 + NA_WIN_W)
    idx = kc - c + (NA_WIN_W - 1)
    cache = {}

    def tile(a_left, a_right):
        key = (a_left, a_right)
        if key not in cache:
            t = jnp.full(shape, NEG, F32)
            if key != (None, None):
                for j in range(ncol):
                    sl = NEG if a_left is None else rpb_ref[base + a_left * ncol + j]
                    sr = NEG if a_right is None else rpb_ref[base + a_right * ncol + j]
                    t = jnp.where(idx == j, jnp.where(right, sr, sl), t)
                t = jnp.where(col_ok, t, NEG)
            cache[key] = t
        return cache[key]

    for cls in range(3):
        for qr in range(NBR_Q_ROWS):
            for p in range(NBR_K_ROWS // 2):
                t = tile(_nbr_window(cls, qr, 2 * p), _nbr_window(cls, qr, 2 * p + 1))
                o_ref[0, 0, cls, qr * GRID_W:(qr + 1) * GRID_W, p * 2 * GRID_W:(p + 1) * 2 * GRID_W] = t


def _bias_call(rpb):
    depth, n_heads = rpb.shape[:2]
    shape = (depth, n_heads, 3, NBR_Q_ROWS * GRID_W, NBR_K_ROWS * GRID_W)
    return pl.pallas_call(
        functools.partial(_bias_kernel, n_heads=n_heads),
        grid=(depth, n_heads),
        in_specs=[pl.BlockSpec(memory_space=pltpu.SMEM)],
        out_specs=pl.BlockSpec((1, 1) + shape[2:], lambda l, h: (l, h, 0, 0, 0)),
        out_shape=jax.ShapeDtypeStruct(shape, F32),
        compiler_params=_params(("arbitrary", "arbitrary")),
        name="nbr_bias",
    )(rpb.reshape(-1))


def _nbr_kernel(q_ref, k_ref, v_ref, kc_ref, vc_ref, bias_ref, o_ref, *, rows):
    g = pl.program_id(2)
    n_steps = rows // NBR_Q_ROWS
    start_row = jnp.clip(g * NBR_Q_ROWS - NA_WIN_H // 2, 0, rows - NBR_K_ROWS)
    start = pl.multiple_of(start_row * GRID_W, (NA_WIN_H // 2) * GRID_W)
    cls = jnp.where(g == 0, 0, jnp.where(g == n_steps - 1, 2, 1))
    nk = NBR_K_ROWS * GRID_W
    q = (q_ref[0].astype(F32) * HEAD_DIM ** -0.5).astype(BF16)
    k = k_ref[0, pl.ds(start, nk), :].astype(BF16)
    v = v_ref[0, pl.ds(start, nk), :].astype(BF16)
    s_win = _dot_nt(q, k) + bias_ref[0, 0, cls]
    s_ctx = _dot_nt(q, kc_ref[0].astype(BF16))
    m = jnp.maximum(jnp.max(s_win, axis=-1, keepdims=True), jnp.max(s_ctx, axis=-1, keepdims=True))
    p_win = jnp.exp(s_win - m)
    p_ctx = jnp.exp(s_ctx - m)
    l = jnp.sum(p_win, axis=-1, keepdims=True) + jnp.sum(p_ctx, axis=-1, keepdims=True)
    o = _dot(p_win.astype(BF16), v) + _dot(p_ctx.astype(BF16), vc_ref[0].astype(BF16))
    o_ref[0] = (o / l).astype(o_ref.dtype)


def _nbr_call(main, main_ctx, bias, *, layer, col_q, col_k, col_v, n_heads, out_dtype):
    b, s, _ = main.shape
    ctx_len = main_ctx.shape[1]
    rows = s // GRID_W
    tq = NBR_Q_ROWS * GRID_W
    cq, ck, cv = col_q // HEAD_DIM, col_k // HEAD_DIM, col_v // HEAD_DIM
    return pl.pallas_call(
        functools.partial(_nbr_kernel, rows=rows),
        grid=(n_heads, b, rows // NBR_Q_ROWS),
        in_specs=[pl.BlockSpec((1, tq, HEAD_DIM), lambda h, bi, g: (bi, g, cq + h)),
                  pl.BlockSpec((1, s, HEAD_DIM), lambda h, bi, g: (bi, 0, ck + h)),
                  pl.BlockSpec((1, s, HEAD_DIM), lambda h, bi, g: (bi, 0, cv + h)),
                  pl.BlockSpec((1, ctx_len, HEAD_DIM), lambda h, bi, g: (bi, 0, ck + h)),
                  pl.BlockSpec((1, ctx_len, HEAD_DIM), lambda h, bi, g: (bi, 0, cv + h)),
                  pl.BlockSpec((1, 1) + bias.shape[2:], lambda h, bi, g: (layer, h, 0, 0, 0))],
        out_specs=pl.BlockSpec((1, tq, HEAD_DIM), lambda h, bi, g: (bi, g, h)),
        out_shape=jax.ShapeDtypeStruct((b, s, n_heads * HEAD_DIM), out_dtype),
        compiler_params=_params(("arbitrary", "arbitrary", "arbitrary")),
        name="nbr_attn",
    )(main, main, main, main_ctx, main_ctx, bias)


def _dnprep_kernel(x_ref, prev_ref, next_ref, w_ref, q_ref, k_ref, v_ref, *, tiles_per_batch, n_heads):
    i = pl.program_id(0) % tiles_per_batch
    x = x_ref[...].astype(F32)
    tm = x.shape[0]
    row = lax.broadcasted_iota(jnp.int32, x.shape, 0)
    halo = prev_ref.shape[0]
    prev_row = jnp.where(i == 0, 0.0, prev_ref[halo - 1:halo, :].astype(F32))
    next_row = jnp.where(i == tiles_per_batch - 1, 0.0, next_ref[0:1, :].astype(F32))
    x_prev = jnp.where(row == 0, prev_row, pltpu.roll(x, 1, axis=0))
    x_next = jnp.where(row == tm - 1, next_row, pltpu.roll(x, tm - 1, axis=0))
    y = _silu(w_ref[0:1, :] * x_prev + w_ref[1:2, :] * x + w_ref[2:3, :] * x_next)
    w = n_heads * HEAD_DIM
    for h in range(n_heads):
        sl = slice(h * HEAD_DIM, (h + 1) * HEAD_DIM)
        qh = y[:, h * HEAD_DIM:(h + 1) * HEAD_DIM]
        kh = y[:, w + h * HEAD_DIM:w + (h + 1) * HEAD_DIM]
        q_ref[:, sl] = qh * lax.rsqrt(jnp.sum(qh * qh, axis=-1, keepdims=True) + EPS) * HEAD_DIM ** -0.5
        k_ref[:, sl] = kh * lax.rsqrt(jnp.sum(kh * kh, axis=-1, keepdims=True) + EPS)
    v_ref[...] = y[:, 2 * w:]


def _dnprep_call(main, conv_w, *, col, n_heads, tm, tiles_per_batch):
    m = main.shape[0]
    w3 = 3 * n_heads * HEAD_DIM
    halo = 16
    hb = tm // halo
    n_halo = m // halo
    out = jax.ShapeDtypeStruct((m, n_heads * HEAD_DIM), F32)
    o_spec = pl.BlockSpec((tm, n_heads * HEAD_DIM), lambda i: (i, 0))
    return pl.pallas_call(
        functools.partial(_dnprep_kernel, tiles_per_batch=tiles_per_batch, n_heads=n_heads),
        grid=(m // tm,),
        in_specs=[pl.BlockSpec((tm, w3), lambda i: (i, col // w3)),
                  pl.BlockSpec((halo, w3), lambda i: (jnp.maximum(i * hb - 1, 0), col // w3)),
                  pl.BlockSpec((halo, w3), lambda i: (jnp.minimum((i + 1) * hb, n_halo - 1), col // w3)),
                  pl.BlockSpec((CONV_K, w3), lambda i: (0, 0))],
        out_specs=[o_spec, o_spec, o_spec],
        out_shape=[out, out, out],
        compiler_params=_params(("arbitrary",)),
        name="dn_prep",
    )(main, main, main, conv_w)


def _tri_masks(n, upper):
    i = lax.broadcasted_iota(jnp.int32, (n, n), 0)
    j = lax.broadcasted_iota(jnp.int32, (n, n), 1)
    incl = (j >= i) if upper else (j <= i)
    strict = (j > i) if upper else (j < i)
    eye = (i == j).astype(F32)
    base = 8
    same_base = (i // base) == (j // base)
    levels = []
    s = base
    while s < n:
        levels.append(((i // (2 * s)) == (j // (2 * s))) & ((i // s) != (j // s)))
        s *= 2
    return incl, strict, eye, same_base, levels


def _unit_tri_inverse(lm, eye, same_base, levels):
    l1 = jnp.where(same_base, lm, 0.0)
    l2 = _dot(l1, l1, HIGHEST)
    l4 = _dot(l2, l2, HIGHEST)
    t = _dot(_dot(eye - l1, eye + l2, HIGHEST), eye + l4, HIGHEST)
    for lvl in levels:
        c = jnp.where(lvl, lm, 0.0)
        t = t - _dot(_dot(t, c, HIGHEST), t, HIGHEST)
    return t


def _gdn_kernel(qf_ref, kf_ref, vf_ref, qb_ref, kb_ref, vb_ref, abf_ref, abb_ref, abtf_ref, abtb_ref,
                alr_ref, dtr_ref, alc_ref, dtc_ref, s0_ref, of_ref, ob_ref, sfin_ref, s_ref, *, n_heads):
    n = pl.program_id(1)
    c = DN_CHUNK

    @pl.when(n == 0)
    def _():
        s_ref[...] = s0_ref[0]

    for d in range(2):
        upper = d == 1
        q_ref, k_ref, v_ref, ab_ref, abt_ref, o_ref = (
            (qf_ref, kf_ref, vf_ref, abf_ref, abtf_ref, of_ref) if d == 0 else
            (qb_ref, kb_ref, vb_ref, abb_ref, abtb_ref, ob_ref))
        incl, strict, eye, same_base, levels = _tri_masks(c, upper)
        ab = ab_ref[0]
        abt = abt_ref[0, 0]
        beta_tok = jax.nn.sigmoid(ab)
        g_tok = -jnp.exp(alr_ref[...]) * _softplus(ab + dtr_ref[...])
        g_row = -jnp.exp(alc_ref[...]) * _softplus(abt + dtc_ref[...])
        cum = incl.astype(F32)
        gc_tok = _dot(cum, g_tok, HIGHEST)
        gc_rows = _dot_nt(g_row, cum, HIGHEST)
        last = 0 if upper else c - 1
        for h in range(n_heads):
            sl = slice(h * HEAD_DIM, (h + 1) * HEAD_DIM)
            cb, cg = d * 2 * n_heads + h, d * 2 * n_heads + n_heads + h
            q, k, v = q_ref[0, :, sl], k_ref[0, :, sl], v_ref[0, :, sl]
            beta_c = beta_tok[:, cb:cb + 1]
            gc_c = gc_tok[:, cg:cg + 1]
            gc_r = gc_rows[cg:cg + 1, :]
            g_last = gc_c[last:last + 1, :]
            decay = jnp.where(incl, jnp.exp(jnp.where(incl, gc_c - gc_r, 0.0)), 0.0)
            kbeta = k * beta_c
            lm = jnp.where(strict, _dot_nt(kbeta, k, HIGHEST) * decay, 0.0)
            t = _unit_tri_inverse(lm, eye, same_base, levels)
            u = _dot(t, v * beta_c, HIGHEST)
            w = _dot(t, kbeta * jnp.exp(gc_c), HIGHEST)
            qk = _dot_nt(q, k, HIGHEST) * decay
            s = s_ref[d * n_heads + h]
            v_new = u - _dot(w, s, HIGHEST)
            o_ref[0, :, sl] = _dot(q * jnp.exp(gc_c), s, HIGHEST) + _dot(qk, v_new, HIGHEST)
            kd = k * jnp.exp(g_last - gc_c)
            s_ref[d * n_heads + h] = s * jnp.exp(g_last) + _dot(kd.T, v_new, HIGHEST)

    @pl.when(n == pl.num_programs(1) - 1)
    def _():
        sfin_ref[0] = s_ref[...]


def _gdn_call(qn, kn, vv, ab, abt, gate_consts, s0, *, n_heads):
    b, t, w = qn.shape
    n = t // DN_CHUNK
    fwd = lambda bi, i: (bi, i, 0)
    bwd = lambda bi, i: (bi, n - 1 - i, 0)
    tok = lambda im: pl.BlockSpec((1, DN_CHUNK, w), im)
    abs_ = lambda im: pl.BlockSpec((1, DN_CHUNK, HEAD_DIM), im)
    abts = lambda rev: pl.BlockSpec((1, 1) + abt.shape[2:],
                                    (lambda bi, i: (bi, n - 1 - i, 0, 0)) if rev else
                                    (lambda bi, i: (bi, i, 0, 0)))
    const = lambda a: pl.BlockSpec(a.shape, lambda bi, i: (0,) * a.ndim)
    st_spec = pl.BlockSpec((1,) + s0.shape[1:], lambda bi, i: (bi, 0, 0, 0))
    o_shape = jax.ShapeDtypeStruct((b, t, w), F32)
    return pl.pallas_call(
        functools.partial(_gdn_kernel, n_heads=n_heads),
        grid=(b, n),
        in_specs=[tok(fwd), tok(fwd), tok(fwd), tok(bwd), tok(bwd), tok(bwd),
                  abs_(fwd), abs_(bwd), abts(False), abts(True)]
                 + [const(a) for a in gate_consts] + [st_spec],
        out_specs=[tok(fwd), tok(bwd), st_spec],
        out_shape=[o_shape, o_shape, jax.ShapeDtypeStruct(s0.shape, F32)],
        scratch_shapes=[pltpu.VMEM(s0.shape[1:], F32)],
        compiler_params=_params(("arbitrary", "arbitrary")),
        name="gdn",
    )(qn, kn, vv, qn, kn, vv, ab, ab, abt, abt, *gate_consts, s0)


def _gate_consts(a_log, dt_bias):
    n_heads = a_log.shape[1]
    z = jnp.zeros((2, n_heads), F32)
    lay = lambda p: jnp.concatenate([z, p.astype(F32)], axis=1).reshape(-1)
    al, dt = lay(a_log), lay(dt_bias)
    pad = lambda vec: jnp.pad(vec, (0, HEAD_DIM - vec.shape[0])).reshape(1, HEAD_DIM)
    return pad(al), pad(dt), al.reshape(-1, 1), dt.reshape(-1, 1)


def _outproj_kernel(x_ref, gate_ref, ya_ref, za_ref, yb_ref, zb_ref, ocf_ref, ocb_ref, zc_ref, on_ref,
                    w_ref, lng_ref, lnb_ref, o_ref, mix_ref, *, alpha, wa, wb, n_c_heads):
    mix_ref[:, 0:wa] = (ya_ref[...].astype(F32) * _silu(za_ref[...].astype(F32))).astype(BF16)
    mix_ref[:, wa:wa + wb] = (yb_ref[...].astype(F32) * _silu(zb_ref[...].astype(F32))).astype(BF16)
    for h in range(n_c_heads):
        sl = slice(h * HEAD_DIM, (h + 1) * HEAD_DIM)
        oc = ocf_ref[:, sl] + ocb_ref[:, sl]
        yc = oc * lax.rsqrt(jnp.mean(oc * oc, axis=-1, keepdims=True) + EPS) * on_ref[...]
        off = wa + wb + h * HEAD_DIM
        mix_ref[:, off:off + HEAD_DIM] = (yc * _silu(zc_ref[:, sl].astype(F32))).astype(BF16)
    y = _dot(mix_ref[...], w_ref[...])
    r = alpha * x_ref[...] + gate_ref[0] * y
    rc = r - jnp.mean(r, axis=-1, keepdims=True)
    o_ref[...] = rc * lax.rsqrt(jnp.mean(rc * rc, axis=-1, keepdims=True) + EPS) * lng_ref[...] + lnb_ref[...]


def _outproj_call(x2d, mod_l, ya, main, yb, ocf, ocb, o_norm, w_out, ln_g, ln_b, *, col_za, col_zb,
                  col_zc, alpha, tiles_per_batch, ctx_row, tm):
    m, d = x2d.shape
    wa, wb, wc = ya.shape[1], yb.shape[1], ocf.shape[1]
    if ctx_row is None:
        row = lambda i: i // tiles_per_batch
    else:
        row = lambda i: ctx_row
    rows = lambda w, cb=0: pl.BlockSpec((tm, w), lambda i: (i, cb))
    one = lambda w: pl.BlockSpec((1, w), lambda i: (0, 0))
    return pl.pallas_call(
        functools.partial(_outproj_kernel, alpha=alpha, wa=wa, wb=wb, n_c_heads=wc // HEAD_DIM),
        grid=(m // tm,),
        in_specs=[rows(d), pl.BlockSpec((1, 1, d), lambda i: (row(i), 0, 2)),
                  rows(wa), rows(wa, col_za // wa), rows(wb), rows(wb, col_zb // wb),
                  rows(wc), rows(wc), rows(wc, col_zc // wc), one(HEAD_DIM),
                  pl.BlockSpec(w_out.shape, lambda i: (0, 0)), one(d), one(d)],
        out_specs=rows(d),
        out_shape=jax.ShapeDtypeStruct((m, d), F32),
        scratch_shapes=[pltpu.VMEM((tm, wa + wb + wc), BF16)],
        compiler_params=_params(("arbitrary",)),
        name="outproj",
    )(x2d, mod_l, ya, main, yb, main, ocf, ocb, main, o_norm.reshape(1, HEAD_DIM), w_out,
      ln_g.reshape(1, d), ln_b.reshape(1, d))


def kernel(x, c, ctx, c_ctx, w_mod, b_mod, w_in, q_norm, k_norm, rpb, conv_w, a_log, dt_bias, o_norm,
           w_out, ln_g, ln_b):
    bsz, seq, d = x.shape
    ctx_len = ctx.shape[1]
    depth = w_mod.shape[0]
    a_heads = d // (2 * HEAD_DIM)
    a_kv = a_heads // 4
    b_heads = d // (4 * HEAD_DIM)
    c_heads = d // (4 * HEAD_DIM)
    wa, wkv, wb, wc = a_heads * HEAD_DIM, a_kv * HEAD_DIM, b_heads * HEAD_DIM, c_heads * HEAD_DIM
    assert bsz < MOD_ROWS and seq % (NBR_Q_ROWS * GRID_W) == 0 and seq // GRID_W >= NBR_K_ROWS
    assert ctx_len % DN_CHUNK == 0 and 4 * c_heads <= 16
    alpha = (2 * depth) ** 0.25

    ref_sizes = dict(qa=wa, ka=wkv, va=wkv, za=wa, qb=wb, kb=wb, vb=wb, zb=wb, qkvc=3 * wc, zc=wc)
    ref_off, off = {}, 0
    for name, size in ref_sizes.items():
        ref_off[name] = off
        off += size
    ab_off, n_ab = off, 4 * c_heads
    order = ("qa", "za", "ka", "va", "qb", "kb", "vb", "zb", "qkvc", "zc")
    col, off = {}, 0
    for name in order:
        col[name] = off
        off += ref_sizes[name]
    n_main = off

    cc = jnp.concatenate([c, c_ctx[None], jnp.zeros((MOD_ROWS - bsz - 1, d), F32)], axis=0)
    mod = _mod_call(cc, w_mod, b_mod)
    tables = _rope_tables(seq)
    bias = _bias_call(rpb)

    tm_lat = 512
    tm_ctx = min(256, ctx_len)
    main_dtype = F32
    xl = x.reshape(bsz * seq, d)
    xc = ctx.reshape(bsz * ctx_len, d)
    for l in range(depth):
        with_ctx_out = l < depth - 1
        mod_l = mod[l].reshape(MOD_ROWS, 1, 3 * d)
        w_l = w_in[l]
        w_main = jnp.concatenate(
            [w_l[:, ref_off[nm]:ref_off[nm] + ref_sizes[nm]] for nm in order], axis=1).astype(BF16)
        w_ab = jnp.pad(w_l[:, ab_off:ab_off + n_ab], ((0, 0), (0, HEAD_DIM - n_ab))).astype(BF16)
        w_abt = w_l[:, ab_off:ab_off + n_ab].T.astype(BF16)
        w_out_l = w_out[l].astype(BF16)
        gate_consts = _gate_consts(a_log[l], dt_bias[l])

        main_x, ab_x, abt_x = _inproj_call(xc, mod_l, w_main, w_ab, w_abt, tiles_per_batch=None,
                                           ctx_row=bsz, tm=tm_ctx, tn=512, out_dtype=main_dtype)
        main, ab, abt = _inproj_call(xl, mod_l, w_main, w_ab, w_abt, tiles_per_batch=seq // tm_lat,
                                     ctx_row=None, tm=tm_lat, tn=512, out_dtype=main_dtype)
        main3 = main.reshape(bsz, seq, n_main)
        main_x3 = main_x.reshape(bsz, ctx_len, n_main)

        qa_x, ka_x = _aprep_call(main_x, q_norm[l], k_norm[l], None, col_q=col["qa"], col_k=col["ka"],
                                 nq=a_heads, nk=a_kv, tm=tm_ctx, tiles_per_batch=ctx_len // tm_ctx)
        qa, ka = _aprep_call(main, q_norm[l], k_norm[l], tables, col_q=col["qa"], col_k=col["ka"],
                             nq=a_heads, nk=a_kv, tm=tm_lat, tiles_per_batch=seq // tm_lat)
        ka_x3 = ka_x.reshape(bsz, ctx_len, wkv)
        ya = _flash_call(qa.reshape(bsz, seq, wa),
                         [(ka_x3, 0, main_x3, col["va"]), (ka.reshape(bsz, seq, wkv), 0, main3, col["va"])],
                         n_kv_heads=a_kv, group=a_heads // a_kv, q_col=0, tq=256, tk=512, q_scale=None,
                         out_dtype=F32)

        yb = _nbr_call(main3, main_x3, bias, layer=l, col_q=col["qb"], col_k=col["kb"], col_v=col["vb"],
                       n_heads=b_heads, out_dtype=F32)

        def delta(main2d, ab2d, abt2d, t_len, tm, s0):
            qn, kn, vv = _dnprep_call(main2d, conv_w[l], col=col["qkvc"], n_heads=c_heads, tm=tm,
                                      tiles_per_batch=t_len // tm)
            n = t_len // DN_CHUNK
            abt4 = abt2d.reshape(n_ab, bsz, n, DN_CHUNK).transpose(1, 2, 0, 3)
            r3 = lambda a: a.reshape(bsz, t_len, a.shape[-1])
            return _gdn_call(r3(qn), r3(kn), r3(vv), r3(ab2d), abt4, gate_consts, s0, n_heads=c_heads)

        s_zero = jnp.zeros((bsz, 2 * c_heads, HEAD_DIM, HEAD_DIM), F32)
        ocf_x, ocb_x, s_ctx = delta(main_x, ab_x, abt_x, ctx_len, tm_ctx, s_zero)
        ocf, ocb, _ = delta(main, ab, abt, seq, tm_lat, s_ctx)

        x_new = _outproj_call(xl, mod_l, ya.reshape(bsz * seq, wa), main, yb.reshape(bsz * seq, wb),
                              ocf.reshape(bsz * seq, wc), ocb.reshape(bsz * seq, wc), o_norm[l], w_out_l,
                              ln_g[l], ln_b[l], col_za=col["za"], col_zb=col["zb"], col_zc=col["zc"],
                              alpha=alpha, tiles_per_batch=seq // 256, ctx_row=None, tm=256)

        if with_ctx_out:
            ya_x = _flash_call(qa_x.reshape(bsz, ctx_len, wa), [(ka_x3, 0, main_x3, col["va"])],
                               n_kv_heads=a_kv, group=a_heads // a_kv, q_col=0, tq=ctx_len, tk=ctx_len,
                               q_scale=None, out_dtype=F32)
            yb_x = _flash_call(main_x3, [(main_x3, col["kb"], main_x3, col["vb"])],
                               n_kv_heads=b_heads, group=1, q_col=col["qb"], tq=ctx_len, tk=ctx_len,
                               q_scale=HEAD_DIM ** -0.5, out_dtype=F32)
            xc = _outproj_call(xc, mod_l, ya_x.reshape(bsz * ctx_len, wa), main_x,
                               yb_x.reshape(bsz * ctx_len, wb), ocf_x.reshape(bsz * ctx_len, wc),
                               ocb_x.reshape(bsz * ctx_len, wc), o_norm[l], w_out_l, ln_g[l], ln_b[l],
                               col_za=col["za"], col_zb=col["zb"], col_zc=col["zc"], alpha=alpha,
                               tiles_per_batch=None, ctx_row=bsz, tm=tm_ctx)
        xl = x_new
    return xl.reshape(bsz, seq, d)
```

```python
import functools

import jax
import jax.numpy as jnp
from jax import lax
from jax.experimental import pallas as pl
from jax.experimental.pallas import tpu as pltpu

F32 = jnp.float32
BF16 = jnp.bfloat16

HEAD_DIM = 128
GRID_W = 64
ROPE_THETA = 10000.0
NA_WIN_H = 8
NA_WIN_W = 16
CONV_K = 3
DN_CHUNK = 64
EPS = 1e-6
NEG = -1e30
MOD_ROWS = 8
NBR_Q_ROWS = 8
NBR_K_ROWS = 16
VMEM_LIMIT = 56 * 1024 * 1024

HIGHEST = lax.Precision.HIGHEST
INV_PRECISION = None


def _silu(z):
    return z * jax.nn.sigmoid(z)


def _softplus(z):
    return jnp.maximum(z, 0.0) + jnp.log(1.0 + jnp.exp(-jnp.abs(z)))


def _dot(a, b, precision=None):
    return jnp.dot(a, b, preferred_element_type=F32, precision=precision)


def _dot_nt(a, b, precision=None):
    return lax.dot_general(a, b, (((1,), (1,)), ((), ())), preferred_element_type=F32,
                           precision=precision)


def _params(sem):
    return pltpu.CompilerParams(dimension_semantics=sem, vmem_limit_bytes=VMEM_LIMIT)


def _mod_kernel(c_ref, w_ref, b_ref, o_ref):
    s = _silu(c_ref[...]).astype(BF16)
    o_ref[0] = _dot(s, w_ref[0].astype(BF16)) + b_ref[0]


def _mod_call(cc, w_mod, b_mod):
    depth, d, n = w_mod.shape
    tn = 768
    return pl.pallas_call(
        _mod_kernel,
        grid=(depth, n // tn),
        in_specs=[pl.BlockSpec((MOD_ROWS, d), lambda l, j: (0, 0)),
                  pl.BlockSpec((1, d, tn), lambda l, j: (l, 0, j)),
                  pl.BlockSpec((1, 1, tn), lambda l, j: (l, 0, j))],
        out_specs=pl.BlockSpec((1, MOD_ROWS, tn), lambda l, j: (l, 0, j)),
        out_shape=jax.ShapeDtypeStruct((depth, MOD_ROWS, n), F32),
        compiler_params=_params(("arbitrary", "arbitrary")),
        name="mod",
    )(cc, w_mod, b_mod.reshape(depth, 1, n))


def _inproj_kernel(x_ref, shift_ref, scale_ref, w_ref, wab_ref, wabt_ref, o_ref, oab_ref, oabt_ref,
                   h_ref):
    @pl.when(pl.program_id(1) == 0)
    def _():
        x = x_ref[...]
        xc = x - jnp.mean(x, axis=-1, keepdims=True)
        y = xc * lax.rsqrt(jnp.mean(xc * xc, axis=-1, keepdims=True) + EPS)
        h = (y * (1.0 + scale_ref[0]) + shift_ref[0]).astype(BF16)
        h_ref[...] = h
        oab_ref[...] = _dot(h, wab_ref[...])
        oabt_ref[...] = _dot_nt(wabt_ref[...], h)

    o_ref[...] = _dot(h_ref[...], w_ref[...]).astype(o_ref.dtype)


def _inproj_call(x2d, mod_l, w_main, w_ab, w_abt, *, tiles_per_batch, ctx_row, tm, tn, out_dtype):
    m, d = x2d.shape
    n = w_main.shape[1]
    nab = w_abt.shape[0]
    if ctx_row is None:
        row = lambda i: i // tiles_per_batch
    else:
        row = lambda i: ctx_row
    return pl.pallas_call(
        _inproj_kernel,
        grid=(m // tm, n // tn),
        in_specs=[pl.BlockSpec((tm, d), lambda i, j: (i, 0)),
                  pl.BlockSpec((1, 1, d), lambda i, j: (row(i), 0, 0)),
                  pl.BlockSpec((1, 1, d), lambda i, j: (row(i), 0, 1)),
                  pl.BlockSpec((d, tn), lambda i, j: (0, j)),
                  pl.BlockSpec((d, HEAD_DIM), lambda i, j: (0, 0)),
                  pl.BlockSpec((nab, d), lambda i, j: (0, 0))],
        out_specs=[pl.BlockSpec((tm, tn), lambda i, j: (i, j)),
                   pl.BlockSpec((tm, HEAD_DIM), lambda i, j: (i, 0)),
                   pl.BlockSpec((nab, tm), lambda i, j: (0, i))],
        out_shape=[jax.ShapeDtypeStruct((m, n), out_dtype),
                   jax.ShapeDtypeStruct((m, HEAD_DIM), F32),
                   jax.ShapeDtypeStruct((nab, m), F32)],
        scratch_shapes=[pltpu.VMEM((tm, d), BF16)],
        compiler_params=_params(("arbitrary", "arbitrary")),
        name="inproj",
    )(x2d, mod_l, mod_l, w_main, w_ab, w_abt)


def _swap_quarters(x):
    lane = lax.broadcasted_iota(jnp.int32, x.shape, 1)
    fwd = pltpu.roll(x, 3 * HEAD_DIM // 4, axis=1)
    bwd = pltpu.roll(x, HEAD_DIM // 4, axis=1)
    return jnp.where((lane % (HEAD_DIM // 2)) < HEAD_DIM // 4, fwd, bwd)


def _aprep_kernel(*refs, nq, nk, rope):
    if rope:
        q_ref, k_ref, qw_ref, kw_ref, cos_ref, sin_ref, qo_ref, ko_ref = refs
        cos, sin = cos_ref[...], sin_ref[...]
    else:
        q_ref, k_ref, qw_ref, kw_ref, qo_ref, ko_ref = refs

    def norm(x, w):
        y = x * lax.rsqrt(jnp.mean(x * x, axis=-1, keepdims=True) + EPS) * w
        if rope:
            y = y * cos + _swap_quarters(y) * sin
        return y

    for h in range(nq):
        sl = slice(h * HEAD_DIM, (h + 1) * HEAD_DIM)
        y = norm(q_ref[:, sl].astype(F32), qw_ref[...])
        qo_ref[:, sl] = (y * HEAD_DIM ** -0.5).astype(qo_ref.dtype)
    for h in range(nk):
        sl = slice(h * HEAD_DIM, (h + 1) * HEAD_DIM)
        ko_ref[:, sl] = norm(k_ref[:, sl].astype(F32), kw_ref[...]).astype(ko_ref.dtype)


def _aprep_call(main, q_norm, k_norm, tables, *, col_q, col_k, nq, nk, tm, tiles_per_batch):
    m = main.shape[0]
    wq, wk = nq * HEAD_DIM, nk * HEAD_DIM
    in_specs = [pl.BlockSpec((tm, wq), lambda i: (i, col_q // wq)),
                pl.BlockSpec((tm, wk), lambda i: (i, col_k // wk)),
                pl.BlockSpec((1, HEAD_DIM), lambda i: (0, 0)),
                pl.BlockSpec((1, HEAD_DIM), lambda i: (0, 0))]
    args = [main, main, q_norm.reshape(1, HEAD_DIM), k_norm.reshape(1, HEAD_DIM)]
    if tables is not None:
        in_specs += [pl.BlockSpec((tm, HEAD_DIM), lambda i: (i % tiles_per_batch, 0))] * 2
        args += list(tables)
    return pl.pallas_call(
        functools.partial(_aprep_kernel, nq=nq, nk=nk, rope=tables is not None),
        grid=(m // tm,),
        in_specs=in_specs,
        out_specs=[pl.BlockSpec((tm, wq), lambda i: (i, 0)),
                   pl.BlockSpec((tm, wk), lambda i: (i, 0))],
        out_shape=[jax.ShapeDtypeStruct((m, wq), BF16), jax.ShapeDtypeStruct((m, wk), BF16)],
        compiler_params=_params(("arbitrary",)),
        name="aprep",
    )(*args)


def _rope_tables(n_tokens):
    t = jnp.arange(n_tokens, dtype=jnp.int32)
    row = (t // GRID_W).astype(F32)
    col = (t % GRID_W).astype(F32)
    half = HEAD_DIM // 2
    inv_freq = ROPE_THETA ** (-jnp.arange(0, half, 2, dtype=F32) / half)
    ar, ac = row[:, None] * inv_freq, col[:, None] * inv_freq
    cos = jnp.concatenate([jnp.cos(ar), jnp.cos(ar), jnp.cos(ac), jnp.cos(ac)], -1)
    sin = jnp.concatenate([-jnp.sin(ar), jnp.sin(ar), -jnp.sin(ac), jnp.sin(ac)], -1)
    return cos, sin


def _flash_kernel(*refs, group, tq, sources, q_scale):
    q_ref = refs[0]
    o_ref = refs[1 + 2 * len(sources)]
    q = jnp.concatenate([q_ref[0, :, g * HEAD_DIM:(g + 1) * HEAD_DIM] for g in range(group)], axis=0)
    if q_scale is not None:
        q = q.astype(F32) * q_scale
    q = q.astype(BF16)
    rows = group * tq

    def step(carry, k, v):
        m, l, acc = carry
        s = _dot_nt(q, k.astype(BF16))
        m_new = jnp.maximum(m, jnp.max(s, axis=-1, keepdims=True))
        alpha = jnp.exp(m - m_new)
        p = jnp.exp(s - m_new)
        l = alpha * l + jnp.sum(p, axis=-1, keepdims=True)
        acc = alpha * acc + _dot(p.astype(BF16), v.astype(BF16))
        return m_new, l, acc

    carry = (jnp.full((rows, 1), -jnp.inf, F32), jnp.zeros((rows, 1), F32),
             jnp.zeros((rows, HEAD_DIM), F32))
    for si, (length, tk) in enumerate(sources):
        k_ref, v_ref = refs[1 + 2 * si], refs[2 + 2 * si]
        if length == tk:
            carry = step(carry, k_ref[0], v_ref[0])
        else:
            def body(i, c, k_ref=k_ref, v_ref=v_ref, tk=tk):
                st = pl.multiple_of(i * tk, tk)
                return step(c, k_ref[0, pl.ds(st, tk), :], v_ref[0, pl.ds(st, tk), :])
            carry = lax.fori_loop(0, length // tk, body, carry)
    _, l, acc = carry
    o = acc / l
    for g in range(group):
        o_ref[0, :, g * HEAD_DIM:(g + 1) * HEAD_DIM] = o[g * tq:(g + 1) * tq].astype(o_ref.dtype)


def _flash_call(q, kv, *, n_kv_heads, group, q_col, tq, tk, q_scale, out_dtype):
    b, t_q, _ = q.shape
    wq = group * HEAD_DIM
    in_specs = [pl.BlockSpec((1, tq, wq), lambda bi, g, qi: (bi, qi, q_col // wq + g))]
    args = [q]
    sources = []
    for k_arr, k_col, v_arr, v_col in kv:
        t_k = k_arr.shape[1]
        in_specs.append(pl.BlockSpec((1, t_k, HEAD_DIM),
                                     lambda bi, g, qi, c=k_col // HEAD_DIM: (bi, 0, c + g)))
        in_specs.append(pl.BlockSpec((1, t_k, HEAD_DIM),
                                     lambda bi, g, qi, c=v_col // HEAD_DIM: (bi, 0, c + g)))
        args += [k_arr, v_arr]
        sources.append((t_k, min(tk, t_k)))
    return pl.pallas_call(
        functools.partial(_flash_kernel, group=group, tq=tq, sources=tuple(sources), q_scale=q_scale),
        grid=(b, n_kv_heads, t_q // tq),
        in_specs=in_specs,
        out_specs=pl.BlockSpec((1, tq, wq), lambda bi, g, qi: (bi, qi, g)),
        out_shape=jax.ShapeDtypeStruct((b, t_q, n_kv_heads * wq), out_dtype),
        compiler_params=_params(("arbitrary", "arbitrary", "arbitrary")),
        name="flash",
    )(*args)


def _nbr_window(cls, qr, kr):
    if cls == 0:
        lo, dr = max(qr - NA_WIN_H // 2, 0), kr - qr
    elif cls == 1:
        lo, dr = qr, kr - NA_WIN_H // 2 - qr
    else:
        lo, dr = min(NA_WIN_H // 2 + qr, NBR_K_ROWS - NA_WIN_H), kr - NBR_Q_ROWS - qr
    if lo <= kr < lo + NA_WIN_H:
        assert -NA_WIN_H < dr < NA_WIN_H
        return dr + NA_WIN_H - 1
    return None


def _bias_kernel(rpb_ref, o_ref, *, n_heads):
    nrow, ncol = 2 * NA_WIN_H - 1, 2 * NA_WIN_W - 1
    base = (pl.program_id(0) * n_heads + pl.program_id(1)) * (nrow * ncol)
    shape = (GRID_W, 2 * GRID_W)
    c = lax.broadcasted_iota(jnp.int32, shape, 0)
    lane = lax.broadcasted_iota(jnp.int32, shape, 1)
    right = lane >= GRID_W
    kc = jnp.where(right, lane - GRID_W, lane)
    c0 = jnp.clip(c - NA_WIN_W // 2, 0, GRID_W - NA_WIN_W)
    col_ok = (kc >= c0) & (kc < c0 + NA_WIN_W)
    idx = kc - c + (NA_WIN_W - 1)
    cache = {}

    def tile(a_left, a_right):
        key = (a_left, a_right)
        if key not in cache:
            t = jnp.full(shape, NEG, F32)
            if key != (None, None):
                for j in range(ncol):
                    sl = NEG if a_left is None else rpb_ref[base + a_left * ncol + j]
                    sr = NEG if a_right is None else rpb_ref[base + a_right * ncol + j]
                    t = jnp.where(idx == j, jnp.where(right, sr, sl), t)
                t = jnp.where(col_ok, t, NEG)
            cache[key] = t
        return cache[key]

    for cls in range(3):
        for qr in range(NBR_Q_ROWS):
            for p in range(NBR_K_ROWS // 2):
                t = tile(_nbr_window(cls, qr, 2 * p), _nbr_window(cls, qr, 2 * p + 1))
                o_ref[0, 0, cls, qr * GRID_W:(qr + 1) * GRID_W, p * 2 * GRID_W:(p + 1) * 2 * GRID_W] = t


def _bias_call(rpb):
    depth, n_heads = rpb.shape[:2]
    shape = (depth, n_heads, 3, NBR_Q_ROWS * GRID_W, NBR_K_ROWS * GRID_W)
    return pl.pallas_call(
        functools.partial(_bias_kernel, n_heads=n_heads),
        grid=(depth, n_heads),
        in_specs=[pl.BlockSpec(memory_space=pltpu.SMEM)],
        out_specs=pl.BlockSpec((1, 1) + shape[2:], lambda l, h: (l, h, 0, 0, 0)),
        out_shape=jax.ShapeDtypeStruct(shape, F32),
        compiler_params=_params(("arbitrary", "arbitrary")),
        name="nbr_bias",
    )(rpb.reshape(-1))


def _nbr_kernel(q_ref, k_ref, v_ref, kc_ref, vc_ref, bias_ref, o_ref, *, rows):
    g = pl.program_id(2)
    n_steps = rows // NBR_Q_ROWS
    start_row = jnp.clip(g * NBR_Q_ROWS - NA_WIN_H // 2, 0, rows - NBR_K_ROWS)
    start = pl.multiple_of(start_row * GRID_W, (NA_WIN_H // 2) * GRID_W)
    cls = jnp.where(g == 0, 0, jnp.where(g == n_steps - 1, 2, 1))
    nk = NBR_K_ROWS * GRID_W
    q = (q_ref[0].astype(F32) * HEAD_DIM ** -0.5).astype(BF16)
    k = k_ref[0, pl.ds(start, nk), :].astype(BF16)
    v = v_ref[0, pl.ds(start, nk), :].astype(BF16)
    s_win = _dot_nt(q, k) + bias_ref[0, 0, cls]
    s_ctx = _dot_nt(q, kc_ref[0].astype(BF16))
    m = jnp.maximum(jnp.max(s_win, axis=-1, keepdims=True), jnp.max(s_ctx, axis=-1, keepdims=True))
    p_win = jnp.exp(s_win - m)
    p_ctx = jnp.exp(s_ctx - m)
    l = jnp.sum(p_win, axis=-1, keepdims=True) + jnp.sum(p_ctx, axis=-1, keepdims=True)
    o = _dot(p_win.astype(BF16), v) + _dot(p_ctx.astype(BF16), vc_ref[0].astype(BF16))
    o_ref[0] = (o / l).astype(o_ref.dtype)


def _nbr_call(main, main_ctx, bias, *, layer, col_q, col_k, col_v, n_heads, out_dtype):
    b, s, _ = main.shape
    ctx_len = main_ctx.shape[1]
    rows = s // GRID_W
    tq = NBR_Q_ROWS * GRID_W
    cq, ck, cv = col_q // HEAD_DIM, col_k // HEAD_DIM, col_v // HEAD_DIM
    return pl.pallas_call(
        functools.partial(_nbr_kernel, rows=rows),
        grid=(n_heads, b, rows // NBR_Q_ROWS),
        in_specs=[pl.BlockSpec((1, tq, HEAD_DIM), lambda h, bi, g: (bi, g, cq + h)),
                  pl.BlockSpec((1, s, HEAD_DIM), lambda h, bi, g: (bi, 0, ck + h)),
                  pl.BlockSpec((1, s, HEAD_DIM), lambda h, bi, g: (bi, 0, cv + h)),
                  pl.BlockSpec((1, ctx_len, HEAD_DIM), lambda h, bi, g: (bi, 0, ck + h)),
                  pl.BlockSpec((1, ctx_len, HEAD_DIM), lambda h, bi, g: (bi, 0, cv + h)),
                  pl.BlockSpec((1, 1) + bias.shape[2:], lambda h, bi, g: (layer, h, 0, 0, 0))],
        out_specs=pl.BlockSpec((1, tq, HEAD_DIM), lambda h, bi, g: (bi, g, h)),
        out_shape=jax.ShapeDtypeStruct((b, s, n_heads * HEAD_DIM), out_dtype),
        compiler_params=_params(("arbitrary", "arbitrary", "arbitrary")),
        name="nbr_attn",
    )(main, main, main, main_ctx, main_ctx, bias)


def _dnprep_kernel(x_ref, prev_ref, next_ref, w_ref, q_ref, k_ref, v_ref, *, tiles_per_batch, n_heads):
    i = pl.program_id(0) % tiles_per_batch
    x = x_ref[...].astype(F32)
    tm = x.shape[0]
    row = lax.broadcasted_iota(jnp.int32, x.shape, 0)
    halo = prev_ref.shape[0]
    prev_row = jnp.where(i == 0, 0.0, prev_ref[halo - 1:halo, :].astype(F32))
    next_row = jnp.where(i == tiles_per_batch - 1, 0.0, next_ref[0:1, :].astype(F32))
    x_prev = jnp.where(row == 0, prev_row, pltpu.roll(x, 1, axis=0))
    x_next = jnp.where(row == tm - 1, next_row, pltpu.roll(x, tm - 1, axis=0))
    y = _silu(w_ref[0:1, :] * x_prev + w_ref[1:2, :] * x + w_ref[2:3, :] * x_next)
    w = n_heads * HEAD_DIM
    for h in range(n_heads):
        sl = slice(h * HEAD_DIM, (h + 1) * HEAD_DIM)
        qh = y[:, h * HEAD_DIM:(h + 1) * HEAD_DIM]
        kh = y[:, w + h * HEAD_DIM:w + (h + 1) * HEAD_DIM]
        q_ref[:, sl] = qh * lax.rsqrt(jnp.sum(qh * qh, axis=-1, keepdims=True) + EPS) * HEAD_DIM ** -0.5
        k_ref[:, sl] = kh * lax.rsqrt(jnp.sum(kh * kh, axis=-1, keepdims=True) + EPS)
    v_ref[...] = y[:, 2 * w:]


def _dnprep_call(main, conv_w, *, col, n_heads, tm, tiles_per_batch):
    m = main.shape[0]
    w3 = 3 * n_heads * HEAD_DIM
    halo = 16
    hb = tm // halo
    n_halo = m // halo
    out = jax.ShapeDtypeStruct((m, n_heads * HEAD_DIM), F32)
    o_spec = pl.BlockSpec((tm, n_heads * HEAD_DIM), lambda i: (i, 0))
    return pl.pallas_call(
        functools.partial(_dnprep_kernel, tiles_per_batch=tiles_per_batch, n_heads=n_heads),
        grid=(m // tm,),
        in_specs=[pl.BlockSpec((tm, w3), lambda i: (i, col // w3)),
                  pl.BlockSpec((halo, w3), lambda i: (jnp.maximum(i * hb - 1, 0), col // w3)),
                  pl.BlockSpec((halo, w3), lambda i: (jnp.minimum((i + 1) * hb, n_halo - 1), col // w3)),
                  pl.BlockSpec((CONV_K, w3), lambda i: (0, 0))],
        out_specs=[o_spec, o_spec, o_spec],
        out_shape=[out, out, out],
        compiler_params=_params(("arbitrary",)),
        name="dn_prep",
    )(main, main, main, conv_w)


def _tri_masks(n, upper):
    i = lax.broadcasted_iota(jnp.int32, (n, n), 0)
    j = lax.broadcasted_iota(jnp.int32, (n, n), 1)
    incl = (j >= i) if upper else (j <= i)
    strict = (j > i) if upper else (j < i)
    eye = (i == j).astype(F32)
    base = 8
    same_base = (i // base) == (j // base)
    levels = []
    s = base
    while s < n:
        levels.append(((i // (2 * s)) == (j // (2 * s))) & ((i // s) != (j // s)))
        s *= 2
    return incl, strict, eye, same_base, levels


def _unit_tri_inverse(lms, masks, precision):
    mm = lambda a, b: _dot(a, b, precision) if precision is not None else _dot(a.astype(BF16), b.astype(BF16))
    eyes = [m[2] for m in masks]
    l1 = [jnp.where(m[3], lm, 0.0) for lm, m in zip(lms, masks)]
    l2 = [mm(a, a) for a in l1]
    l4 = [mm(a, a) for a in l2]
    t = [mm(e - a, e + b) for e, a, b in zip(eyes, l1, l2)]
    t = [mm(a, e + b) for e, a, b in zip(eyes, t, l4)]
    for lvl in range(len(masks[0][4])):
        cs = [jnp.where(m[4][lvl], lm, 0.0) for lm, m in zip(lms, masks)]
        tc = [mm(a, b) for a, b in zip(t, cs)]
        t = [a - mm(b, a) for a, b in zip(t, tc)]
    return t


def _gdn_local_kernel(q_ref, k_ref, v_ref, ab_ref, abt_ref, alr_ref, dtr_ref, alc_ref, dtc_ref,
                      wq_ref, u_ref, kdt_ref, qk_ref, eg_ref, *, n_heads, n_chunks, inv_precision):
    c = DN_CHUNK
    masks = [_tri_masks(c, upper) for upper in (False, True)]
    gates = {}
    for d in range(2):
        cum = masks[d][0].astype(F32)
        for ci in range(n_chunks):
            ab = ab_ref[0, ci * c:(ci + 1) * c, :]
            abt = abt_ref[0, ci]
            g_tok = -jnp.exp(alr_ref[...]) * _softplus(ab + dtr_ref[...])
            g_row = -jnp.exp(alc_ref[...]) * _softplus(abt + dtc_ref[...])
            gates[ci, d] = (jax.nn.sigmoid(ab), _dot(cum, g_tok, HIGHEST), _dot_nt(g_row, cum, HIGHEST))

    streams = [(ci, d, h) for ci in range(n_chunks) for d in range(2) for h in range(n_heads)]
    pre = []
    for ci, d, h in streams:
        incl, strict = masks[d][0], masks[d][1]
        rows = slice(ci * c, (ci + 1) * c)
        sl = slice(h * HEAD_DIM, (h + 1) * HEAD_DIM)
        cb, cg = d * 2 * n_heads + h, d * 2 * n_heads + n_heads + h
        beta_tok, gc_tok, gc_rows = gates[ci, d]
        q, k, v = q_ref[0, rows, sl], k_ref[0, rows, sl], v_ref[0, rows, sl]
        beta_c = beta_tok[:, cb:cb + 1]
        gc_c = gc_tok[:, cg:cg + 1]
        last = 0 if d == 1 else c - 1
        g_last = gc_c[last:last + 1, :]
        decay = jnp.where(incl, jnp.exp(jnp.where(incl, gc_c - gc_rows[cg:cg + 1, :], 0.0)), 0.0)
        kbeta = k * beta_c
        e_c = jnp.exp(gc_c)
        wq_ref[0, d, ci, c:2 * c, sl] = (q * e_c).astype(wq_ref.dtype)
        kdt_ref[0, d, ci, h] = (k * jnp.exp(g_last - gc_c)).T.astype(kdt_ref.dtype)
        eg_ref[0, d, ci, h:h + 1, :] = jnp.broadcast_to(jnp.exp(g_last), (1, HEAD_DIM))
        lhs = jnp.concatenate([kbeta, q], axis=0).astype(BF16)
        rhs = jnp.concatenate([v * beta_c, kbeta * e_c], axis=1).astype(BF16)
        pre.append((lhs, k.astype(BF16), rhs, decay))
    kq = [_dot_nt(lhs, kb) for lhs, kb, _, _ in pre]
    lms = []
    for (ci, d, h), x, (_, _, _, decay) in zip(streams, kq, pre):
        qk_ref[0, d, ci, h] = (x[c:] * decay).astype(qk_ref.dtype)
        lms.append(jnp.where(masks[d][1], x[:c] * decay, 0.0))
    ts = _unit_tri_inverse(lms, [masks[d] for _, d, _ in streams], inv_precision)
    uws = [_dot(t.astype(BF16), rhs) for t, (_, _, rhs, _) in zip(ts, pre)]
    for (ci, d, h), uw in zip(streams, uws):
        sl = slice(h * HEAD_DIM, (h + 1) * HEAD_DIM)
        u_ref[0, d, ci * c:(ci + 1) * c, sl] = uw[:, :HEAD_DIM]
        wq_ref[0, d, ci, 0:c, sl] = uw[:, HEAD_DIM:].astype(wq_ref.dtype)


def _gdn_local_call(qn, kn, vv, ab, abt, gate_consts, *, n_heads, n_chunks, inv_precision):
    b, t, w = qn.shape
    c = DN_CHUNK
    n = t // c
    tm = n_chunks * c
    tok = pl.BlockSpec((1, tm, w), lambda bi, i: (bi, i, 0))
    const = lambda a: pl.BlockSpec(a.shape, lambda bi, i: (0,) * a.ndim)
    return pl.pallas_call(
        functools.partial(_gdn_local_kernel, n_heads=n_heads, n_chunks=n_chunks, inv_precision=inv_precision),
        grid=(b, n // n_chunks),
        in_specs=[tok, tok, tok, pl.BlockSpec((1, tm, HEAD_DIM), lambda bi, i: (bi, i, 0)),
                  pl.BlockSpec((1, n_chunks) + abt.shape[2:], lambda bi, i: (bi, i, 0, 0))]
                 + [const(a) for a in gate_consts],
        out_specs=[pl.BlockSpec((1, 2, n_chunks, 2 * c, w), lambda bi, i: (bi, 0, i, 0, 0)),
                   pl.BlockSpec((1, 2, tm, w), lambda bi, i: (bi, 0, i, 0)),
                   pl.BlockSpec((1, 2, n_chunks, n_heads, HEAD_DIM, c), lambda bi, i: (bi, 0, i, 0, 0, 0)),
                   pl.BlockSpec((1, 2, n_chunks, n_heads, c, c), lambda bi, i: (bi, 0, i, 0, 0, 0)),
                   pl.BlockSpec((1, 2, n_chunks, n_heads, HEAD_DIM), lambda bi, i: (bi, 0, i, 0, 0))],
        out_shape=[jax.ShapeDtypeStruct((b, 2, n, 2 * c, w), BF16),
                   jax.ShapeDtypeStruct((b, 2, t, w), F32),
                   jax.ShapeDtypeStruct((b, 2, n, n_heads, HEAD_DIM, c), BF16),
                   jax.ShapeDtypeStruct((b, 2, n, n_heads, c, c), BF16),
                   jax.ShapeDtypeStruct((b, 2, n, n_heads, HEAD_DIM), F32)],
        compiler_params=_params(("arbitrary", "arbitrary")),
        name="gdn_local",
    )(qn, kn, vv, ab, abt, *gate_consts)


def _gdn_scan_kernel(wqf_ref, wqb_ref, uf_ref, ub_ref, kdtf_ref, kdtb_ref, qkf_ref, qkb_ref, egf_ref,
                     egb_ref, s0_ref, of_ref, ob_ref, sfin_ref, s_ref, *, n_heads, bsz):
    n = pl.program_id(0)
    c = DN_CHUNK

    @pl.when(n == 0)
    def _():
        s_ref[...] = s0_ref[...]

    refs = ((wqf_ref, uf_ref, kdtf_ref, qkf_ref, egf_ref, of_ref),
            (wqb_ref, ub_ref, kdtb_ref, qkb_ref, egb_ref, ob_ref))
    streams = [(b, d, h) for b in range(bsz) for d in range(2) for h in range(n_heads)]
    cols = lambda h: slice(h * HEAD_DIM, (h + 1) * HEAD_DIM)
    ws = [_dot(refs[d][0][b, 0, 0, :, cols(h)], s_ref[b, d * n_heads + h].astype(BF16))
          for b, d, h in streams]
    v_new = [(refs[d][1][b, 0, :, cols(h)] - x[:c]).astype(BF16) for (b, d, h), x in zip(streams, ws)]
    intra = [_dot(refs[d][3][b, 0, 0, h], v) for (b, d, h), v in zip(streams, v_new)]
    upd = [_dot(refs[d][2][b, 0, 0, h], v) for (b, d, h), v in zip(streams, v_new)]
    for (b, d, h), x, y, z in zip(streams, ws, intra, upd):
        refs[d][5][b, :, cols(h)] = x[c:] + y
        i = d * n_heads + h
        s_ref[b, i] = s_ref[b, i] * refs[d][4][b, 0, 0, h:h + 1, :] + z

    @pl.when(n == pl.num_programs(0) - 1)
    def _():
        sfin_ref[...] = s_ref[...]


def _gdn_scan_call(wq, u, kdt, qk, eg, s0, *, n_heads):
    b, _, n, _, w = wq.shape
    c = DN_CHUNK
    t = n * c

    def pair(arr, blk):
        nd = len(blk)
        fwd = pl.BlockSpec((b, 1) + blk, lambda i: (0, 0, i) + (0,) * (nd - 1))
        bwd = pl.BlockSpec((b, 1) + blk, lambda i: (0, 1, n - 1 - i) + (0,) * (nd - 1))
        return [fwd, bwd], [arr, arr]

    specs, args = [], []
    for arr, blk in ((wq, (1, 2 * c, w)), (u, (c, w)), (kdt, (1, n_heads, HEAD_DIM, c)),
                     (qk, (1, n_heads, c, c)), (eg, (1, n_heads, HEAD_DIM))):
        sp, ar = pair(arr, blk)
        specs += sp
        args += ar
    st_spec = pl.BlockSpec(s0.shape, lambda i: (0, 0, 0, 0))
    o_shape = jax.ShapeDtypeStruct((b, t, w), F32)
    return pl.pallas_call(
        functools.partial(_gdn_scan_kernel, n_heads=n_heads, bsz=b),
        grid=(n,),
        in_specs=specs + [st_spec],
        out_specs=[pl.BlockSpec((b, c, w), lambda i: (0, i, 0)),
                   pl.BlockSpec((b, c, w), lambda i: (0, n - 1 - i, 0)), st_spec],
        out_shape=[o_shape, o_shape, jax.ShapeDtypeStruct(s0.shape, F32)],
        scratch_shapes=[pltpu.VMEM(s0.shape, F32)],
        compiler_params=_params(("arbitrary",)),
        name="gdn_scan",
    )(*args, s0)


def _gate_consts(a_log, dt_bias):
    n_heads = a_log.shape[1]
    z = jnp.zeros((2, n_heads), F32)
    lay = lambda p: jnp.concatenate([z, p.astype(F32)], axis=1).reshape(-1)
    al, dt = lay(a_log), lay(dt_bias)
    pad = lambda vec: jnp.pad(vec, (0, HEAD_DIM - vec.shape[0])).reshape(1, HEAD_DIM)
    return pad(al), pad(dt), al.reshape(-1, 1), dt.reshape(-1, 1)


def _outproj_kernel(x_ref, gate_ref, ya_ref, za_ref, yb_ref, zb_ref, ocf_ref, ocb_ref, zc_ref, on_ref,
                    w_ref, lng_ref, lnb_ref, o_ref, mix_ref, *, alpha, wa, wb, n_c_heads):
    mix_ref[:, 0:wa] = (ya_ref[...].astype(F32) * _silu(za_ref[...].astype(F32))).astype(BF16)
    mix_ref[:, wa:wa + wb] = (yb_ref[...].astype(F32) * _silu(zb_ref[...].astype(F32))).astype(BF16)
    for h in range(n_c_heads):
        sl = slice(h * HEAD_DIM, (h + 1) * HEAD_DIM)
        oc = ocf_ref[:, sl] + ocb_ref[:, sl]
        yc = oc * lax.rsqrt(jnp.mean(oc * oc, axis=-1, keepdims=True) + EPS) * on_ref[...]
        off = wa + wb + h * HEAD_DIM
        mix_ref[:, off:off + HEAD_DIM] = (yc * _silu(zc_ref[:, sl].astype(F32))).astype(BF16)
    y = _dot(mix_ref[...], w_ref[...])
    r = alpha * x_ref[...] + gate_ref[0] * y
    rc = r - jnp.mean(r, axis=-1, keepdims=True)
    o_ref[...] = rc * lax.rsqrt(jnp.mean(rc * rc, axis=-1, keepdims=True) + EPS) * lng_ref[...] + lnb_ref[...]


def _outproj_call(x2d, mod_l, ya, main, yb, ocf, ocb, o_norm, w_out, ln_g, ln_b, *, col_za, col_zb,
                  col_zc, alpha, tiles_per_batch, ctx_row, tm):
    m, d = x2d.shape
    wa, wb, wc = ya.shape[1], yb.shape[1], ocf.shape[1]
    if ctx_row is None:
        row = lambda i: i // tiles_per_batch
    else:
        row = lambda i: ctx_row
    rows = lambda w, cb=0: pl.BlockSpec((tm, w), lambda i: (i, cb))
    one = lambda w: pl.BlockSpec((1, w), lambda i: (0, 0))
    return pl.pallas_call(
        functools.partial(_outproj_kernel, alpha=alpha, wa=wa, wb=wb, n_c_heads=wc // HEAD_DIM),
        grid=(m // tm,),
        in_specs=[rows(d), pl.BlockSpec((1, 1, d), lambda i: (row(i), 0, 2)),
                  rows(wa), rows(wa, col_za // wa), rows(wb), rows(wb, col_zb // wb),
                  rows(wc), rows(wc), rows(wc, col_zc // wc), one(HEAD_DIM),
                  pl.BlockSpec(w_out.shape, lambda i: (0, 0)), one(d), one(d)],
        out_specs=rows(d),
        out_shape=jax.ShapeDtypeStruct((m, d), F32),
        scratch_shapes=[pltpu.VMEM((tm, wa + wb + wc), BF16)],
        compiler_params=_params(("arbitrary",)),
        name="outproj",
    )(x2d, mod_l, ya, main, yb, main, ocf, ocb, main, o_norm.reshape(1, HEAD_DIM), w_out,
      ln_g.reshape(1, d), ln_b.reshape(1, d))


def kernel(x, c, ctx, c_ctx, w_mod, b_mod, w_in, q_norm, k_norm, rpb, conv_w, a_log, dt_bias, o_norm,
           w_out, ln_g, ln_b):
    bsz, seq, d = x.shape
    ctx_len = ctx.shape[1]
    depth = w_mod.shape[0]
    a_heads = d // (2 * HEAD_DIM)
    a_kv = a_heads // 4
    b_heads = d // (4 * HEAD_DIM)
    c_heads = d // (4 * HEAD_DIM)
    wa, wkv, wb, wc = a_heads * HEAD_DIM, a_kv * HEAD_DIM, b_heads * HEAD_DIM, c_heads * HEAD_DIM
    assert bsz < MOD_ROWS and seq % (NBR_Q_ROWS * GRID_W) == 0 and seq // GRID_W >= NBR_K_ROWS
    assert ctx_len % DN_CHUNK == 0 and 4 * c_heads <= 16
    alpha = (2 * depth) ** 0.25

    ref_sizes = dict(qa=wa, ka=wkv, va=wkv, za=wa, qb=wb, kb=wb, vb=wb, zb=wb, qkvc=3 * wc, zc=wc)
    ref_off, off = {}, 0
    for name, size in ref_sizes.items():
        ref_off[name] = off
        off += size
    ab_off, n_ab = off, 4 * c_heads
    order = ("qa", "za", "ka", "va", "qb", "kb", "vb", "zb", "qkvc", "zc")
    col, off = {}, 0
    for name in order:
        col[name] = off
        off += ref_sizes[name]
    n_main = off

    cc = jnp.concatenate([c, c_ctx[None], jnp.zeros((MOD_ROWS - bsz - 1, d), F32)], axis=0)
    mod = _mod_call(cc, w_mod, b_mod)
    tables = _rope_tables(seq)
    bias = _bias_call(rpb)

    tm_lat = 512
    tm_proj = 1024 if seq % 1024 == 0 else tm_lat
    tm_ctx = min(256, ctx_len)
    main_dtype = BF16
    xl = x.reshape(bsz * seq, d)
    xc = ctx.reshape(bsz * ctx_len, d)
    for l in range(depth):
        with_ctx_out = l < depth - 1
        mod_l = mod[l].reshape(MOD_ROWS, 1, 3 * d)
        w_l = w_in[l]
        w_main = jnp.concatenate(
            [w_l[:, ref_off[nm]:ref_off[nm] + ref_sizes[nm]] for nm in order], axis=1).astype(BF16)
        w_ab = jnp.pad(w_l[:, ab_off:ab_off + n_ab], ((0, 0), (0, HEAD_DIM - n_ab))).astype(BF16)
        w_abt = w_l[:, ab_off:ab_off + n_ab].T.astype(BF16)
        w_out_l = w_out[l].astype(BF16)
        gate_consts = _gate_consts(a_log[l], dt_bias[l])

        main_x, ab_x, abt_x = _inproj_call(xc, mod_l, w_main, w_ab, w_abt, tiles_per_batch=None,
                                           ctx_row=bsz, tm=tm_ctx, tn=512, out_dtype=main_dtype)
        main, ab, abt = _inproj_call(xl, mod_l, w_main, w_ab, w_abt, tiles_per_batch=seq // tm_proj,
                                     ctx_row=None, tm=tm_proj, tn=512, out_dtype=main_dtype)
        main3 = main.reshape(bsz, seq, n_main)
        main_x3 = main_x.reshape(bsz, ctx_len, n_main)

        qa_x, ka_x = _aprep_call(main_x, q_norm[l], k_norm[l], None, col_q=col["qa"], col_k=col["ka"],
                                 nq=a_heads, nk=a_kv, tm=tm_ctx, tiles_per_batch=ctx_len // tm_ctx)
        qa, ka = _aprep_call(main, q_norm[l], k_norm[l], tables, col_q=col["qa"], col_k=col["ka"],
                             nq=a_heads, nk=a_kv, tm=tm_lat, tiles_per_batch=seq // tm_lat)
        ka_x3 = ka_x.reshape(bsz, ctx_len, wkv)
        ya = _flash_call(qa.reshape(bsz, seq, wa),
                         [(ka_x3, 0, main_x3, col["va"]), (ka.reshape(bsz, seq, wkv), 0, main3, col["va"])],
                         n_kv_heads=a_kv, group=a_heads // a_kv, q_col=0, tq=256, tk=512, q_scale=None,
                         out_dtype=F32)

        yb = _nbr_call(main3, main_x3, bias, layer=l, col_q=col["qb"], col_k=col["kb"], col_v=col["vb"],
                       n_heads=b_heads, out_dtype=F32)

        def delta(main2d, ab2d, abt2d, t_len, tm, s0):
            qn, kn, vv = _dnprep_call(main2d, conv_w[l], col=col["qkvc"], n_heads=c_heads, tm=tm,
                                      tiles_per_batch=t_len // tm)
            n = t_len // DN_CHUNK
            abt4 = abt2d.reshape(n_ab, bsz, n, DN_CHUNK).transpose(1, 2, 0, 3)
            r3 = lambda a: a.reshape(bsz, t_len, a.shape[-1])
            local = _gdn_local_call(r3(qn), r3(kn), r3(vv), r3(ab2d), abt4, gate_consts, n_heads=c_heads,
                                    n_chunks=2, inv_precision=INV_PRECISION)
            return _gdn_scan_call(*local, s0, n_heads=c_heads)

        s_zero = jnp.zeros((bsz, 2 * c_heads, HEAD_DIM, HEAD_DIM), F32)
        ocf_x, ocb_x, s_ctx = delta(main_x, ab_x, abt_x, ctx_len, tm_ctx, s_zero)
        ocf, ocb, _ = delta(main, ab, abt, seq, tm_lat, s_ctx)

        x_new = _outproj_call(xl, mod_l, ya.reshape(bsz * seq, wa), main, yb.reshape(bsz * seq, wb),
                              ocf.reshape(bsz * seq, wc), ocb.reshape(bsz * seq, wc), o_norm[l], w_out_l,
                              ln_g[l], ln_b[l], col_za=col["za"], col_zb=col["zb"], col_zc=col["zc"],
                              alpha=alpha, tiles_per_batch=seq // 256, ctx_row=None, tm=256)

        if with_ctx_out:
            ya_x = _flash_call(qa_x.reshape(bsz, ctx_len, wa), [(ka_x3, 0, main_x3, col["va"])],
                               n_kv_heads=a_kv, group=a_heads // a_kv, q_col=0, tq=ctx_len, tk=ctx_len,
                               q_scale=None, out_dtype=F32)
            yb_x = _flash_call(main_x3, [(main_x3, col["kb"], main_x3, col["vb"])],
                               n_kv_heads=b_heads, group=1, q_col=col["qb"], tq=ctx_len, tk=ctx_len,
                               q_scale=HEAD_DIM ** -0.5, out_dtype=F32)
            xc = _outproj_call(xc, mod_l, ya_x.reshape(bsz * ctx_len, wa), main_x,
                               yb_x.reshape(bsz * ctx_len, wb), ocf_x.reshape(bsz * ctx_len, wc),
                               ocb_x.reshape(bsz * ctx_len, wc), o_norm[l], w_out_l, ln_g[l], ln_b[l],
                               col_za=col["za"], col_zb=col["zb"], col_zc=col["zc"], alpha=alpha,
                               tiles_per_batch=None, ctx_row=bsz, tm=tm_ctx)
        xl = x_new
    return xl.reshape(bsz, seq, d)
```

```python
import functools

import jax
import jax.numpy as jnp
from jax import lax
from jax.experimental import pallas as pl
from jax.experimental.pallas import tpu as pltpu

F32 = jnp.float32
BF16 = jnp.bfloat16

HEAD_DIM = 128
GRID_W = 64
ROPE_THETA = 10000.0
NA_WIN_H = 8
NA_WIN_W = 16
CONV_K = 3
DN_CHUNK = 64
EPS = 1e-6
NEG = -1e30
MOD_ROWS = 8
NBR_Q_ROWS = 8
NBR_K_ROWS = 16
VMEM_LIMIT = 56 * 1024 * 1024
LOG2E = 1.4426950408889634
SCORE_SCALE = HEAD_DIM ** -0.5 * LOG2E

HIGHEST = lax.Precision.HIGHEST
INV_PRECISION = None


def _silu(z):
    return z * jax.nn.sigmoid(z)


def _softplus(z):
    return jnp.maximum(z, 0.0) + jnp.log(1.0 + jnp.exp(-jnp.abs(z)))


def _dot(a, b, precision=None):
    return jnp.dot(a, b, preferred_element_type=F32, precision=precision)


def _dot_nt(a, b, precision=None):
    return lax.dot_general(a, b, (((1,), (1,)), ((), ())), preferred_element_type=F32,
                           precision=precision)


def _params(sem):
    return pltpu.CompilerParams(dimension_semantics=sem, vmem_limit_bytes=VMEM_LIMIT)


def _mod_kernel(c_ref, w_ref, b_ref, o_ref):
    s = _silu(c_ref[...]).astype(BF16)
    o_ref[0] = _dot(s, w_ref[0].astype(BF16)) + b_ref[0]


def _mod_call(cc, w_mod, b_mod):
    depth, d, n = w_mod.shape
    tn = 768
    return pl.pallas_call(
        _mod_kernel,
        grid=(depth, n // tn),
        in_specs=[pl.BlockSpec((MOD_ROWS, d), lambda l, j: (0, 0)),
                  pl.BlockSpec((1, d, tn), lambda l, j: (l, 0, j)),
                  pl.BlockSpec((1, 1, tn), lambda l, j: (l, 0, j))],
        out_specs=pl.BlockSpec((1, MOD_ROWS, tn), lambda l, j: (l, 0, j)),
        out_shape=jax.ShapeDtypeStruct((depth, MOD_ROWS, n), F32),
        compiler_params=_params(("arbitrary", "arbitrary")),
        name="mod",
    )(cc, w_mod, b_mod.reshape(depth, 1, n))


LN_ROWS = 256


def _inproj_kernel(x_ref, shift_ref, scale_ref, w_ref, wab_ref, o_ref, oab_ref, h_ref):
    @pl.when(pl.program_id(1) == 0)
    def _():
        for r0 in range(0, x_ref.shape[0], LN_ROWS):
            r = slice(r0, r0 + LN_ROWS)
            x = x_ref[r, :]
            xc = x - jnp.mean(x, axis=-1, keepdims=True)
            y = xc * lax.rsqrt(jnp.mean(xc * xc, axis=-1, keepdims=True) + EPS)
            h_ref[r, :] = (y * (1.0 + scale_ref[0]) + shift_ref[0]).astype(BF16)
            oab_ref[r, :] = _dot(h_ref[r, :], wab_ref[...])

    o_ref[...] = _dot(h_ref[...], w_ref[...]).astype(o_ref.dtype)


def _inproj_call(x2d, mod_l, w_all, w_ab, *, n_main, tiles_per_batch, ctx_row, tm, tn, out_dtype):
    m, d = x2d.shape
    n = n_main
    assert n % tn == 0 and tm % LN_ROWS == 0
    if ctx_row is None:
        row = lambda i: i // tiles_per_batch
    else:
        row = lambda i: ctx_row
    return pl.pallas_call(
        _inproj_kernel,
        grid=(m // tm, n // tn),
        in_specs=[pl.BlockSpec((tm, d), lambda i, j: (i, 0)),
                  pl.BlockSpec((1, 1, d), lambda i, j: (row(i), 0, 0)),
                  pl.BlockSpec((1, 1, d), lambda i, j: (row(i), 0, 1)),
                  pl.BlockSpec((d, tn), lambda i, j: (0, j)),
                  pl.BlockSpec((d, HEAD_DIM), lambda i, j: (0, 0))],
        out_specs=[pl.BlockSpec((tm, tn), lambda i, j: (i, j)),
                   pl.BlockSpec((tm, HEAD_DIM), lambda i, j: (i, 0))],
        out_shape=[jax.ShapeDtypeStruct((m, n), out_dtype),
                   jax.ShapeDtypeStruct((m, HEAD_DIM), F32)],
        scratch_shapes=[pltpu.VMEM((tm, d), BF16)],
        compiler_params=_params(("arbitrary", "arbitrary")),
        name="inproj",
    )(x2d, mod_l, mod_l, w_all, w_ab)


def _swap_quarters(x):
    lane = lax.broadcasted_iota(jnp.int32, x.shape, 1)
    fwd = pltpu.roll(x, 3 * HEAD_DIM // 4, axis=1)
    bwd = pltpu.roll(x, HEAD_DIM // 4, axis=1)
    return jnp.where((lane % (HEAD_DIM // 2)) < HEAD_DIM // 4, fwd, bwd)


def _aprep_kernel(*refs, nq, nk, rope):
    if rope:
        q_ref, k_ref, qw_ref, kw_ref, cos_ref, sin_ref, qo_ref, ko_ref = refs
        cos, sin = cos_ref[...], sin_ref[...]
    else:
        q_ref, k_ref, qw_ref, kw_ref, qo_ref, ko_ref = refs

    def norm(x, w):
        y = x * lax.rsqrt(jnp.mean(x * x, axis=-1, keepdims=True) + EPS) * w
        if rope:
            y = y * cos + _swap_quarters(y) * sin
        return y

    for h in range(nq):
        sl = slice(h * HEAD_DIM, (h + 1) * HEAD_DIM)
        y = norm(q_ref[:, sl].astype(F32), qw_ref[...])
        qo_ref[:, sl] = (y * SCORE_SCALE).astype(qo_ref.dtype)
    for h in range(nk):
        sl = slice(h * HEAD_DIM, (h + 1) * HEAD_DIM)
        ko_ref[:, sl] = norm(k_ref[:, sl].astype(F32), kw_ref[...]).astype(ko_ref.dtype)


def _aprep_call(main, q_norm, k_norm, tables, *, col_q, col_k, nq, nk, tm, tiles_per_batch):
    m = main.shape[0]
    wq, wk = nq * HEAD_DIM, nk * HEAD_DIM
    in_specs = [pl.BlockSpec((tm, wq), lambda i: (i, col_q // wq)),
                pl.BlockSpec((tm, wk), lambda i: (i, col_k // wk)),
                pl.BlockSpec((1, HEAD_DIM), lambda i: (0, 0)),
                pl.BlockSpec((1, HEAD_DIM), lambda i: (0, 0))]
    args = [main, main, q_norm.reshape(1, HEAD_DIM), k_norm.reshape(1, HEAD_DIM)]
    if tables is not None:
        in_specs += [pl.BlockSpec((tm, HEAD_DIM), lambda i: (i % tiles_per_batch, 0))] * 2
        args += list(tables)
    return pl.pallas_call(
        functools.partial(_aprep_kernel, nq=nq, nk=nk, rope=tables is not None),
        grid=(m // tm,),
        in_specs=in_specs,
        out_specs=[pl.BlockSpec((tm, wq), lambda i: (i, 0)),
                   pl.BlockSpec((tm, wk), lambda i: (i, 0))],
        out_shape=[jax.ShapeDtypeStruct((m, wq), BF16), jax.ShapeDtypeStruct((m, wk), BF16)],
        compiler_params=_params(("arbitrary",)),
        name="aprep",
    )(*args)


def _rope_tables(n_tokens):
    t = jnp.arange(n_tokens, dtype=jnp.int32)
    row = (t // GRID_W).astype(F32)
    col = (t % GRID_W).astype(F32)
    half = HEAD_DIM // 2
    inv_freq = ROPE_THETA ** (-jnp.arange(0, half, 2, dtype=F32) / half)
    ar, ac = row[:, None] * inv_freq, col[:, None] * inv_freq
    cos = jnp.concatenate([jnp.cos(ar), jnp.cos(ar), jnp.cos(ac), jnp.cos(ac)], -1)
    sin = jnp.concatenate([-jnp.sin(ar), jnp.sin(ar), -jnp.sin(ac), jnp.sin(ac)], -1)
    return cos, sin


def _flash_kernel(*refs, group, tq, sources, q_scale):
    q_ref = refs[0]
    o_ref = refs[1 + 2 * len(sources)]
    q = jnp.concatenate([q_ref[0, :, g * HEAD_DIM:(g + 1) * HEAD_DIM] for g in range(group)], axis=0)
    if q_scale is not None:
        q = q.astype(F32) * q_scale
    q = q.astype(BF16)
    rows = group * tq

    def step(carry, k, v):
        m, l, acc = carry
        s = _dot_nt(q, k.astype(BF16))
        m_new = jnp.maximum(m, jnp.max(s, axis=-1, keepdims=True))
        alpha = jnp.exp2(m - m_new)
        p = jnp.exp2(s - m_new)
        l = alpha * l + jnp.sum(p, axis=-1, keepdims=True)
        acc = alpha * acc + _dot(p.astype(BF16), v.astype(BF16))
        return m_new, l, acc

    carry = (jnp.full((rows, 1), -jnp.inf, F32), jnp.zeros((rows, 1), F32),
             jnp.zeros((rows, HEAD_DIM), F32))
    for si, (length, tk) in enumerate(sources):
        k_ref, v_ref = refs[1 + 2 * si], refs[2 + 2 * si]
        if length == tk:
            carry = step(carry, k_ref[0], v_ref[0])
        else:
            def body(i, c, k_ref=k_ref, v_ref=v_ref, tk=tk):
                st = pl.multiple_of(i * tk, tk)
                return step(c, k_ref[0, pl.ds(st, tk), :], v_ref[0, pl.ds(st, tk), :])
            carry = lax.fori_loop(0, length // tk, body, carry)
    _, l, acc = carry
    o = acc / l
    for g in range(group):
        o_ref[0, :, g * HEAD_DIM:(g + 1) * HEAD_DIM] = o[g * tq:(g + 1) * tq].astype(o_ref.dtype)


SOFTMAX_UNIT = 16


def _gqa_kernel(q_ref, kc_ref, vc_ref, kl_ref, vl_ref, o_ref, q4_ref, s0_ref, s1_ref, sc_ref, p_ref,
                m_ref, l_ref, a_ref, acc_ref, *, group, tq, tk):
    rows = group * tq
    half = rows // 2
    n_lat = kl_ref.shape[1] // tk
    for g in range(group):
        q4_ref[g * tq:(g + 1) * tq, :] = q_ref[0, :, g * HEAD_DIM:(g + 1) * HEAD_DIM]
    m_ref[...] = jnp.full(m_ref.shape, -jnp.inf, F32)
    l_ref[...] = jnp.zeros(l_ref.shape, F32)
    acc_ref[...] = jnp.zeros(acc_ref.shape, F32)
    halves = (slice(0, half), slice(half, rows))

    def issue(dst_ref, k, rs):
        dst_ref[rs, :] = _dot_nt(q4_ref[rs, :], k)

    def softmax(src_ref, rs, width):
        for r0 in range(rs.start, rs.stop, SOFTMAX_UNIT):
            r = slice(r0, r0 + SOFTMAX_UNIT)
            cols = [slice(j, j + HEAD_DIM) for j in range(0, width, HEAD_DIM)]
            s = [src_ref[r, c] for c in cols]
            mx = functools.reduce(jnp.maximum, s)
            m_old = m_ref[r, :]
            m_new = jnp.maximum(m_old, jnp.max(mx, axis=-1, keepdims=True))
            alpha = jnp.exp2(m_old - m_new)
            p = [jnp.exp2(x - m_new) for x in s]
            for c, x in zip(cols, p):
                p_ref[r, c] = x.astype(BF16)
            m_ref[r, :] = m_new
            a_ref[r, :] = alpha
            l_ref[r, :] = alpha * l_ref[r, :] + functools.reduce(jnp.add, p)

    def accumulate(rs, width, v):
        pv = _dot(p_ref[rs, 0:width], v)
        acc_ref[rs, :] = acc_ref[rs, :] * a_ref[rs, :] + pv

    def chunk_step(cur_ref, width, v, nxt_ref, k_next):
        for rs in halves:
            if nxt_ref is not None:
                issue(nxt_ref, k_next, rs)
            softmax(cur_ref, rs, width)
            accumulate(rs, width, v)

    lat = lambda ref, i: ref[0, pl.ds(pl.multiple_of(i * tk, tk), tk), :]
    for rs in halves:
        issue(s0_ref, lat(kl_ref, 0), rs)

    def pair(j, carry):
        i = 2 * j
        chunk_step(s0_ref, tk, lat(vl_ref, i), s1_ref, lat(kl_ref, i + 1))
        chunk_step(s1_ref, tk, lat(vl_ref, i + 1), s0_ref, lat(kl_ref, i + 2))
        return carry

    lax.fori_loop(0, n_lat // 2 - 1, pair, 0)
    chunk_step(s0_ref, tk, lat(vl_ref, n_lat - 2), s1_ref, lat(kl_ref, n_lat - 1))
    chunk_step(s1_ref, tk, lat(vl_ref, n_lat - 1), sc_ref, kc_ref[0])
    chunk_step(sc_ref, kc_ref.shape[1], vc_ref[0], None, None)

    o = acc_ref[...] / jnp.sum(l_ref[...], axis=-1, keepdims=True)
    for g in range(group):
        o_ref[0, :, g * HEAD_DIM:(g + 1) * HEAD_DIM] = o[g * tq:(g + 1) * tq].astype(o_ref.dtype)


def _gqa_call(q, k_ctx, v_ctx, v_ctx_col, k_lat, v_lat, v_lat_col, *, n_kv_heads, group, tq, tk, out_dtype):
    b, s, _ = q.shape
    ctx_len = k_ctx.shape[1]
    wq = group * HEAD_DIM
    rows = group * tq
    assert s % (2 * tk) == 0 and s // tk >= 2 and ctx_len % HEAD_DIM == 0 and ctx_len <= tk
    assert rows % (2 * SOFTMAX_UNIT) == 0
    cvc, cvl = v_ctx_col // HEAD_DIM, v_lat_col // HEAD_DIM
    kv_spec = lambda t, c: pl.BlockSpec((1, t, HEAD_DIM), lambda bi, g, qi: (bi, 0, c + g))
    stat = pltpu.VMEM((rows, HEAD_DIM), F32)
    return pl.pallas_call(
        functools.partial(_gqa_kernel, group=group, tq=tq, tk=tk),
        grid=(b, n_kv_heads, s // tq),
        in_specs=[pl.BlockSpec((1, tq, wq), lambda bi, g, qi: (bi, qi, g)),
                  kv_spec(ctx_len, 0), kv_spec(ctx_len, cvc), kv_spec(s, 0), kv_spec(s, cvl)],
        out_specs=pl.BlockSpec((1, tq, wq), lambda bi, g, qi: (bi, qi, g)),
        out_shape=jax.ShapeDtypeStruct((b, s, n_kv_heads * wq), out_dtype),
        scratch_shapes=[pltpu.VMEM((rows, HEAD_DIM), BF16), pltpu.VMEM((rows, tk), F32),
                        pltpu.VMEM((rows, tk), F32), pltpu.VMEM((rows, ctx_len), F32),
                        pltpu.VMEM((rows, tk), BF16), stat, stat, stat, stat],
        compiler_params=_params(("arbitrary", "arbitrary", "arbitrary")),
        name="gqa",
    )(q, k_ctx, v_ctx, k_lat, v_lat)


def _flash_call(q, kv, *, n_kv_heads, group, q_col, tq, tk, q_scale, out_dtype):
    b, t_q, _ = q.shape
    wq = group * HEAD_DIM
    in_specs = [pl.BlockSpec((1, tq, wq), lambda bi, g, qi: (bi, qi, q_col // wq + g))]
    args = [q]
    sources = []
    for k_arr, k_col, v_arr, v_col in kv:
        t_k = k_arr.shape[1]
        in_specs.append(pl.BlockSpec((1, t_k, HEAD_DIM),
                                     lambda bi, g, qi, c=k_col // HEAD_DIM: (bi, 0, c + g)))
        in_specs.append(pl.BlockSpec((1, t_k, HEAD_DIM),
                                     lambda bi, g, qi, c=v_col // HEAD_DIM: (bi, 0, c + g)))
        args += [k_arr, v_arr]
        sources.append((t_k, min(tk, t_k)))
    return pl.pallas_call(
        functools.partial(_flash_kernel, group=group, tq=tq, sources=tuple(sources), q_scale=q_scale),
        grid=(b, n_kv_heads, t_q // tq),
        in_specs=in_specs,
        out_specs=pl.BlockSpec((1, tq, wq), lambda bi, g, qi: (bi, qi, g)),
        out_shape=jax.ShapeDtypeStruct((b, t_q, n_kv_heads * wq), out_dtype),
        compiler_params=_params(("arbitrary", "arbitrary", "arbitrary")),
        name="flash",
    )(*args)


def _nbr_window(cls, qr, kr):
    if cls == 0:
        lo, dr = max(qr - NA_WIN_H // 2, 0), kr - qr
    elif cls == 1:
        lo, dr = qr, kr - NA_WIN_H // 2 - qr
    else:
        lo, dr = min(NA_WIN_H // 2 + qr, NBR_K_ROWS - NA_WIN_H), kr - NBR_Q_ROWS - qr
    if lo <= kr < lo + NA_WIN_H:
        assert -NA_WIN_H < dr < NA_WIN_H
        return dr + NA_WIN_H - 1
    return None


def _bias_kernel(rpb_ref, o_ref, *, n_heads):
    nrow, ncol = 2 * NA_WIN_H - 1, 2 * NA_WIN_W - 1
    base = (pl.program_id(0) * n_heads + pl.program_id(1)) * (nrow * ncol)
    shape = (GRID_W, 2 * GRID_W)
    c = lax.broadcasted_iota(jnp.int32, shape, 0)
    lane = lax.broadcasted_iota(jnp.int32, shape, 1)
    right = lane >= GRID_W
    kc = jnp.where(right, lane - GRID_W, lane)
    c0 = jnp.clip(c - NA_WIN_W // 2, 0, GRID_W - NA_WIN_W)
    col_ok = (kc >= c0) & (kc < c0 + NA_WIN_W)
    idx = kc - c + (NA_WIN_W - 1)
    cache = {}

    def tile(a_left, a_right):
        key = (a_left, a_right)
        if key not in cache:
            t = jnp.full(shape, NEG, F32)
            if key != (None, None):
                for j in range(ncol):
                    sl = NEG if a_left is None else rpb_ref[base + a_left * ncol + j]
                    sr = NEG if a_right is None else rpb_ref[base + a_right * ncol + j]
                    t = jnp.where(idx == j, jnp.where(right, sr, sl), t)
                t = jnp.where(col_ok, t, NEG)
            cache[key] = t
        return cache[key]

    for cls in range(3):
        for qr in range(NBR_Q_ROWS):
            for p in range(NBR_K_ROWS // 2):
                t = tile(_nbr_window(cls, qr, 2 * p), _nbr_window(cls, qr, 2 * p + 1))
                o_ref[0, 0, cls, qr * GRID_W:(qr + 1) * GRID_W, p * 2 * GRID_W:(p + 1) * 2 * GRID_W] = t


def _bias_call(rpb):
    depth, n_heads = rpb.shape[:2]
    shape = (depth, n_heads, 3, NBR_Q_ROWS * GRID_W, NBR_K_ROWS * GRID_W)
    return pl.pallas_call(
        functools.partial(_bias_kernel, n_heads=n_heads),
        grid=(depth, n_heads),
        in_specs=[pl.BlockSpec(memory_space=pltpu.SMEM)],
        out_specs=pl.BlockSpec((1, 1) + shape[2:], lambda l, h: (l, h, 0, 0, 0)),
        out_shape=jax.ShapeDtypeStruct(shape, F32),
        compiler_params=_params(("arbitrary", "arbitrary")),
        name="nbr_bias",
    )(rpb.reshape(-1))


def _nbr_kernel(q_ref, k_ref, v_ref, kc_ref, vc_ref, bias_ref, o_ref, *, rows):
    g = pl.program_id(2)
    n_steps = rows // NBR_Q_ROWS
    start_row = jnp.clip(g * NBR_Q_ROWS - NA_WIN_H // 2, 0, rows - NBR_K_ROWS)
    start = pl.multiple_of(start_row * GRID_W, (NA_WIN_H // 2) * GRID_W)
    cls = jnp.where(g == 0, 0, jnp.where(g == n_steps - 1, 2, 1))
    nk = NBR_K_ROWS * GRID_W
    q = (q_ref[0].astype(F32) * HEAD_DIM ** -0.5).astype(BF16)
    k = k_ref[0, pl.ds(start, nk), :].astype(BF16)
    v = v_ref[0, pl.ds(start, nk), :].astype(BF16)
    s_win = _dot_nt(q, k) + bias_ref[0, 0, cls]
    s_ctx = _dot_nt(q, kc_ref[0].astype(BF16))
    m = jnp.maximum(jnp.max(s_win, axis=-1, keepdims=True), jnp.max(s_ctx, axis=-1, keepdims=True))
    p_win = jnp.exp(s_win - m)
    p_ctx = jnp.exp(s_ctx - m)
    l = jnp.sum(p_win, axis=-1, keepdims=True) + jnp.sum(p_ctx, axis=-1, keepdims=True)
    o = _dot(p_win.astype(BF16), v) + _dot(p_ctx.astype(BF16), vc_ref[0].astype(BF16))
    o_ref[0] = (o / l).astype(o_ref.dtype)


def _nbr_call(main, main_ctx, bias, *, layer, col_q, col_k, col_v, n_heads, out_dtype):
    b, s, _ = main.shape
    ctx_len = main_ctx.shape[1]
    rows = s // GRID_W
    tq = NBR_Q_ROWS * GRID_W
    cq, ck, cv = col_q // HEAD_DIM, col_k // HEAD_DIM, col_v // HEAD_DIM
    return pl.pallas_call(
        functools.partial(_nbr_kernel, rows=rows),
        grid=(n_heads, b, rows // NBR_Q_ROWS),
        in_specs=[pl.BlockSpec((1, tq, HEAD_DIM), lambda h, bi, g: (bi, g, cq + h)),
                  pl.BlockSpec((1, s, HEAD_DIM), lambda h, bi, g: (bi, 0, ck + h)),
                  pl.BlockSpec((1, s, HEAD_DIM), lambda h, bi, g: (bi, 0, cv + h)),
                  pl.BlockSpec((1, ctx_len, HEAD_DIM), lambda h, bi, g: (bi, 0, ck + h)),
                  pl.BlockSpec((1, ctx_len, HEAD_DIM), lambda h, bi, g: (bi, 0, cv + h)),
                  pl.BlockSpec((1, 1) + bias.shape[2:], lambda h, bi, g: (layer, h, 0, 0, 0))],
        out_specs=pl.BlockSpec((1, tq, HEAD_DIM), lambda h, bi, g: (bi, g, h)),
        out_shape=jax.ShapeDtypeStruct((b, s, n_heads * HEAD_DIM), out_dtype),
        compiler_params=_params(("arbitrary", "arbitrary", "arbitrary")),
        name="nbr_attn",
    )(main, main, main, main_ctx, main_ctx, bias)


def _dnprep_kernel(x_ref, prev_ref, next_ref, w_ref, q_ref, k_ref, v_ref, *, tiles_per_batch, n_heads):
    i = pl.program_id(0) % tiles_per_batch
    x = x_ref[...].astype(F32)
    tm = x.shape[0]
    row = lax.broadcasted_iota(jnp.int32, x.shape, 0)
    halo = prev_ref.shape[0]
    prev_row = jnp.where(i == 0, 0.0, prev_ref[halo - 1:halo, :].astype(F32))
    next_row = jnp.where(i == tiles_per_batch - 1, 0.0, next_ref[0:1, :].astype(F32))
    x_prev = jnp.where(row == 0, prev_row, pltpu.roll(x, 1, axis=0))
    x_next = jnp.where(row == tm - 1, next_row, pltpu.roll(x, tm - 1, axis=0))
    y = _silu(w_ref[0:1, :] * x_prev + w_ref[1:2, :] * x + w_ref[2:3, :] * x_next)
    w = n_heads * HEAD_DIM
    for h in range(n_heads):
        sl = slice(h * HEAD_DIM, (h + 1) * HEAD_DIM)
        qh = y[:, h * HEAD_DIM:(h + 1) * HEAD_DIM]
        kh = y[:, w + h * HEAD_DIM:w + (h + 1) * HEAD_DIM]
        q_ref[:, sl] = qh * lax.rsqrt(jnp.sum(qh * qh, axis=-1, keepdims=True) + EPS) * HEAD_DIM ** -0.5
        k_ref[:, sl] = kh * lax.rsqrt(jnp.sum(kh * kh, axis=-1, keepdims=True) + EPS)
    v_ref[...] = y[:, 2 * w:]


def _dnprep_call(main, conv_w, *, col, n_heads, tm, tiles_per_batch):
    m = main.shape[0]
    w3 = 3 * n_heads * HEAD_DIM
    halo = 16
    hb = tm // halo
    n_halo = m // halo
    out = jax.ShapeDtypeStruct((m, n_heads * HEAD_DIM), F32)
    o_spec = pl.BlockSpec((tm, n_heads * HEAD_DIM), lambda i: (i, 0))
    return pl.pallas_call(
        functools.partial(_dnprep_kernel, tiles_per_batch=tiles_per_batch, n_heads=n_heads),
        grid=(m // tm,),
        in_specs=[pl.BlockSpec((tm, w3), lambda i: (i, col // w3)),
                  pl.BlockSpec((halo, w3), lambda i: (jnp.maximum(i * hb - 1, 0), col // w3)),
                  pl.BlockSpec((halo, w3), lambda i: (jnp.minimum((i + 1) * hb, n_halo - 1), col // w3)),
                  pl.BlockSpec((CONV_K, w3), lambda i: (0, 0))],
        out_specs=[o_spec, o_spec, o_spec],
        out_shape=[out, out, out],
        compiler_params=_params(("arbitrary",)),
        name="dn_prep",
    )(main, main, main, conv_w)


def _tri_masks(n, upper):
    i = lax.broadcasted_iota(jnp.int32, (n, n), 0)
    j = lax.broadcasted_iota(jnp.int32, (n, n), 1)
    incl = (j >= i) if upper else (j <= i)
    strict = (j > i) if upper else (j < i)
    eye = (i == j).astype(F32)
    base = 8
    same_base = (i // base) == (j // base)
    levels = []
    s = base
    while s < n:
        levels.append(((i // (2 * s)) == (j // (2 * s))) & ((i // s) != (j // s)))
        s *= 2
    return incl, strict, eye, same_base, levels


def _unit_tri_inverse(lms, masks, precision):
    mm = lambda a, b: _dot(a, b, precision) if precision is not None else _dot(a.astype(BF16), b.astype(BF16))
    eyes = [m[2] for m in masks]
    l1 = [jnp.where(m[3], lm, 0.0) for lm, m in zip(lms, masks)]
    l2 = [mm(a, a) for a in l1]
    l4 = [mm(a, a) for a in l2]
    t = [mm(e - a, e + b) for e, a, b in zip(eyes, l1, l2)]
    t = [mm(a, e + b) for e, a, b in zip(eyes, t, l4)]
    for lvl in range(len(masks[0][4])):
        cs = [jnp.where(m[4][lvl], lm, 0.0) for lm, m in zip(lms, masks)]
        tc = [mm(a, b) for a, b in zip(t, cs)]
        t = [a - mm(b, a) for a, b in zip(t, tc)]
    return t


def _gdn_local_kernel(q_ref, k_ref, v_ref, ab_ref, alr_ref, dtr_ref,
                      wq_ref, u_ref, kdt_ref, qk_ref, eg_ref, *, n_heads, n_chunks, inv_precision):
    c = DN_CHUNK
    masks = [_tri_masks(c, upper) for upper in (False, True)]
    gates = {}
    for d in range(2):
        cum = masks[d][0].astype(F32)
        for ci in range(n_chunks):
            ab = ab_ref[0, ci * c:(ci + 1) * c, :]
            g_tok = -jnp.exp(alr_ref[...]) * _softplus(ab + dtr_ref[...])
            gc_tok = _dot(cum, g_tok, HIGHEST)
            gates[ci, d] = (jax.nn.sigmoid(ab), gc_tok, gc_tok.T)

    streams = [(ci, d, h) for ci in range(n_chunks) for d in range(2) for h in range(n_heads)]
    pre = []
    for ci, d, h in streams:
        incl, strict = masks[d][0], masks[d][1]
        rows = slice(ci * c, (ci + 1) * c)
        sl = slice(h * HEAD_DIM, (h + 1) * HEAD_DIM)
        cb, cg = d * 2 * n_heads + h, d * 2 * n_heads + n_heads + h
        beta_tok, gc_tok, gc_rows = gates[ci, d]
        q, k, v = q_ref[0, rows, sl], k_ref[0, rows, sl], v_ref[0, rows, sl]
        beta_c = beta_tok[:, cb:cb + 1]
        gc_c = gc_tok[:, cg:cg + 1]
        last = 0 if d == 1 else c - 1
        g_last = gc_c[last:last + 1, :]
        decay = jnp.where(incl, jnp.exp(jnp.where(incl, gc_c - gc_rows[cg:cg + 1, :], 0.0)), 0.0)
        kbeta = k * beta_c
        e_c = jnp.exp(gc_c)
        wq_ref[0, d, ci, c:2 * c, sl] = (q * e_c).astype(wq_ref.dtype)
        kdt_ref[0, d, ci, h] = (k * jnp.exp(g_last - gc_c)).T.astype(kdt_ref.dtype)
        eg_ref[0, d, ci, h:h + 1, :] = jnp.broadcast_to(jnp.exp(g_last), (1, HEAD_DIM))
        lhs = jnp.concatenate([kbeta, q], axis=0).astype(BF16)
        rhs = jnp.concatenate([v * beta_c, kbeta * e_c], axis=1).astype(BF16)
        pre.append((lhs, k.astype(BF16), rhs, decay))
    kq = [_dot_nt(lhs, kb) for lhs, kb, _, _ in pre]
    lms = []
    for (ci, d, h), x, (_, _, _, decay) in zip(streams, kq, pre):
        qk_ref[0, d, ci, h] = (x[c:] * decay).astype(qk_ref.dtype)
        lms.append(jnp.where(masks[d][1], x[:c] * decay, 0.0))
    ts = _unit_tri_inverse(lms, [masks[d] for _, d, _ in streams], inv_precision)
    uws = [_dot(t.astype(BF16), rhs) for t, (_, _, rhs, _) in zip(ts, pre)]
    for (ci, d, h), uw in zip(streams, uws):
        sl = slice(h * HEAD_DIM, (h + 1) * HEAD_DIM)
        u_ref[0, d, ci * c:(ci + 1) * c, sl] = uw[:, :HEAD_DIM]
        wq_ref[0, d, ci, 0:c, sl] = uw[:, HEAD_DIM:].astype(wq_ref.dtype)


def _gdn_local_call(qn, kn, vv, ab, gate_consts, *, n_heads, n_chunks, inv_precision):
    b, t, w = qn.shape
    c = DN_CHUNK
    n = t // c
    tm = n_chunks * c
    tok = pl.BlockSpec((1, tm, w), lambda bi, i: (bi, i, 0))
    const = lambda a: pl.BlockSpec(a.shape, lambda bi, i: (0,) * a.ndim)
    return pl.pallas_call(
        functools.partial(_gdn_local_kernel, n_heads=n_heads, n_chunks=n_chunks, inv_precision=inv_precision),
        grid=(b, n // n_chunks),
        in_specs=[tok, tok, tok, pl.BlockSpec((1, tm, HEAD_DIM), lambda bi, i: (bi, i, 0))]
                 + [const(a) for a in gate_consts],
        out_specs=[pl.BlockSpec((1, 2, n_chunks, 2 * c, w), lambda bi, i: (bi, 0, i, 0, 0)),
                   pl.BlockSpec((1, 2, tm, w), lambda bi, i: (bi, 0, i, 0)),
                   pl.BlockSpec((1, 2, n_chunks, n_heads, HEAD_DIM, c), lambda bi, i: (bi, 0, i, 0, 0, 0)),
                   pl.BlockSpec((1, 2, n_chunks, n_heads, c, c), lambda bi, i: (bi, 0, i, 0, 0, 0)),
                   pl.BlockSpec((1, 2, n_chunks, n_heads, HEAD_DIM), lambda bi, i: (bi, 0, i, 0, 0))],
        out_shape=[jax.ShapeDtypeStruct((b, 2, n, 2 * c, w), BF16),
                   jax.ShapeDtypeStruct((b, 2, t, w), F32),
                   jax.ShapeDtypeStruct((b, 2, n, n_heads, HEAD_DIM, c), BF16),
                   jax.ShapeDtypeStruct((b, 2, n, n_heads, c, c), BF16),
                   jax.ShapeDtypeStruct((b, 2, n, n_heads, HEAD_DIM), F32)],
        compiler_params=_params(("arbitrary", "arbitrary")),
        name="gdn_local",
    )(qn, kn, vv, ab, *gate_consts)


def _gdn_scan_kernel(wqf_ref, wqb_ref, uf_ref, ub_ref, kdtf_ref, kdtb_ref, qkf_ref, qkb_ref, egf_ref,
                     egb_ref, s0_ref, of_ref, ob_ref, sfin_ref, s_ref, *, n_heads, bsz):
    n = pl.program_id(0)
    c = DN_CHUNK

    @pl.when(n == 0)
    def _():
        s_ref[...] = s0_ref[...]

    refs = ((wqf_ref, uf_ref, kdtf_ref, qkf_ref, egf_ref, of_ref),
            (wqb_ref, ub_ref, kdtb_ref, qkb_ref, egb_ref, ob_ref))
    streams = [(b, d, h) for b in range(bsz) for d in range(2) for h in range(n_heads)]
    cols = lambda h: slice(h * HEAD_DIM, (h + 1) * HEAD_DIM)
    ws = [_dot(refs[d][0][b, 0, 0, :, cols(h)], s_ref[b, d * n_heads + h].astype(BF16))
          for b, d, h in streams]
    v_new = [(refs[d][1][b, 0, :, cols(h)] - x[:c]).astype(BF16) for (b, d, h), x in zip(streams, ws)]
    intra = [_dot(refs[d][3][b, 0, 0, h], v) for (b, d, h), v in zip(streams, v_new)]
    upd = [_dot(refs[d][2][b, 0, 0, h], v) for (b, d, h), v in zip(streams, v_new)]
    for (b, d, h), x, y, z in zip(streams, ws, intra, upd):
        refs[d][5][b, :, cols(h)] = x[c:] + y
        i = d * n_heads + h
        s_ref[b, i] = s_ref[b, i] * refs[d][4][b, 0, 0, h:h + 1, :] + z

    @pl.when(n == pl.num_programs(0) - 1)
    def _():
        sfin_ref[...] = s_ref[...]


def _gdn_scan_call(wq, u, kdt, qk, eg, s0, *, n_heads):
    b, _, n, _, w = wq.shape
    c = DN_CHUNK
    t = n * c

    def pair(arr, blk):
        nd = len(blk)
        fwd = pl.BlockSpec((b, 1) + blk, lambda i: (0, 0, i) + (0,) * (nd - 1))
        bwd = pl.BlockSpec((b, 1) + blk, lambda i: (0, 1, n - 1 - i) + (0,) * (nd - 1))
        return [fwd, bwd], [arr, arr]

    specs, args = [], []
    for arr, blk in ((wq, (1, 2 * c, w)), (u, (c, w)), (kdt, (1, n_heads, HEAD_DIM, c)),
                     (qk, (1, n_heads, c, c)), (eg, (1, n_heads, HEAD_DIM))):
        sp, ar = pair(arr, blk)
        specs += sp
        args += ar
    st_spec = pl.BlockSpec(s0.shape, lambda i: (0, 0, 0, 0))
    o_shape = jax.ShapeDtypeStruct((b, t, w), F32)
    return pl.pallas_call(
        functools.partial(_gdn_scan_kernel, n_heads=n_heads, bsz=b),
        grid=(n,),
        in_specs=specs + [st_spec],
        out_specs=[pl.BlockSpec((b, c, w), lambda i: (0, i, 0)),
                   pl.BlockSpec((b, c, w), lambda i: (0, n - 1 - i, 0)), st_spec],
        out_shape=[o_shape, o_shape, jax.ShapeDtypeStruct(s0.shape, F32)],
        scratch_shapes=[pltpu.VMEM(s0.shape, F32)],
        compiler_params=_params(("arbitrary",)),
        name="gdn_scan",
    )(*args, s0)


def _gate_consts(a_log, dt_bias):
    n_heads = a_log.shape[1]
    z = jnp.zeros((2, n_heads), F32)
    lay = lambda p: jnp.concatenate([z, p.astype(F32)], axis=1).reshape(-1)
    al, dt = lay(a_log), lay(dt_bias)
    pad = lambda vec: jnp.pad(vec, (0, HEAD_DIM - vec.shape[0])).reshape(1, HEAD_DIM)
    return pad(al), pad(dt)


def _outproj_kernel(x_ref, gate_ref, ya_ref, za0_ref, za1_ref, yb_ref, zb_ref, ocf_ref, ocb_ref, zc_ref, on_ref,
                    w_ref, lng_ref, lnb_ref, o_ref, mix_ref, *, alpha, wa, wb, n_c_heads):
    for i, za_ref in enumerate((za0_ref, za1_ref)):
        cs = slice(i * wa // 2, (i + 1) * wa // 2)
        mix_ref[:, cs] = (ya_ref[:, cs].astype(F32) * _silu(za_ref[...].astype(F32))).astype(BF16)
    mix_ref[:, wa:wa + wb] = (yb_ref[...].astype(F32) * _silu(zb_ref[...].astype(F32))).astype(BF16)
    for h in range(n_c_heads):
        sl = slice(h * HEAD_DIM, (h + 1) * HEAD_DIM)
        oc = ocf_ref[:, sl] + ocb_ref[:, sl]
        yc = oc * lax.rsqrt(jnp.mean(oc * oc, axis=-1, keepdims=True) + EPS) * on_ref[...]
        off = wa + wb + h * HEAD_DIM
        mix_ref[:, off:off + HEAD_DIM] = (yc * _silu(zc_ref[:, sl].astype(F32))).astype(BF16)
    y = _dot(mix_ref[...], w_ref[...])
    r = alpha * x_ref[...] + gate_ref[0] * y
    rc = r - jnp.mean(r, axis=-1, keepdims=True)
    o_ref[...] = rc * lax.rsqrt(jnp.mean(rc * rc, axis=-1, keepdims=True) + EPS) * lng_ref[...] + lnb_ref[...]


def _outproj_call(x2d, mod_l, ya, main, yb, ocf, ocb, o_norm, w_out, ln_g, ln_b, *, col_za, col_zb,
                  col_zc, alpha, tiles_per_batch, ctx_row, tm):
    m, d = x2d.shape
    wa, wb, wc = ya.shape[1], yb.shape[1], ocf.shape[1]
    if ctx_row is None:
        row = lambda i: i // tiles_per_batch
    else:
        row = lambda i: ctx_row
    rows = lambda w, cb=0: pl.BlockSpec((tm, w), lambda i: (i, cb))
    one = lambda w: pl.BlockSpec((1, w), lambda i: (0, 0))
    wz = wa // 2
    assert col_za % wz == 0 and col_zb % wb == 0 and col_zc % wc == 0
    return pl.pallas_call(
        functools.partial(_outproj_kernel, alpha=alpha, wa=wa, wb=wb, n_c_heads=wc // HEAD_DIM),
        grid=(m // tm,),
        in_specs=[rows(d), pl.BlockSpec((1, 1, d), lambda i: (row(i), 0, 2)),
                  rows(wa), rows(wz, col_za // wz), rows(wz, col_za // wz + 1), rows(wb), rows(wb, col_zb // wb),
                  rows(wc), rows(wc), rows(wc, col_zc // wc), one(HEAD_DIM),
                  pl.BlockSpec(w_out.shape, lambda i: (0, 0)), one(d), one(d)],
        out_specs=rows(d),
        out_shape=jax.ShapeDtypeStruct((m, d), F32),
        scratch_shapes=[pltpu.VMEM((tm, wa + wb + wc), BF16)],
        compiler_params=_params(("arbitrary",)),
        name="outproj",
    )(x2d, mod_l, ya, main, main, yb, main, ocf, ocb, main, o_norm.reshape(1, HEAD_DIM), w_out,
      ln_g.reshape(1, d), ln_b.reshape(1, d))


def kernel(x, c, ctx, c_ctx, w_mod, b_mod, w_in, q_norm, k_norm, rpb, conv_w, a_log, dt_bias, o_norm,
           w_out, ln_g, ln_b):
    bsz, seq, d = x.shape
    ctx_len = ctx.shape[1]
    depth = w_mod.shape[0]
    a_heads = d // (2 * HEAD_DIM)
    a_kv = a_heads // 4
    b_heads = d // (4 * HEAD_DIM)
    c_heads = d // (4 * HEAD_DIM)
    wa, wkv, wb, wc = a_heads * HEAD_DIM, a_kv * HEAD_DIM, b_heads * HEAD_DIM, c_heads * HEAD_DIM
    assert bsz < MOD_ROWS and seq % (NBR_Q_ROWS * GRID_W) == 0 and seq // GRID_W >= NBR_K_ROWS
    assert ctx_len % DN_CHUNK == 0 and 4 * c_heads <= 16
    alpha = (2 * depth) ** 0.25

    sizes = dict(qa=wa, ka=wkv, va=wkv, za=wa, qb=wb, kb=wb, vb=wb, zb=wb, qkvc=3 * wc, zc=wc)
    col, off = {}, 0
    for name, size in sizes.items():
        col[name] = off
        off += size
    n_main, n_ab = off, 4 * c_heads

    cc = jnp.concatenate([c, c_ctx[None], jnp.zeros((MOD_ROWS - bsz - 1, d), F32)], axis=0)
    mod = _mod_call(cc, w_mod, b_mod)
    tables = _rope_tables(seq)
    bias = _bias_call(rpb)

    tm_lat = 512
    tm_proj = 1024 if seq % 1024 == 0 else tm_lat
    tm_ctx = min(256, ctx_len)
    main_dtype = BF16
    xl = x.reshape(bsz * seq, d)
    xc = ctx.reshape(bsz * ctx_len, d)
    for l in range(depth):
        with_ctx_out = l < depth - 1
        mod_l = mod[l].reshape(MOD_ROWS, 1, 3 * d)
        w_all = w_in[l].astype(BF16)
        w_ab = jnp.pad(w_all[:, n_main:n_main + n_ab], ((0, 0), (0, HEAD_DIM - n_ab)))
        w_out_l = w_out[l].astype(BF16)
        gate_consts = _gate_consts(a_log[l], dt_bias[l])

        main_x, ab_x = _inproj_call(xc, mod_l, w_all, w_ab, n_main=n_main, tiles_per_batch=None,
                                    ctx_row=bsz, tm=tm_ctx, tn=512, out_dtype=main_dtype)
        main, ab = _inproj_call(xl, mod_l, w_all, w_ab, n_main=n_main, tiles_per_batch=seq // tm_proj,
                                ctx_row=None, tm=tm_proj, tn=512, out_dtype=main_dtype)
        main3 = main.reshape(bsz, seq, n_main)
        main_x3 = main_x.reshape(bsz, ctx_len, n_main)

        qa_x, ka_x = _aprep_call(main_x, q_norm[l], k_norm[l], None, col_q=col["qa"], col_k=col["ka"],
                                 nq=a_heads, nk=a_kv, tm=tm_ctx, tiles_per_batch=ctx_len // tm_ctx)
        qa, ka = _aprep_call(main, q_norm[l], k_norm[l], tables, col_q=col["qa"], col_k=col["ka"],
                             nq=a_heads, nk=a_kv, tm=tm_lat, tiles_per_batch=seq // tm_lat)
        ka_x3 = ka_x.reshape(bsz, ctx_len, wkv)
        ya = _gqa_call(qa.reshape(bsz, seq, wa), ka_x3, main_x3, col["va"], ka.reshape(bsz, seq, wkv), main3,
                       col["va"], n_kv_heads=a_kv, group=a_heads // a_kv, tq=256, tk=512, out_dtype=BF16)

        yb = _nbr_call(main3, main_x3, bias, layer=l, col_q=col["qb"], col_k=col["kb"], col_v=col["vb"],
                       n_heads=b_heads, out_dtype=BF16)

        def delta(main2d, ab2d, t_len, tm, s0):
            qn, kn, vv = _dnprep_call(main2d, conv_w[l], col=col["qkvc"], n_heads=c_heads, tm=tm,
                                      tiles_per_batch=t_len // tm)
            r3 = lambda a: a.reshape(bsz, t_len, a.shape[-1])
            local = _gdn_local_call(r3(qn), r3(kn), r3(vv), r3(ab2d), gate_consts, n_heads=c_heads,
                                    n_chunks=2, inv_precision=INV_PRECISION)
            return _gdn_scan_call(*local, s0, n_heads=c_heads)

        s_zero = jnp.zeros((bsz, 2 * c_heads, HEAD_DIM, HEAD_DIM), F32)
        ocf_x, ocb_x, s_ctx = delta(main_x, ab_x, ctx_len, tm_ctx, s_zero)
        ocf, ocb, _ = delta(main, ab, seq, tm_lat, s_ctx)

        x_new = _outproj_call(xl, mod_l, ya.reshape(bsz * seq, wa), main, yb.reshape(bsz * seq, wb),
                              ocf.reshape(bsz * seq, wc), ocb.reshape(bsz * seq, wc), o_norm[l], w_out_l,
                              ln_g[l], ln_b[l], col_za=col["za"], col_zb=col["zb"], col_zc=col["zc"],
                              alpha=alpha, tiles_per_batch=seq // 256, ctx_row=None, tm=256)

        if with_ctx_out:
            ya_x = _flash_call(qa_x.reshape(bsz, ctx_len, wa), [(ka_x3, 0, main_x3, col["va"])],
                               n_kv_heads=a_kv, group=a_heads // a_kv, q_col=0, tq=ctx_len, tk=ctx_len,
                               q_scale=None, out_dtype=F32)
            yb_x = _flash_call(main_x3, [(main_x3, col["kb"], main_x3, col["vb"])],
                               n_kv_heads=b_heads, group=1, q_col=col["qb"], tq=ctx_len, tk=ctx_len,
                               q_scale=SCORE_SCALE, out_dtype=F32)
            xc = _outproj_call(xc, mod_l, ya_x.reshape(bsz * ctx_len, wa), main_x,
                               yb_x.reshape(bsz * ctx_len, wb), ocf_x.reshape(bsz * ctx_len, wc),
                               ocb_x.reshape(bsz * ctx_len, wc), o_norm[l], w_out_l, ln_g[l], ln_b[l],
                               col_za=col["za"], col_zb=col["zb"], col_zc=col["zc"], alpha=alpha,
                               tiles_per_batch=None, ctx_row=bsz, tm=tm_ctx)
        xl = x_new
    return xl.reshape(bsz, seq, d)
```

```python
import functools

import jax
import jax.numpy as jnp
from jax import lax
from jax.experimental import pallas as pl
from jax.experimental.pallas import tpu as pltpu

F32 = jnp.float32
BF16 = jnp.bfloat16

HEAD_DIM = 128
GRID_W = 64
ROPE_THETA = 10000.0
NA_WIN_H = 8
NA_WIN_W = 16
CONV_K = 3
DN_CHUNK = 64
EPS = 1e-6
NEG = -1e30
MOD_ROWS = 8
NBR_Q_ROWS = 8
NBR_K_ROWS = 16
NBR_HALF_Q = 4
NBR_HALF_K = 12
VMEM_LIMIT = 56 * 1024 * 1024
INPROJ_VMEM_LIMIT = 60 * 1024 * 1024
LOG2E = 1.4426950408889634
SCORE_SCALE = HEAD_DIM ** -0.5 * LOG2E

HIGHEST = lax.Precision.HIGHEST
INV_PRECISION = None


def _silu(z):
    return z * jax.nn.sigmoid(z)


def _softplus(z):
    return jnp.maximum(z, 0.0) + jnp.log(1.0 + jnp.exp(-jnp.abs(z)))


def _dot(a, b, precision=None):
    return jnp.dot(a, b, preferred_element_type=F32, precision=precision)


def _dot_nt(a, b, precision=None):
    return lax.dot_general(a, b, (((1,), (1,)), ((), ())), preferred_element_type=F32,
                           precision=precision)


def _params(sem, vmem_limit=VMEM_LIMIT):
    return pltpu.CompilerParams(dimension_semantics=sem, vmem_limit_bytes=vmem_limit)


def _mod_kernel(c_ref, w_ref, b_ref, o_ref):
    s = _silu(c_ref[...]).astype(BF16)
    o_ref[0] = _dot(s, w_ref[0].astype(BF16)) + b_ref[0]


def _mod_call(cc, w_mod, b_mod):
    depth, d, n = w_mod.shape
    tn = 768
    return pl.pallas_call(
        _mod_kernel,
        grid=(depth, n // tn),
        in_specs=[pl.BlockSpec((MOD_ROWS, d), lambda l, j: (0, 0)),
                  pl.BlockSpec((1, d, tn), lambda l, j: (l, 0, j)),
                  pl.BlockSpec((1, 1, tn), lambda l, j: (l, 0, j))],
        out_specs=pl.BlockSpec((1, MOD_ROWS, tn), lambda l, j: (l, 0, j)),
        out_shape=jax.ShapeDtypeStruct((depth, MOD_ROWS, n), F32),
        compiler_params=_params(("arbitrary", "arbitrary")),
        name="mod",
    )(cc, w_mod, b_mod.reshape(depth, 1, n))


LN_ROWS = 256


def _inproj_kernel(x_ref, shift_ref, scale_ref, w_ref, wab_ref, o_ref, oab_ref, h_ref):
    @pl.when(pl.program_id(1) == 0)
    def _():
        for r0 in range(0, x_ref.shape[0], LN_ROWS):
            r = slice(r0, r0 + LN_ROWS)
            x = x_ref[r, :]
            xc = x - jnp.mean(x, axis=-1, keepdims=True)
            y = xc * lax.rsqrt(jnp.mean(xc * xc, axis=-1, keepdims=True) + EPS)
            h_ref[r, :] = (y * (1.0 + scale_ref[0]) + shift_ref[0]).astype(BF16)
            oab_ref[r, :] = _dot(h_ref[r, :], wab_ref[...])

    o_ref[...] = _dot(h_ref[...], w_ref[...]).astype(o_ref.dtype)


def _inproj_call(x2d, mod_l, w_all, w_ab, *, n_main, tiles_per_batch, ctx_row, tm, tn, out_dtype):
    m, d = x2d.shape
    n = n_main
    assert n % tn == 0 and tm % LN_ROWS == 0
    if ctx_row is None:
        row = lambda i: i // tiles_per_batch
    else:
        row = lambda i: ctx_row
    return pl.pallas_call(
        _inproj_kernel,
        grid=(m // tm, n // tn),
        in_specs=[pl.BlockSpec((tm, d), lambda i, j: (i, 0)),
                  pl.BlockSpec((1, 1, d), lambda i, j: (row(i), 0, 0)),
                  pl.BlockSpec((1, 1, d), lambda i, j: (row(i), 0, 1)),
                  pl.BlockSpec((d, tn), lambda i, j: (0, j)),
                  pl.BlockSpec((d, HEAD_DIM), lambda i, j: (0, 0))],
        out_specs=[pl.BlockSpec((tm, tn), lambda i, j: (i, j)),
                   pl.BlockSpec((tm, HEAD_DIM), lambda i, j: (i, 0))],
        out_shape=[jax.ShapeDtypeStruct((m, n), out_dtype),
                   jax.ShapeDtypeStruct((m, HEAD_DIM), F32)],
        scratch_shapes=[pltpu.VMEM((tm, d), BF16)],
        compiler_params=_params(("arbitrary", "arbitrary"), INPROJ_VMEM_LIMIT),
        name="inproj",
    )(x2d, mod_l, mod_l, w_all, w_ab)


def _swap_quarters(x):
    lane = lax.broadcasted_iota(jnp.int32, x.shape, 1)
    fwd = pltpu.roll(x, 3 * HEAD_DIM // 4, axis=1)
    bwd = pltpu.roll(x, HEAD_DIM // 4, axis=1)
    return jnp.where((lane % (HEAD_DIM // 2)) < HEAD_DIM // 4, fwd, bwd)


def _aprep_kernel(*refs, nq, nk, rope):
    if rope:
        q_ref, k_ref, qw_ref, kw_ref, cos_ref, sin_ref, qo_ref, ko_ref = refs
        cos, sin = cos_ref[...], sin_ref[...]
    else:
        q_ref, k_ref, qw_ref, kw_ref, qo_ref, ko_ref = refs

    def norm(x, w):
        y = x * lax.rsqrt(jnp.mean(x * x, axis=-1, keepdims=True) + EPS) * w
        if rope:
            y = y * cos + _swap_quarters(y) * sin
        return y

    for h in range(nq):
        sl = slice(h * HEAD_DIM, (h + 1) * HEAD_DIM)
        y = norm(q_ref[:, sl].astype(F32), qw_ref[...])
        qo_ref[:, sl] = (y * SCORE_SCALE).astype(qo_ref.dtype)
    for h in range(nk):
        sl = slice(h * HEAD_DIM, (h + 1) * HEAD_DIM)
        ko_ref[:, sl] = norm(k_ref[:, sl].astype(F32), kw_ref[...]).astype(ko_ref.dtype)


def _aprep_call(main, q_norm, k_norm, tables, *, col_q, col_k, nq, nk, tm, tiles_per_batch):
    m = main.shape[0]
    wq, wk = nq * HEAD_DIM, nk * HEAD_DIM
    in_specs = [pl.BlockSpec((tm, wq), lambda i: (i, col_q // wq)),
                pl.BlockSpec((tm, wk), lambda i: (i, col_k // wk)),
                pl.BlockSpec((1, HEAD_DIM), lambda i: (0, 0)),
                pl.BlockSpec((1, HEAD_DIM), lambda i: (0, 0))]
    args = [main, main, q_norm.reshape(1, HEAD_DIM), k_norm.reshape(1, HEAD_DIM)]
    if tables is not None:
        in_specs += [pl.BlockSpec((tm, HEAD_DIM), lambda i: (i % tiles_per_batch, 0))] * 2
        args += list(tables)
    return pl.pallas_call(
        functools.partial(_aprep_kernel, nq=nq, nk=nk, rope=tables is not None),
        grid=(m // tm,),
        in_specs=in_specs,
        out_specs=[pl.BlockSpec((tm, wq), lambda i: (i, 0)),
                   pl.BlockSpec((tm, wk), lambda i: (i, 0))],
        out_shape=[jax.ShapeDtypeStruct((m, wq), BF16), jax.ShapeDtypeStruct((m, wk), BF16)],
        compiler_params=_params(("arbitrary",)),
        name="aprep",
    )(*args)


def _rope_tables(n_tokens):
    t = jnp.arange(n_tokens, dtype=jnp.int32)
    row = (t // GRID_W).astype(F32)
    col = (t % GRID_W).astype(F32)
    half = HEAD_DIM // 2
    inv_freq = ROPE_THETA ** (-jnp.arange(0, half, 2, dtype=F32) / half)
    ar, ac = row[:, None] * inv_freq, col[:, None] * inv_freq
    cos = jnp.concatenate([jnp.cos(ar), jnp.cos(ar), jnp.cos(ac), jnp.cos(ac)], -1)
    sin = jnp.concatenate([-jnp.sin(ar), jnp.sin(ar), -jnp.sin(ac), jnp.sin(ac)], -1)
    return cos, sin


def _flash_kernel(*refs, group, tq, sources, q_scale):
    q_ref = refs[0]
    o_ref = refs[1 + 2 * len(sources)]
    q = jnp.concatenate([q_ref[0, :, g * HEAD_DIM:(g + 1) * HEAD_DIM] for g in range(group)], axis=0)
    if q_scale is not None:
        q = q.astype(F32) * q_scale
    q = q.astype(BF16)
    rows = group * tq

    def step(carry, k, v):
        m, l, acc = carry
        s = _dot_nt(q, k.astype(BF16))
        m_new = jnp.maximum(m, jnp.max(s, axis=-1, keepdims=True))
        alpha = jnp.exp2(m - m_new)
        p = jnp.exp2(s - m_new)
        l = alpha * l + jnp.sum(p, axis=-1, keepdims=True)
        acc = alpha * acc + _dot(p.astype(BF16), v.astype(BF16))
        return m_new, l, acc

    carry = (jnp.full((rows, 1), -jnp.inf, F32), jnp.zeros((rows, 1), F32),
             jnp.zeros((rows, HEAD_DIM), F32))
    for si, (length, tk) in enumerate(sources):
        k_ref, v_ref = refs[1 + 2 * si], refs[2 + 2 * si]
        if length == tk:
            carry = step(carry, k_ref[0], v_ref[0])
        else:
            def body(i, c, k_ref=k_ref, v_ref=v_ref, tk=tk):
                st = pl.multiple_of(i * tk, tk)
                return step(c, k_ref[0, pl.ds(st, tk), :], v_ref[0, pl.ds(st, tk), :])
            carry = lax.fori_loop(0, length // tk, body, carry)
    _, l, acc = carry
    o = acc / l
    for g in range(group):
        o_ref[0, :, g * HEAD_DIM:(g + 1) * HEAD_DIM] = o[g * tq:(g + 1) * tq].astype(o_ref.dtype)


SOFTMAX_UNIT = 16


def _gqa_kernel(q_ref, kc_ref, vc_ref, kl_ref, vl_ref, o_ref, q4_ref, s0_ref, s1_ref, sc_ref, p_ref,
                m_ref, l_ref, a_ref, acc_ref, *, group, tq, tk):
    rows = group * tq
    half = rows // 2
    n_lat = kl_ref.shape[1] // tk
    for g in range(group):
        q4_ref[g * tq:(g + 1) * tq, :] = q_ref[0, :, g * HEAD_DIM:(g + 1) * HEAD_DIM]
    m_ref[...] = jnp.full(m_ref.shape, -jnp.inf, F32)
    l_ref[...] = jnp.zeros(l_ref.shape, F32)
    acc_ref[...] = jnp.zeros(acc_ref.shape, F32)
    halves = (slice(0, half), slice(half, rows))

    def issue(dst_ref, k, rs):
        dst_ref[rs, :] = _dot_nt(q4_ref[rs, :], k)

    def softmax(src_ref, rs, width):
        for r0 in range(rs.start, rs.stop, SOFTMAX_UNIT):
            r = slice(r0, r0 + SOFTMAX_UNIT)
            cols = [slice(j, j + HEAD_DIM) for j in range(0, width, HEAD_DIM)]
            s = [src_ref[r, c] for c in cols]
            mx = functools.reduce(jnp.maximum, s)
            m_old = m_ref[r, :]
            m_new = jnp.maximum(m_old, jnp.max(mx, axis=-1, keepdims=True))
            alpha = jnp.exp2(m_old - m_new)
            p = [jnp.exp2(x - m_new) for x in s]
            for c, x in zip(cols, p):
                p_ref[r, c] = x.astype(BF16)
            m_ref[r, :] = m_new
            a_ref[r, :] = alpha
            l_ref[r, :] = alpha * l_ref[r, :] + functools.reduce(jnp.add, p)

    def accumulate(rs, width, v):
        pv = _dot(p_ref[rs, 0:width], v)
        acc_ref[rs, :] = acc_ref[rs, :] * a_ref[rs, :] + pv

    def chunk_step(cur_ref, width, v, nxt_ref, k_next):
        for rs in halves:
            if nxt_ref is not None:
                issue(nxt_ref, k_next, rs)
            softmax(cur_ref, rs, width)
            accumulate(rs, width, v)

    lat = lambda ref, i: ref[0, pl.ds(pl.multiple_of(i * tk, tk), tk), :]
    for rs in halves:
        issue(s0_ref, lat(kl_ref, 0), rs)

    def pair(j, carry):
        i = 2 * j
        chunk_step(s0_ref, tk, lat(vl_ref, i), s1_ref, lat(kl_ref, i + 1))
        chunk_step(s1_ref, tk, lat(vl_ref, i + 1), s0_ref, lat(kl_ref, i + 2))
        return carry

    lax.fori_loop(0, n_lat // 2 - 1, pair, 0)
    chunk_step(s0_ref, tk, lat(vl_ref, n_lat - 2), s1_ref, lat(kl_ref, n_lat - 1))
    chunk_step(s1_ref, tk, lat(vl_ref, n_lat - 1), sc_ref, kc_ref[0])
    chunk_step(sc_ref, kc_ref.shape[1], vc_ref[0], None, None)

    o = acc_ref[...] / jnp.sum(l_ref[...], axis=-1, keepdims=True)
    for g in range(group):
        o_ref[0, :, g * HEAD_DIM:(g + 1) * HEAD_DIM] = o[g * tq:(g + 1) * tq].astype(o_ref.dtype)


def _gqa_call(q, k_ctx, v_ctx, v_ctx_col, k_lat, v_lat, v_lat_col, *, n_kv_heads, group, tq, tk, out_dtype):
    b, s, _ = q.shape
    ctx_len = k_ctx.shape[1]
    wq = group * HEAD_DIM
    rows = group * tq
    assert s % (2 * tk) == 0 and s // tk >= 2 and ctx_len % HEAD_DIM == 0 and ctx_len <= tk
    assert rows % (2 * SOFTMAX_UNIT) == 0
    cvc, cvl = v_ctx_col // HEAD_DIM, v_lat_col // HEAD_DIM
    kv_spec = lambda t, c: pl.BlockSpec((1, t, HEAD_DIM), lambda bi, g, qi: (bi, 0, c + g))
    stat = pltpu.VMEM((rows, HEAD_DIM), F32)
    return pl.pallas_call(
        functools.partial(_gqa_kernel, group=group, tq=tq, tk=tk),
        grid=(b, n_kv_heads, s // tq),
        in_specs=[pl.BlockSpec((1, tq, wq), lambda bi, g, qi: (bi, qi, g)),
                  kv_spec(ctx_len, 0), kv_spec(ctx_len, cvc), kv_spec(s, 0), kv_spec(s, cvl)],
        out_specs=pl.BlockSpec((1, tq, wq), lambda bi, g, qi: (bi, qi, g)),
        out_shape=jax.ShapeDtypeStruct((b, s, n_kv_heads * wq), out_dtype),
        scratch_shapes=[pltpu.VMEM((rows, HEAD_DIM), BF16), pltpu.VMEM((rows, tk), F32),
                        pltpu.VMEM((rows, tk), F32), pltpu.VMEM((rows, ctx_len), F32),
                        pltpu.VMEM((rows, tk), BF16), stat, stat, stat, stat],
        compiler_params=_params(("arbitrary", "arbitrary", "arbitrary")),
        name="gqa",
    )(q, k_ctx, v_ctx, k_lat, v_lat)


def _flash_call(q, kv, *, n_kv_heads, group, q_col, tq, tk, q_scale, out_dtype):
    b, t_q, _ = q.shape
    wq = group * HEAD_DIM
    in_specs = [pl.BlockSpec((1, tq, wq), lambda bi, g, qi: (bi, qi, q_col // wq + g))]
    args = [q]
    sources = []
    for k_arr, k_col, v_arr, v_col in kv:
        t_k = k_arr.shape[1]
        in_specs.append(pl.BlockSpec((1, t_k, HEAD_DIM),
                                     lambda bi, g, qi, c=k_col // HEAD_DIM: (bi, 0, c + g)))
        in_specs.append(pl.BlockSpec((1, t_k, HEAD_DIM),
                                     lambda bi, g, qi, c=v_col // HEAD_DIM: (bi, 0, c + g)))
        args += [k_arr, v_arr]
        sources.append((t_k, min(tk, t_k)))
    return pl.pallas_call(
        functools.partial(_flash_kernel, group=group, tq=tq, sources=tuple(sources), q_scale=q_scale),
        grid=(b, n_kv_heads, t_q // tq),
        in_specs=in_specs,
        out_specs=pl.BlockSpec((1, tq, wq), lambda bi, g, qi: (bi, qi, g)),
        out_shape=jax.ShapeDtypeStruct((b, t_q, n_kv_heads * wq), out_dtype),
        compiler_params=_params(("arbitrary", "arbitrary", "arbitrary")),
        name="flash",
    )(*args)


def _nbr_window(cls, qr, kr):
    if cls == 0:
        lo, dr = max(qr - NA_WIN_H // 2, 0), kr - qr
    elif cls == 1:
        lo, dr = qr, kr - NA_WIN_H // 2 - qr
    else:
        lo, dr = min(NA_WIN_H // 2 + qr, NBR_K_ROWS - NA_WIN_H), kr - NBR_Q_ROWS - qr
    if lo <= kr < lo + NA_WIN_H:
        assert -NA_WIN_H < dr < NA_WIN_H
        return dr + NA_WIN_H - 1
    return None


def _nbr_half_offset(cls, hf):
    return (0, (0, NBR_HALF_Q)[hf], NBR_HALF_Q)[cls]


def _bias_kernel(rpb_ref, o_ref, *, n_heads):
    nrow, ncol = 2 * NA_WIN_H - 1, 2 * NA_WIN_W - 1
    base = (pl.program_id(0) * n_heads + pl.program_id(1)) * (nrow * ncol)
    shape = (GRID_W, 2 * GRID_W)
    c = lax.broadcasted_iota(jnp.int32, shape, 0)
    lane = lax.broadcasted_iota(jnp.int32, shape, 1)
    right = lane >= GRID_W
    kc = jnp.where(right, lane - GRID_W, lane)
    c0 = jnp.clip(c - NA_WIN_W // 2, 0, GRID_W - NA_WIN_W)
    col_ok = (kc >= c0) & (kc < c0 + NA_WIN_W)
    idx = kc - c + (NA_WIN_W - 1)
    cache = {}

    def tile(a_left, a_right):
        key = (a_left, a_right)
        if key not in cache:
            t = jnp.full(shape, NEG, F32)
            if key != (None, None):
                for j in range(ncol):
                    sl = NEG if a_left is None else rpb_ref[base + a_left * ncol + j]
                    sr = NEG if a_right is None else rpb_ref[base + a_right * ncol + j]
                    t = jnp.where(idx == j, jnp.where(right, sr, sl), t)
                t = jnp.where(col_ok, t * LOG2E, NEG)
            cache[key] = t
        return cache[key]

    for cls in range(3):
        for hf in range(2):
            off = _nbr_half_offset(cls, hf)
            assert all(off <= kr < off + NBR_HALF_K for q in range(hf * NBR_HALF_Q, (hf + 1) * NBR_HALF_Q)
                       for kr in range(NBR_K_ROWS) if _nbr_window(cls, q, kr) is not None)
            for qr in range(NBR_HALF_Q):
                for p in range(NBR_HALF_K // 2):
                    kr = _nbr_half_offset(cls, hf) + 2 * p
                    t = tile(_nbr_window(cls, hf * NBR_HALF_Q + qr, kr),
                             _nbr_window(cls, hf * NBR_HALF_Q + qr, kr + 1))
                    o_ref[0, 0, cls, hf, qr * GRID_W:(qr + 1) * GRID_W, p * 2 * GRID_W:(p + 1) * 2 * GRID_W] = t


def _bias_call(rpb):
    depth, n_heads = rpb.shape[:2]
    shape = (depth, n_heads, 3, 2, NBR_HALF_Q * GRID_W, NBR_HALF_K * GRID_W)
    return pl.pallas_call(
        functools.partial(_bias_kernel, n_heads=n_heads),
        grid=(depth, n_heads),
        in_specs=[pl.BlockSpec(memory_space=pltpu.SMEM)],
        out_specs=pl.BlockSpec((1, 1) + shape[2:], lambda l, h: (l, h, 0, 0, 0, 0)),
        out_shape=jax.ShapeDtypeStruct(shape, F32),
        compiler_params=_params(("arbitrary", "arbitrary")),
        name="nbr_bias",
    )(rpb.reshape(-1))


def _nbr_kernel(q_ref, k_ref, v_ref, kc_ref, vc_ref, bias_ref, o_ref, *, rows):
    g = pl.program_id(2)
    n_steps = rows // NBR_Q_ROWS
    start_row = jnp.clip(g * NBR_Q_ROWS - NA_WIN_H // 2, 0, rows - NBR_K_ROWS)
    start = pl.multiple_of(start_row * GRID_W, (NA_WIN_H // 2) * GRID_W)
    cls = jnp.where(g == 0, 0, jnp.where(g == n_steps - 1, 2, 1))
    nq, nk = NBR_HALF_Q * GRID_W, NBR_HALF_K * GRID_W
    kc, vc = kc_ref[0].astype(BF16), vc_ref[0].astype(BF16)
    scores = []
    for hf in range(2):
        off = jnp.where(cls == 0, 0, jnp.where(cls == 1, hf * NBR_HALF_Q, NBR_HALF_Q))
        st = pl.multiple_of(start + off * GRID_W, NBR_HALF_Q * GRID_W)
        q = (q_ref[0, hf * nq:(hf + 1) * nq, :].astype(F32) * SCORE_SCALE).astype(BF16)
        s_win = _dot_nt(q, k_ref[0, pl.ds(st, nk), :].astype(BF16)) + bias_ref[0, 0, cls, hf]
        scores.append((st, s_win, _dot_nt(q, kc)))
    for hf, (st, s_win, s_ctx) in enumerate(scores):
        m = jnp.maximum(jnp.max(s_win, axis=-1, keepdims=True), jnp.max(s_ctx, axis=-1, keepdims=True))
        p_win = jnp.exp2(s_win - m)
        p_ctx = jnp.exp2(s_ctx - m)
        l = jnp.sum(p_win, axis=-1, keepdims=True) + jnp.sum(p_ctx, axis=-1, keepdims=True)
        o = _dot(p_win.astype(BF16), v_ref[0, pl.ds(st, nk), :].astype(BF16)) + _dot(p_ctx.astype(BF16), vc)
        o_ref[0, hf * nq:(hf + 1) * nq, :] = (o / l).astype(o_ref.dtype)


def _nbr_call(main, main_ctx, bias, *, layer, col_q, col_k, col_v, n_heads, out_dtype):
    b, s, _ = main.shape
    ctx_len = main_ctx.shape[1]
    rows = s // GRID_W
    tq = NBR_Q_ROWS * GRID_W
    cq, ck, cv = col_q // HEAD_DIM, col_k // HEAD_DIM, col_v // HEAD_DIM
    return pl.pallas_call(
        functools.partial(_nbr_kernel, rows=rows),
        grid=(n_heads, b, rows // NBR_Q_ROWS),
        in_specs=[pl.BlockSpec((1, tq, HEAD_DIM), lambda h, bi, g: (bi, g, cq + h)),
                  pl.BlockSpec((1, s, HEAD_DIM), lambda h, bi, g: (bi, 0, ck + h)),
                  pl.BlockSpec((1, s, HEAD_DIM), lambda h, bi, g: (bi, 0, cv + h)),
                  pl.BlockSpec((1, ctx_len, HEAD_DIM), lambda h, bi, g: (bi, 0, ck + h)),
                  pl.BlockSpec((1, ctx_len, HEAD_DIM), lambda h, bi, g: (bi, 0, cv + h)),
                  pl.BlockSpec((1, 1) + bias.shape[2:], lambda h, bi, g: (layer, h, 0, 0, 0, 0))],
        out_specs=pl.BlockSpec((1, tq, HEAD_DIM), lambda h, bi, g: (bi, g, h)),
        out_shape=jax.ShapeDtypeStruct((b, s, n_heads * HEAD_DIM), out_dtype),
        compiler_params=_params(("arbitrary", "arbitrary", "arbitrary")),
        name="nbr_attn",
    )(main, main, main, main_ctx, main_ctx, bias)


def _dnprep_kernel(x_ref, prev_ref, next_ref, w_ref, q_ref, k_ref, v_ref, *, tiles_per_batch, n_heads):
    i = pl.program_id(0) % tiles_per_batch
    x = x_ref[...].astype(F32)
    tm = x.shape[0]
    row = lax.broadcasted_iota(jnp.int32, x.shape, 0)
    halo = prev_ref.shape[0]
    prev_row = jnp.where(i == 0, 0.0, prev_ref[halo - 1:halo, :].astype(F32))
    next_row = jnp.where(i == tiles_per_batch - 1, 0.0, next_ref[0:1, :].astype(F32))
    x_prev = jnp.where(row == 0, prev_row, pltpu.roll(x, 1, axis=0))
    x_next = jnp.where(row == tm - 1, next_row, pltpu.roll(x, tm - 1, axis=0))
    y = _silu(w_ref[0:1, :] * x_prev + w_ref[1:2, :] * x + w_ref[2:3, :] * x_next)
    w = n_heads * HEAD_DIM
    for h in range(n_heads):
        sl = slice(h * HEAD_DIM, (h + 1) * HEAD_DIM)
        qh = y[:, h * HEAD_DIM:(h + 1) * HEAD_DIM]
        kh = y[:, w + h * HEAD_DIM:w + (h + 1) * HEAD_DIM]
        qn = qh * lax.rsqrt(jnp.sum(qh * qh, axis=-1, keepdims=True) + EPS) * HEAD_DIM ** -0.5
        q_ref[:, sl] = qn.astype(q_ref.dtype)
        k_ref[:, sl] = (kh * lax.rsqrt(jnp.sum(kh * kh, axis=-1, keepdims=True) + EPS)).astype(k_ref.dtype)
    v_ref[...] = y[:, 2 * w:].astype(v_ref.dtype)


def _dnprep_call(main, conv_w, *, col, n_heads, tm, tiles_per_batch):
    m = main.shape[0]
    w3 = 3 * n_heads * HEAD_DIM
    halo = 16
    hb = tm // halo
    n_halo = m // halo
    out = jax.ShapeDtypeStruct((m, n_heads * HEAD_DIM), BF16)
    o_spec = pl.BlockSpec((tm, n_heads * HEAD_DIM), lambda i: (i, 0))
    return pl.pallas_call(
        functools.partial(_dnprep_kernel, tiles_per_batch=tiles_per_batch, n_heads=n_heads),
        grid=(m // tm,),
        in_specs=[pl.BlockSpec((tm, w3), lambda i: (i, col // w3)),
                  pl.BlockSpec((halo, w3), lambda i: (jnp.maximum(i * hb - 1, 0), col // w3)),
                  pl.BlockSpec((halo, w3), lambda i: (jnp.minimum((i + 1) * hb, n_halo - 1), col // w3)),
                  pl.BlockSpec((CONV_K, w3), lambda i: (0, 0))],
        out_specs=[o_spec, o_spec, o_spec],
        out_shape=[out, out, out],
        compiler_params=_params(("arbitrary",)),
        name="dn_prep",
    )(main, main, main, conv_w)


def _tri_masks(n, upper):
    i = lax.broadcasted_iota(jnp.int32, (n, n), 0)
    j = lax.broadcasted_iota(jnp.int32, (n, n), 1)
    incl = (j >= i) if upper else (j <= i)
    strict = (j > i) if upper else (j < i)
    eye = (i == j).astype(F32)
    base = 8
    same_base = (i // base) == (j // base)
    levels = []
    s = base
    while s < n:
        levels.append(((i // (2 * s)) == (j // (2 * s))) & ((i // s) != (j // s)))
        s *= 2
    return incl, strict, eye, same_base, levels


def _unit_tri_inverse(lms, masks, precision):
    mm = lambda a, b: _dot(a, b, precision) if precision is not None else _dot(a.astype(BF16), b.astype(BF16))
    eyes = [m[2] for m in masks]
    l1 = [jnp.where(m[3], lm, 0.0) for lm, m in zip(lms, masks)]
    l2 = [mm(a, a) for a in l1]
    l4 = [mm(a, a) for a in l2]
    t = [mm(e - a, e + b) for e, a, b in zip(eyes, l1, l2)]
    t = [mm(a, e + b) for e, a, b in zip(eyes, t, l4)]
    for lvl in range(len(masks[0][4])):
        cs = [jnp.where(m[4][lvl], lm, 0.0) for lm, m in zip(lms, masks)]
        tc = [mm(a, b) for a, b in zip(t, cs)]
        t = [a - mm(b, a) for a, b in zip(t, tc)]
    return t


def _gdn_local_kernel(q_ref, k_ref, v_ref, ab_ref, alr_ref, dtr_ref,
                      wq_ref, u_ref, kdt_ref, qk_ref, eg_ref, *, n_heads, n_chunks, inv_precision):
    c = DN_CHUNK
    masks = [_tri_masks(c, upper) for upper in (False, True)]
    gates = {}
    for d in range(2):
        cum = masks[d][0].astype(F32)
        for ci in range(n_chunks):
            ab = ab_ref[0, ci * c:(ci + 1) * c, :]
            g_tok = -jnp.exp(alr_ref[...]) * _softplus(ab + dtr_ref[...])
            gc_tok = _dot(cum, g_tok, HIGHEST)
            gates[ci, d] = (jax.nn.sigmoid(ab), gc_tok, gc_tok.T)

    streams = [(ci, d, h) for ci in range(n_chunks) for d in range(2) for h in range(n_heads)]
    pre = []
    for ci, d, h in streams:
        incl, strict = masks[d][0], masks[d][1]
        rows = slice(ci * c, (ci + 1) * c)
        sl = slice(h * HEAD_DIM, (h + 1) * HEAD_DIM)
        cb, cg = d * 2 * n_heads + h, d * 2 * n_heads + n_heads + h
        beta_tok, gc_tok, gc_rows = gates[ci, d]
        q, k, v = q_ref[0, rows, sl], k_ref[0, rows, sl], v_ref[0, rows, sl]
        beta_c = beta_tok[:, cb:cb + 1]
        gc_c = gc_tok[:, cg:cg + 1]
        last = 0 if d == 1 else c - 1
        g_last = gc_c[last:last + 1, :]
        decay = jnp.where(incl, jnp.exp(jnp.where(incl, gc_c - gc_rows[cg:cg + 1, :], 0.0)), 0.0)
        kbeta = k * beta_c
        e_c = jnp.exp(gc_c)
        wq_ref[0, d, ci, c:2 * c, sl] = (q * e_c).astype(wq_ref.dtype)
        kdt_ref[0, d, ci, h] = (k * jnp.exp(g_last - gc_c)).T.astype(kdt_ref.dtype)
        eg_ref[0, d, ci, h:h + 1, :] = jnp.broadcast_to(jnp.exp(g_last), (1, HEAD_DIM))
        lhs = jnp.concatenate([kbeta, q], axis=0).astype(BF16)
        rhs = jnp.concatenate([v * beta_c, kbeta * e_c], axis=1).astype(BF16)
        pre.append((lhs, k.astype(BF16), rhs, decay))
    kq = [_dot_nt(lhs, kb) for lhs, kb, _, _ in pre]
    lms = []
    for (ci, d, h), x, (_, _, _, decay) in zip(streams, kq, pre):
        qk_ref[0, d, ci, h] = (x[c:] * decay).astype(qk_ref.dtype)
        lms.append(jnp.where(masks[d][1], x[:c] * decay, 0.0))
    ts = _unit_tri_inverse(lms, [masks[d] for _, d, _ in streams], inv_precision)
    uws = [_dot(t.astype(BF16), rhs) for t, (_, _, rhs, _) in zip(ts, pre)]
    for (ci, d, h), uw in zip(streams, uws):
        sl = slice(h * HEAD_DIM, (h + 1) * HEAD_DIM)
        u_ref[0, d, ci * c:(ci + 1) * c, sl] = uw[:, :HEAD_DIM]
        wq_ref[0, d, ci, 0:c, sl] = uw[:, HEAD_DIM:].astype(wq_ref.dtype)


def _gdn_local_call(qn, kn, vv, ab, gate_consts, *, n_heads, n_chunks, inv_precision):
    b, t, w = qn.shape
    c = DN_CHUNK
    n = t // c
    tm = n_chunks * c
    tok = pl.BlockSpec((1, tm, w), lambda bi, i: (bi, i, 0))
    const = lambda a: pl.BlockSpec(a.shape, lambda bi, i: (0,) * a.ndim)
    return pl.pallas_call(
        functools.partial(_gdn_local_kernel, n_heads=n_heads, n_chunks=n_chunks, inv_precision=inv_precision),
        grid=(b, n // n_chunks),
        in_specs=[tok, tok, tok, pl.BlockSpec((1, tm, HEAD_DIM), lambda bi, i: (bi, i, 0))]
                 + [const(a) for a in gate_consts],
        out_specs=[pl.BlockSpec((1, 2, n_chunks, 2 * c, w), lambda bi, i: (bi, 0, i, 0, 0)),
                   pl.BlockSpec((1, 2, tm, w), lambda bi, i: (bi, 0, i, 0)),
                   pl.BlockSpec((1, 2, n_chunks, n_heads, HEAD_DIM, c), lambda bi, i: (bi, 0, i, 0, 0, 0)),
                   pl.BlockSpec((1, 2, n_chunks, n_heads, c, c), lambda bi, i: (bi, 0, i, 0, 0, 0)),
                   pl.BlockSpec((1, 2, n_chunks, n_heads, HEAD_DIM), lambda bi, i: (bi, 0, i, 0, 0))],
        out_shape=[jax.ShapeDtypeStruct((b, 2, n, 2 * c, w), BF16),
                   jax.ShapeDtypeStruct((b, 2, t, w), F32),
                   jax.ShapeDtypeStruct((b, 2, n, n_heads, HEAD_DIM, c), BF16),
                   jax.ShapeDtypeStruct((b, 2, n, n_heads, c, c), BF16),
                   jax.ShapeDtypeStruct((b, 2, n, n_heads, HEAD_DIM), F32)],
        compiler_params=_params(("arbitrary", "arbitrary")),
        name="gdn_local",
    )(qn, kn, vv, ab, *gate_consts)


def _gdn_scan_kernel(wqf_ref, wqb_ref, uf_ref, ub_ref, kdtf_ref, kdtb_ref, qkf_ref, qkb_ref, egf_ref,
                     egb_ref, s0_ref, of_ref, ob_ref, sfin_ref, s_ref, *, n_heads, bsz):
    n = pl.program_id(0)
    c = DN_CHUNK

    @pl.when(n == 0)
    def _():
        s_ref[...] = s0_ref[...]

    refs = ((wqf_ref, uf_ref, kdtf_ref, qkf_ref, egf_ref, of_ref),
            (wqb_ref, ub_ref, kdtb_ref, qkb_ref, egb_ref, ob_ref))
    streams = [(b, d, h) for b in range(bsz) for d in range(2) for h in range(n_heads)]
    cols = lambda h: slice(h * HEAD_DIM, (h + 1) * HEAD_DIM)
    ws = [_dot(refs[d][0][b, 0, 0, :, cols(h)], s_ref[b, d * n_heads + h].astype(BF16))
          for b, d, h in streams]
    v_new = [(refs[d][1][b, 0, :, cols(h)] - x[:c]).astype(BF16) for (b, d, h), x in zip(streams, ws)]
    intra = [_dot(refs[d][3][b, 0, 0, h], v) for (b, d, h), v in zip(streams, v_new)]
    upd = [_dot(refs[d][2][b, 0, 0, h], v) for (b, d, h), v in zip(streams, v_new)]
    for (b, d, h), x, y, z in zip(streams, ws, intra, upd):
        refs[d][5][b, :, cols(h)] = x[c:] + y
        i = d * n_heads + h
        s_ref[b, i] = s_ref[b, i] * refs[d][4][b, 0, 0, h:h + 1, :] + z

    @pl.when(n == pl.num_programs(0) - 1)
    def _():
        sfin_ref[...] = s_ref[...]


def _gdn_scan_call(wq, u, kdt, qk, eg, s0, *, n_heads):
    b, _, n, _, w = wq.shape
    c = DN_CHUNK
    t = n * c

    def pair(arr, blk):
        nd = len(blk)
        fwd = pl.BlockSpec((b, 1) + blk, lambda i: (0, 0, i) + (0,) * (nd - 1))
        bwd = pl.BlockSpec((b, 1) + blk, lambda i: (0, 1, n - 1 - i) + (0,) * (nd - 1))
        return [fwd, bwd], [arr, arr]

    specs, args = [], []
    for arr, blk in ((wq, (1, 2 * c, w)), (u, (c, w)), (kdt, (1, n_heads, HEAD_DIM, c)),
                     (qk, (1, n_heads, c, c)), (eg, (1, n_heads, HEAD_DIM))):
        sp, ar = pair(arr, blk)
        specs += sp
        args += ar
    st_spec = pl.BlockSpec(s0.shape, lambda i: (0, 0, 0, 0))
    o_shape = jax.ShapeDtypeStruct((b, t, w), F32)
    return pl.pallas_call(
        functools.partial(_gdn_scan_kernel, n_heads=n_heads, bsz=b),
        grid=(n,),
        in_specs=specs + [st_spec],
        out_specs=[pl.BlockSpec((b, c, w), lambda i: (0, i, 0)),
                   pl.BlockSpec((b, c, w), lambda i: (0, n - 1 - i, 0)), st_spec],
        out_shape=[o_shape, o_shape, jax.ShapeDtypeStruct(s0.shape, F32)],
        scratch_shapes=[pltpu.VMEM(s0.shape, F32)],
        compiler_params=_params(("arbitrary",)),
        name="gdn_scan",
    )(*args, s0)


def _gate_consts(a_log, dt_bias):
    n_heads = a_log.shape[1]
    z = jnp.zeros((2, n_heads), F32)
    lay = lambda p: jnp.concatenate([z, p.astype(F32)], axis=1).reshape(-1)
    al, dt = lay(a_log), lay(dt_bias)
    pad = lambda vec: jnp.pad(vec, (0, HEAD_DIM - vec.shape[0])).reshape(1, HEAD_DIM)
    return pad(al), pad(dt)


OUT_ROWS = 256


def _outproj_kernel(x_ref, gate_ref, ya_ref, za0_ref, za1_ref, yb_ref, zb_ref, ocf_ref, ocb_ref, zc_ref, on_ref,
                    w_ref, lng_ref, lnb_ref, o_ref, mix_ref, y_ref, *, alpha, wa, wb, n_c_heads):
    blocks = [slice(r0, r0 + OUT_ROWS) for r0 in range(0, x_ref.shape[0], OUT_ROWS)]

    def project(r):
        for i, za_ref in enumerate((za0_ref, za1_ref)):
            cs = slice(i * wa // 2, (i + 1) * wa // 2)
            mix_ref[r, cs] = (ya_ref[r, cs].astype(F32) * _silu(za_ref[r, :].astype(F32))).astype(BF16)
        mix_ref[r, wa:wa + wb] = (yb_ref[r, :].astype(F32) * _silu(zb_ref[r, :].astype(F32))).astype(BF16)
        for h in range(n_c_heads):
            sl = slice(h * HEAD_DIM, (h + 1) * HEAD_DIM)
            oc = ocf_ref[r, sl] + ocb_ref[r, sl]
            yc = oc * lax.rsqrt(jnp.mean(oc * oc, axis=-1, keepdims=True) + EPS) * on_ref[...]
            off = wa + wb + h * HEAD_DIM
            mix_ref[r, off:off + HEAD_DIM] = (yc * _silu(zc_ref[r, sl].astype(F32))).astype(BF16)
        y_ref[r, :] = _dot(mix_ref[r, :], w_ref[...])

    def finish(r):
        t = alpha * x_ref[r, :] + gate_ref[0] * y_ref[r, :]
        tc = t - jnp.mean(t, axis=-1, keepdims=True)
        scale = lax.rsqrt(jnp.mean(tc * tc, axis=-1, keepdims=True) + EPS)
        o_ref[r, :] = tc * scale * lng_ref[...] + lnb_ref[...]

    for r in blocks:
        project(r)
    for r in blocks:
        finish(r)


def _outproj_call(x2d, mod_l, ya, main, yb, ocf, ocb, o_norm, w_out, ln_g, ln_b, *, col_za, col_zb,
                  col_zc, alpha, tiles_per_batch, ctx_row, tm):
    m, d = x2d.shape
    wa, wb, wc = ya.shape[1], yb.shape[1], ocf.shape[1]
    if ctx_row is None:
        row = lambda i: i // tiles_per_batch
    else:
        row = lambda i: ctx_row
    rows = lambda w, cb=0: pl.BlockSpec((tm, w), lambda i: (i, cb))
    one = lambda w: pl.BlockSpec((1, w), lambda i: (0, 0))
    wz = wa // 2
    assert col_za % wz == 0 and col_zb % wb == 0 and col_zc % wc == 0 and tm % OUT_ROWS == 0
    return pl.pallas_call(
        functools.partial(_outproj_kernel, alpha=alpha, wa=wa, wb=wb, n_c_heads=wc // HEAD_DIM),
        grid=(m // tm,),
        in_specs=[rows(d), pl.BlockSpec((1, 1, d), lambda i: (row(i), 0, 2)),
                  rows(wa), rows(wz, col_za // wz), rows(wz, col_za // wz + 1), rows(wb), rows(wb, col_zb // wb),
                  rows(wc), rows(wc), rows(wc, col_zc // wc), one(HEAD_DIM),
                  pl.BlockSpec(w_out.shape, lambda i: (0, 0)), one(d), one(d)],
        out_specs=rows(d),
        out_shape=jax.ShapeDtypeStruct((m, d), F32),
        scratch_shapes=[pltpu.VMEM((tm, wa + wb + wc), BF16), pltpu.VMEM((tm, d), F32)],
        compiler_params=_params(("arbitrary",)),
        name="outproj",
    )(x2d, mod_l, ya, main, main, yb, main, ocf, ocb, main, o_norm.reshape(1, HEAD_DIM), w_out,
      ln_g.reshape(1, d), ln_b.reshape(1, d))


def kernel(x, c, ctx, c_ctx, w_mod, b_mod, w_in, q_norm, k_norm, rpb, conv_w, a_log, dt_bias, o_norm,
           w_out, ln_g, ln_b):
    bsz, seq, d = x.shape
    ctx_len = ctx.shape[1]
    depth = w_mod.shape[0]
    a_heads = d // (2 * HEAD_DIM)
    a_kv = a_heads // 4
    b_heads = d // (4 * HEAD_DIM)
    c_heads = d // (4 * HEAD_DIM)
    wa, wkv, wb, wc = a_heads * HEAD_DIM, a_kv * HEAD_DIM, b_heads * HEAD_DIM, c_heads * HEAD_DIM
    assert bsz < MOD_ROWS and seq % (NBR_Q_ROWS * GRID_W) == 0 and seq // GRID_W >= NBR_K_ROWS
    assert ctx_len % DN_CHUNK == 0 and 4 * c_heads <= 16
    alpha = (2 * depth) ** 0.25

    sizes = dict(qa=wa, ka=wkv, va=wkv, za=wa, qb=wb, kb=wb, vb=wb, zb=wb, qkvc=3 * wc, zc=wc)
    col, off = {}, 0
    for name, size in sizes.items():
        col[name] = off
        off += size
    n_main, n_ab = off, 4 * c_heads

    cc = jnp.concatenate([c, c_ctx[None], jnp.zeros((MOD_ROWS - bsz - 1, d), F32)], axis=0)
    mod = _mod_call(cc, w_mod, b_mod)
    tables = _rope_tables(seq)
    bias = _bias_call(rpb)

    tm_lat = 512
    tm_proj = 2048 if seq % 2048 == 0 else tm_lat
    tm_ctx = min(256, ctx_len)
    tm_proj_ctx = 1024 if (bsz * ctx_len) % 1024 == 0 else tm_ctx
    tn_proj = 512
    main_dtype = BF16
    xl = x.reshape(bsz * seq, d)
    xc = ctx.reshape(bsz * ctx_len, d)
    for l in range(depth):
        with_ctx_out = l < depth - 1
        mod_l = mod[l].reshape(MOD_ROWS, 1, 3 * d)
        w_all = w_in[l].astype(BF16)
        w_ab = jnp.pad(w_all[:, n_main:n_main + n_ab], ((0, 0), (0, HEAD_DIM - n_ab)))
        w_out_l = w_out[l].astype(BF16)
        gate_consts = _gate_consts(a_log[l], dt_bias[l])

        main_x, ab_x = _inproj_call(xc, mod_l, w_all, w_ab, n_main=n_main, tiles_per_batch=None,
                                    ctx_row=bsz, tm=tm_proj_ctx, tn=tn_proj, out_dtype=main_dtype)
        main, ab = _inproj_call(xl, mod_l, w_all, w_ab, n_main=n_main, tiles_per_batch=seq // tm_proj,
                                ctx_row=None, tm=tm_proj, tn=tn_proj, out_dtype=main_dtype)
        main3 = main.reshape(bsz, seq, n_main)
        main_x3 = main_x.reshape(bsz, ctx_len, n_main)

        qa_x, ka_x = _aprep_call(main_x, q_norm[l], k_norm[l], None, col_q=col["qa"], col_k=col["ka"],
                                 nq=a_heads, nk=a_kv, tm=tm_ctx, tiles_per_batch=ctx_len // tm_ctx)
        qa, ka = _aprep_call(main, q_norm[l], k_norm[l], tables, col_q=col["qa"], col_k=col["ka"],
                             nq=a_heads, nk=a_kv, tm=tm_lat, tiles_per_batch=seq // tm_lat)
        ka_x3 = ka_x.reshape(bsz, ctx_len, wkv)
        ya = _gqa_call(qa.reshape(bsz, seq, wa), ka_x3, main_x3, col["va"], ka.reshape(bsz, seq, wkv), main3,
                       col["va"], n_kv_heads=a_kv, group=a_heads // a_kv, tq=256, tk=512, out_dtype=BF16)

        yb = _nbr_call(main3, main_x3, bias, layer=l, col_q=col["qb"], col_k=col["kb"], col_v=col["vb"],
                       n_heads=b_heads, out_dtype=BF16)

        def delta(main2d, ab2d, t_len, tm, s0):
            qn, kn, vv = _dnprep_call(main2d, conv_w[l], col=col["qkvc"], n_heads=c_heads, tm=tm,
                                      tiles_per_batch=t_len // tm)
            r3 = lambda a: a.reshape(bsz, t_len, a.shape[-1])
            local = _gdn_local_call(r3(qn), r3(kn), r3(vv), r3(ab2d), gate_consts, n_heads=c_heads,
                                    n_chunks=4, inv_precision=INV_PRECISION)
            return _gdn_scan_call(*local, s0, n_heads=c_heads)

        s_zero = jnp.zeros((bsz, 2 * c_heads, HEAD_DIM, HEAD_DIM), F32)
        ocf_x, ocb_x, s_ctx = delta(main_x, ab_x, ctx_len, tm_ctx, s_zero)
        ocf, ocb, _ = delta(main, ab, seq, tm_lat, s_ctx)

        x_new = _outproj_call(xl, mod_l, ya.reshape(bsz * seq, wa), main, yb.reshape(bsz * seq, wb),
                              ocf.reshape(bsz * seq, wc), ocb.reshape(bsz * seq, wc), o_norm[l], w_out_l,
                              ln_g[l], ln_b[l], col_za=col["za"], col_zb=col["zb"], col_zc=col["zc"],
                              alpha=alpha, tiles_per_batch=seq // tm_lat, ctx_row=None, tm=tm_lat)

        if with_ctx_out:
            ya_x = _flash_call(qa_x.reshape(bsz, ctx_len, wa), [(ka_x3, 0, main_x3, col["va"])],
                               n_kv_heads=a_kv, group=a_heads // a_kv, q_col=0, tq=ctx_len, tk=ctx_len,
                               q_scale=None, out_dtype=F32)
            yb_x = _flash_call(main_x3, [(main_x3, col["kb"], main_x3, col["vb"])],
                               n_kv_heads=b_heads, group=1, q_col=col["qb"], tq=ctx_len, tk=ctx_len,
                               q_scale=SCORE_SCALE, out_dtype=F32)
            xc = _outproj_call(xc, mod_l, ya_x.reshape(bsz * ctx_len, wa), main_x,
                               yb_x.reshape(bsz * ctx_len, wb), ocf_x.reshape(bsz * ctx_len, wc),
                               ocb_x.reshape(bsz * ctx_len, wc), o_norm[l], w_out_l, ln_g[l], ln_b[l],
                               col_za=col["za"], col_zb=col["zb"], col_zc=col["zc"], alpha=alpha,
                               tiles_per_batch=None, ctx_row=bsz, tm=tm_ctx)
        xl = x_new
    return xl.reshape(bsz, seq, d)
```

```python
import functools

import jax
import jax.numpy as jnp
from jax import lax
from jax.experimental import pallas as pl
from jax.experimental.pallas import tpu as pltpu

F32 = jnp.float32
BF16 = jnp.bfloat16

HEAD_DIM = 128
GRID_W = 64
ROPE_THETA = 10000.0
NA_WIN_H = 8
NA_WIN_W = 16
CONV_K = 3
DN_CHUNK = 64
EPS = 1e-6
NEG = -1e30
MOD_ROWS = 8
NBR_Q_ROWS = 8
NBR_K_ROWS = 16
NBR_HALF_Q = 4
NBR_HALF_K = 12
VMEM_LIMIT = 56 * 1024 * 1024
INPROJ_VMEM_LIMIT = 60 * 1024 * 1024
LOG2E = 1.4426950408889634
SCORE_SCALE = HEAD_DIM ** -0.5 * LOG2E

HIGHEST = lax.Precision.HIGHEST
INV_PRECISION = None


def _silu(z):
    return z * jax.nn.sigmoid(z)


def _softplus(z):
    return jnp.maximum(z, 0.0) + jnp.log(1.0 + jnp.exp(-jnp.abs(z)))


def _dot(a, b, precision=None):
    return jnp.dot(a, b, preferred_element_type=F32, precision=precision)


def _dot_nt(a, b, precision=None):
    return lax.dot_general(a, b, (((1,), (1,)), ((), ())), preferred_element_type=F32,
                           precision=precision)


def _params(sem, vmem_limit=VMEM_LIMIT):
    return pltpu.CompilerParams(dimension_semantics=sem, vmem_limit_bytes=vmem_limit)


def _mod_kernel(c_ref, w_ref, b_ref, o_ref):
    s = _silu(c_ref[...]).astype(BF16)
    o_ref[0] = _dot(s, w_ref[0].astype(BF16)) + b_ref[0]


def _mod_call(cc, w_mod, b_mod):
    depth, d, n = w_mod.shape
    tn = 768
    return pl.pallas_call(
        _mod_kernel,
        grid=(depth, n // tn),
        in_specs=[pl.BlockSpec((MOD_ROWS, d), lambda l, j: (0, 0)),
                  pl.BlockSpec((1, d, tn), lambda l, j: (l, 0, j)),
                  pl.BlockSpec((1, 1, tn), lambda l, j: (l, 0, j))],
        out_specs=pl.BlockSpec((1, MOD_ROWS, tn), lambda l, j: (l, 0, j)),
        out_shape=jax.ShapeDtypeStruct((depth, MOD_ROWS, n), F32),
        compiler_params=_params(("arbitrary", "arbitrary")),
        name="mod",
    )(cc, w_mod, b_mod.reshape(depth, 1, n))


LN_ROWS = 256


def _inproj_kernel(x_ref, shift_ref, scale_ref, w_ref, wab_ref, o_ref, oab_ref, h_ref):
    @pl.when(pl.program_id(1) == 0)
    def _():
        for r0 in range(0, x_ref.shape[0], LN_ROWS):
            r = slice(r0, r0 + LN_ROWS)
            x = x_ref[r, :]
            xc = x - jnp.mean(x, axis=-1, keepdims=True)
            y = xc * lax.rsqrt(jnp.mean(xc * xc, axis=-1, keepdims=True) + EPS)
            h_ref[r, :] = (y * (1.0 + scale_ref[0]) + shift_ref[0]).astype(BF16)
            oab_ref[r, :] = _dot(h_ref[r, :], wab_ref[...])

    o_ref[...] = _dot(h_ref[...], w_ref[...]).astype(o_ref.dtype)


def _inproj_call(x2d, mod_l, w_all, w_ab, *, n_main, tiles_per_batch, ctx_row, tm, tn, out_dtype):
    m, d = x2d.shape
    n = n_main
    assert n % tn == 0 and tm % LN_ROWS == 0
    if ctx_row is None:
        row = lambda i: i // tiles_per_batch
    else:
        row = lambda i: ctx_row
    return pl.pallas_call(
        _inproj_kernel,
        grid=(m // tm, n // tn),
        in_specs=[pl.BlockSpec((tm, d), lambda i, j: (i, 0)),
                  pl.BlockSpec((1, 1, d), lambda i, j: (row(i), 0, 0)),
                  pl.BlockSpec((1, 1, d), lambda i, j: (row(i), 0, 1)),
                  pl.BlockSpec((d, tn), lambda i, j: (0, j)),
                  pl.BlockSpec((d, HEAD_DIM), lambda i, j: (0, 0))],
        out_specs=[pl.BlockSpec((tm, tn), lambda i, j: (i, j)),
                   pl.BlockSpec((tm, HEAD_DIM), lambda i, j: (i, 0))],
        out_shape=[jax.ShapeDtypeStruct((m, n), out_dtype),
                   jax.ShapeDtypeStruct((m, HEAD_DIM), F32)],
        scratch_shapes=[pltpu.VMEM((tm, d), BF16)],
        compiler_params=_params(("arbitrary", "arbitrary"), INPROJ_VMEM_LIMIT),
        name="inproj",
    )(x2d, mod_l, mod_l, w_all, w_ab)


def _swap_quarters(x):
    lane = lax.broadcasted_iota(jnp.int32, x.shape, 1)
    fwd = pltpu.roll(x, 3 * HEAD_DIM // 4, axis=1)
    bwd = pltpu.roll(x, HEAD_DIM // 4, axis=1)
    return jnp.where((lane % (HEAD_DIM // 2)) < HEAD_DIM // 4, fwd, bwd)


def _aprep_kernel(*refs, nq, nk, rope):
    if rope:
        q_ref, k_ref, qw_ref, kw_ref, cos_ref, sin_ref, qo_ref, ko_ref = refs
        cos, sin = cos_ref[...], sin_ref[...]
    else:
        q_ref, k_ref, qw_ref, kw_ref, qo_ref, ko_ref = refs

    def norm(x, w):
        y = x * lax.rsqrt(jnp.mean(x * x, axis=-1, keepdims=True) + EPS) * w
        if rope:
            y = y * cos + _swap_quarters(y) * sin
        return y

    for h in range(nq):
        sl = slice(h * HEAD_DIM, (h + 1) * HEAD_DIM)
        y = norm(q_ref[:, sl].astype(F32), qw_ref[...])
        qo_ref[:, sl] = (y * SCORE_SCALE).astype(qo_ref.dtype)
    for h in range(nk):
        sl = slice(h * HEAD_DIM, (h + 1) * HEAD_DIM)
        ko_ref[:, sl] = norm(k_ref[:, sl].astype(F32), kw_ref[...]).astype(ko_ref.dtype)


def _aprep_call(main, q_norm, k_norm, tables, *, col_q, col_k, nq, nk, tm, tiles_per_batch):
    m = main.shape[0]
    wq, wk = nq * HEAD_DIM, nk * HEAD_DIM
    in_specs = [pl.BlockSpec((tm, wq), lambda i: (i, col_q // wq)),
                pl.BlockSpec((tm, wk), lambda i: (i, col_k // wk)),
                pl.BlockSpec((1, HEAD_DIM), lambda i: (0, 0)),
                pl.BlockSpec((1, HEAD_DIM), lambda i: (0, 0))]
    args = [main, main, q_norm.reshape(1, HEAD_DIM), k_norm.reshape(1, HEAD_DIM)]
    if tables is not None:
        in_specs += [pl.BlockSpec((tm, HEAD_DIM), lambda i: (i % tiles_per_batch, 0))] * 2
        args += list(tables)
    return pl.pallas_call(
        functools.partial(_aprep_kernel, nq=nq, nk=nk, rope=tables is not None),
        grid=(m // tm,),
        in_specs=in_specs,
        out_specs=[pl.BlockSpec((tm, wq), lambda i: (i, 0)),
                   pl.BlockSpec((tm, wk), lambda i: (i, 0))],
        out_shape=[jax.ShapeDtypeStruct((m, wq), BF16), jax.ShapeDtypeStruct((m, wk), BF16)],
        compiler_params=_params(("arbitrary",)),
        name="aprep",
    )(*args)


def _rope_tables(n_tokens):
    t = jnp.arange(n_tokens, dtype=jnp.int32)
    row = (t // GRID_W).astype(F32)
    col = (t % GRID_W).astype(F32)
    half = HEAD_DIM // 2
    inv_freq = ROPE_THETA ** (-jnp.arange(0, half, 2, dtype=F32) / half)
    ar, ac = row[:, None] * inv_freq, col[:, None] * inv_freq
    cos = jnp.concatenate([jnp.cos(ar), jnp.cos(ar), jnp.cos(ac), jnp.cos(ac)], -1)
    sin = jnp.concatenate([-jnp.sin(ar), jnp.sin(ar), -jnp.sin(ac), jnp.sin(ac)], -1)
    return cos, sin


def _flash_kernel(*refs, group, tq, sources, q_scale):
    q_ref = refs[0]
    o_ref = refs[1 + 2 * len(sources)]
    q = jnp.concatenate([q_ref[0, :, g * HEAD_DIM:(g + 1) * HEAD_DIM] for g in range(group)], axis=0)
    if q_scale is not None:
        q = q.astype(F32) * q_scale
    q = q.astype(BF16)
    rows = group * tq

    def step(carry, k, v):
        m, l, acc = carry
        s = _dot_nt(q, k.astype(BF16))
        m_new = jnp.maximum(m, jnp.max(s, axis=-1, keepdims=True))
        alpha = jnp.exp2(m - m_new)
        p = jnp.exp2(s - m_new)
        l = alpha * l + jnp.sum(p, axis=-1, keepdims=True)
        acc = alpha * acc + _dot(p.astype(BF16), v.astype(BF16))
        return m_new, l, acc

    carry = (jnp.full((rows, 1), -jnp.inf, F32), jnp.zeros((rows, 1), F32),
             jnp.zeros((rows, HEAD_DIM), F32))
    for si, (length, tk) in enumerate(sources):
        k_ref, v_ref = refs[1 + 2 * si], refs[2 + 2 * si]
        if length == tk:
            carry = step(carry, k_ref[0], v_ref[0])
        else:
            def body(i, c, k_ref=k_ref, v_ref=v_ref, tk=tk):
                st = pl.multiple_of(i * tk, tk)
                return step(c, k_ref[0, pl.ds(st, tk), :], v_ref[0, pl.ds(st, tk), :])
            carry = lax.fori_loop(0, length // tk, body, carry)
    _, l, acc = carry
    o = acc / l
    for g in range(group):
        o_ref[0, :, g * HEAD_DIM:(g + 1) * HEAD_DIM] = o[g * tq:(g + 1) * tq].astype(o_ref.dtype)


SOFTMAX_UNIT = 16


def _gqa_kernel(q_ref, kc_ref, vc_ref, kl_ref, vl_ref, o_ref, q4_ref, s0_ref, s1_ref, sc_ref, p_ref,
                m_ref, l_ref, a_ref, acc_ref, *, group, tq, tk):
    rows = group * tq
    half = rows // 2
    n_lat = kl_ref.shape[1] // tk
    for g in range(group):
        q4_ref[g * tq:(g + 1) * tq, :] = q_ref[0, :, g * HEAD_DIM:(g + 1) * HEAD_DIM]
    m_ref[...] = jnp.full(m_ref.shape, -jnp.inf, F32)
    l_ref[...] = jnp.zeros(l_ref.shape, F32)
    acc_ref[...] = jnp.zeros(acc_ref.shape, F32)
    halves = (slice(0, half), slice(half, rows))

    def issue(dst_ref, k, rs):
        dst_ref[rs, :] = _dot_nt(q4_ref[rs, :], k)

    def softmax(src_ref, rs, width):
        for r0 in range(rs.start, rs.stop, SOFTMAX_UNIT):
            r = slice(r0, r0 + SOFTMAX_UNIT)
            cols = [slice(j, j + HEAD_DIM) for j in range(0, width, HEAD_DIM)]
            s = [src_ref[r, c] for c in cols]
            mx = functools.reduce(jnp.maximum, s)
            m_old = m_ref[r, :]
            m_new = jnp.maximum(m_old, jnp.max(mx, axis=-1, keepdims=True))
            alpha = jnp.exp2(m_old - m_new)
            p = [jnp.exp2(x - m_new) for x in s]
            for c, x in zip(cols, p):
                p_ref[r, c] = x.astype(BF16)
            m_ref[r, :] = m_new
            a_ref[r, :] = alpha
            l_ref[r, :] = alpha * l_ref[r, :] + functools.reduce(jnp.add, p)

    def accumulate(rs, width, v):
        pv = _dot(p_ref[rs, 0:width], v)
        acc_ref[rs, :] = acc_ref[rs, :] * a_ref[rs, :] + pv

    def chunk_step(cur_ref, width, v, nxt_ref, k_next):
        for rs in halves:
            if nxt_ref is not None:
                issue(nxt_ref, k_next, rs)
            softmax(cur_ref, rs, width)
            accumulate(rs, width, v)

    lat = lambda ref, i: ref[0, pl.ds(pl.multiple_of(i * tk, tk), tk), :]
    for rs in halves:
        issue(s0_ref, lat(kl_ref, 0), rs)

    def pair(j, carry):
        i = 2 * j
        chunk_step(s0_ref, tk, lat(vl_ref, i), s1_ref, lat(kl_ref, i + 1))
        chunk_step(s1_ref, tk, lat(vl_ref, i + 1), s0_ref, lat(kl_ref, i + 2))
        return carry

    lax.fori_loop(0, n_lat // 2 - 1, pair, 0)
    chunk_step(s0_ref, tk, lat(vl_ref, n_lat - 2), s1_ref, lat(kl_ref, n_lat - 1))
    chunk_step(s1_ref, tk, lat(vl_ref, n_lat - 1), sc_ref, kc_ref[0])
    chunk_step(sc_ref, kc_ref.shape[1], vc_ref[0], None, None)

    o = acc_ref[...] / jnp.sum(l_ref[...], axis=-1, keepdims=True)
    for g in range(group):
        o_ref[0, :, g * HEAD_DIM:(g + 1) * HEAD_DIM] = o[g * tq:(g + 1) * tq].astype(o_ref.dtype)


def _gqa_call(q, k_ctx, v_ctx, v_ctx_col, k_lat, v_lat, v_lat_col, *, n_kv_heads, group, tq, tk, out_dtype):
    b, s, _ = q.shape
    ctx_len = k_ctx.shape[1]
    wq = group * HEAD_DIM
    rows = group * tq
    assert s % (2 * tk) == 0 and s // tk >= 2 and ctx_len % HEAD_DIM == 0 and ctx_len <= tk
    assert rows % (2 * SOFTMAX_UNIT) == 0
    cvc, cvl = v_ctx_col // HEAD_DIM, v_lat_col // HEAD_DIM
    kv_spec = lambda t, c: pl.BlockSpec((1, t, HEAD_DIM), lambda bi, g, qi: (bi, 0, c + g))
    stat = pltpu.VMEM((rows, HEAD_DIM), F32)
    return pl.pallas_call(
        functools.partial(_gqa_kernel, group=group, tq=tq, tk=tk),
        grid=(b, n_kv_heads, s // tq),
        in_specs=[pl.BlockSpec((1, tq, wq), lambda bi, g, qi: (bi, qi, g)),
                  kv_spec(ctx_len, 0), kv_spec(ctx_len, cvc), kv_spec(s, 0), kv_spec(s, cvl)],
        out_specs=pl.BlockSpec((1, tq, wq), lambda bi, g, qi: (bi, qi, g)),
        out_shape=jax.ShapeDtypeStruct((b, s, n_kv_heads * wq), out_dtype),
        scratch_shapes=[pltpu.VMEM((rows, HEAD_DIM), BF16), pltpu.VMEM((rows, tk), F32),
                        pltpu.VMEM((rows, tk), F32), pltpu.VMEM((rows, ctx_len), F32),
                        pltpu.VMEM((rows, tk), BF16), stat, stat, stat, stat],
        compiler_params=_params(("arbitrary", "arbitrary", "arbitrary")),
        name="gqa",
    )(q, k_ctx, v_ctx, k_lat, v_lat)


def _flash_call(q, kv, *, n_kv_heads, group, q_col, tq, tk, q_scale, out_dtype):
    b, t_q, _ = q.shape
    wq = group * HEAD_DIM
    in_specs = [pl.BlockSpec((1, tq, wq), lambda bi, g, qi: (bi, qi, q_col // wq + g))]
    args = [q]
    sources = []
    for k_arr, k_col, v_arr, v_col in kv:
        t_k = k_arr.shape[1]
        in_specs.append(pl.BlockSpec((1, t_k, HEAD_DIM),
                                     lambda bi, g, qi, c=k_col // HEAD_DIM: (bi, 0, c + g)))
        in_specs.append(pl.BlockSpec((1, t_k, HEAD_DIM),
                                     lambda bi, g, qi, c=v_col // HEAD_DIM: (bi, 0, c + g)))
        args += [k_arr, v_arr]
        sources.append((t_k, min(tk, t_k)))
    return pl.pallas_call(
        functools.partial(_flash_kernel, group=group, tq=tq, sources=tuple(sources), q_scale=q_scale),
        grid=(b, n_kv_heads, t_q // tq),
        in_specs=in_specs,
        out_specs=pl.BlockSpec((1, tq, wq), lambda bi, g, qi: (bi, qi, g)),
        out_shape=jax.ShapeDtypeStruct((b, t_q, n_kv_heads * wq), out_dtype),
        compiler_params=_params(("arbitrary", "arbitrary", "arbitrary")),
        name="flash",
    )(*args)


def _nbr_window(cls, qr, kr):
    if cls == 0:
        lo, dr = max(qr - NA_WIN_H // 2, 0), kr - qr
    elif cls == 1:
        lo, dr = qr, kr - NA_WIN_H // 2 - qr
    else:
        lo, dr = min(NA_WIN_H // 2 + qr, NBR_K_ROWS - NA_WIN_H), kr - NBR_Q_ROWS - qr
    if lo <= kr < lo + NA_WIN_H:
        assert -NA_WIN_H < dr < NA_WIN_H
        return dr + NA_WIN_H - 1
    return None


def _nbr_half_offset(cls, hf):
    return (0, (0, NBR_HALF_Q)[hf], NBR_HALF_Q)[cls]


def _bias_kernel(rpb_ref, o_ref, *, n_heads):
    nrow, ncol = 2 * NA_WIN_H - 1, 2 * NA_WIN_W - 1
    base = (pl.program_id(0) * n_heads + pl.program_id(1)) * (nrow * ncol)
    shape = (GRID_W, 2 * GRID_W)
    c = lax.broadcasted_iota(jnp.int32, shape, 0)
    lane = lax.broadcasted_iota(jnp.int32, shape, 1)
    right = lane >= GRID_W
    kc = jnp.where(right, lane - GRID_W, lane)
    c0 = jnp.clip(c - NA_WIN_W // 2, 0, GRID_W - NA_WIN_W)
    col_ok = (kc >= c0) & (kc < c0 + NA_WIN_W)
    idx = kc - c + (NA_WIN_W - 1)
    cache = {}

    def tile(a_left, a_right):
        key = (a_left, a_right)
        if key not in cache:
            t = jnp.full(shape, NEG, F32)
            if key != (None, None):
                for j in range(ncol):
                    sl = NEG if a_left is None else rpb_ref[base + a_left * ncol + j]
                    sr = NEG if a_right is None else rpb_ref[base + a_right * ncol + j]
                    t = jnp.where(idx == j, jnp.where(right, sr, sl), t)
                t = jnp.where(col_ok, t * LOG2E, NEG)
            cache[key] = t
        return cache[key]

    for cls in range(3):
        for hf in range(2):
            off = _nbr_half_offset(cls, hf)
            assert all(off <= kr < off + NBR_HALF_K for q in range(hf * NBR_HALF_Q, (hf + 1) * NBR_HALF_Q)
                       for kr in range(NBR_K_ROWS) if _nbr_window(cls, q, kr) is not None)
            for qr in range(NBR_HALF_Q):
                for p in range(NBR_HALF_K // 2):
                    kr = _nbr_half_offset(cls, hf) + 2 * p
                    t = tile(_nbr_window(cls, hf * NBR_HALF_Q + qr, kr),
                             _nbr_window(cls, hf * NBR_HALF_Q + qr, kr + 1))
                    o_ref[0, 0, cls, hf, qr * GRID_W:(qr + 1) * GRID_W, p * 2 * GRID_W:(p + 1) * 2 * GRID_W] = t


def _bias_call(rpb):
    depth, n_heads = rpb.shape[:2]
    shape = (depth, n_heads, 3, 2, NBR_HALF_Q * GRID_W, NBR_HALF_K * GRID_W)
    return pl.pallas_call(
        functools.partial(_bias_kernel, n_heads=n_heads),
        grid=(depth, n_heads),
        in_specs=[pl.BlockSpec(memory_space=pltpu.SMEM)],
        out_specs=pl.BlockSpec((1, 1) + shape[2:], lambda l, h: (l, h, 0, 0, 0, 0)),
        out_shape=jax.ShapeDtypeStruct(shape, F32),
        compiler_params=_params(("arbitrary", "arbitrary")),
        name="nbr_bias",
    )(rpb.reshape(-1))


def _nbr_kernel(q_ref, k_ref, v_ref, kc_ref, vc_ref, bias_ref, o_ref, *, rows):
    g = pl.program_id(2)
    n_steps = rows // NBR_Q_ROWS
    start_row = jnp.clip(g * NBR_Q_ROWS - NA_WIN_H // 2, 0, rows - NBR_K_ROWS)
    start = pl.multiple_of(start_row * GRID_W, (NA_WIN_H // 2) * GRID_W)
    cls = jnp.where(g == 0, 0, jnp.where(g == n_steps - 1, 2, 1))
    nq, nk = NBR_HALF_Q * GRID_W, NBR_HALF_K * GRID_W
    kc, vc = kc_ref[0].astype(BF16), vc_ref[0].astype(BF16)
    scores = []
    for hf in range(2):
        off = jnp.where(cls == 0, 0, jnp.where(cls == 1, hf * NBR_HALF_Q, NBR_HALF_Q))
        st = pl.multiple_of(start + off * GRID_W, NBR_HALF_Q * GRID_W)
        q = (q_ref[0, hf * nq:(hf + 1) * nq, :].astype(F32) * SCORE_SCALE).astype(BF16)
        s_win = _dot_nt(q, k_ref[0, pl.ds(st, nk), :].astype(BF16)) + bias_ref[0, 0, cls, hf]
        scores.append((st, s_win, _dot_nt(q, kc)))
    for hf, (st, s_win, s_ctx) in enumerate(scores):
        m = jnp.maximum(jnp.max(s_win, axis=-1, keepdims=True), jnp.max(s_ctx, axis=-1, keepdims=True))
        p_win = jnp.exp2(s_win - m)
        p_ctx = jnp.exp2(s_ctx - m)
        l = jnp.sum(p_win, axis=-1, keepdims=True) + jnp.sum(p_ctx, axis=-1, keepdims=True)
        o = _dot(p_win.astype(BF16), v_ref[0, pl.ds(st, nk), :].astype(BF16)) + _dot(p_ctx.astype(BF16), vc)
        o_ref[0, hf * nq:(hf + 1) * nq, :] = (o / l).astype(o_ref.dtype)


def _nbr_call(main, main_ctx, bias, *, layer, col_q, col_k, col_v, n_heads, out_dtype):
    b, s, _ = main.shape
    ctx_len = main_ctx.shape[1]
    rows = s // GRID_W
    tq = NBR_Q_ROWS * GRID_W
    cq, ck, cv = col_q // HEAD_DIM, col_k // HEAD_DIM, col_v // HEAD_DIM
    return pl.pallas_call(
        functools.partial(_nbr_kernel, rows=rows),
        grid=(n_heads, b, rows // NBR_Q_ROWS),
        in_specs=[pl.BlockSpec((1, tq, HEAD_DIM), lambda h, bi, g: (bi, g, cq + h)),
                  pl.BlockSpec((1, s, HEAD_DIM), lambda h, bi, g: (bi, 0, ck + h)),
                  pl.BlockSpec((1, s, HEAD_DIM), lambda h, bi, g: (bi, 0, cv + h)),
                  pl.BlockSpec((1, ctx_len, HEAD_DIM), lambda h, bi, g: (bi, 0, ck + h)),
                  pl.BlockSpec((1, ctx_len, HEAD_DIM), lambda h, bi, g: (bi, 0, cv + h)),
                  pl.BlockSpec((1, 1) + bias.shape[2:], lambda h, bi, g: (layer, h, 0, 0, 0, 0))],
        out_specs=pl.BlockSpec((1, tq, HEAD_DIM), lambda h, bi, g: (bi, g, h)),
        out_shape=jax.ShapeDtypeStruct((b, s, n_heads * HEAD_DIM), out_dtype),
        compiler_params=_params(("arbitrary", "arbitrary", "arbitrary")),
        name="nbr_attn",
    )(main, main, main, main_ctx, main_ctx, bias)


def _dnprep_kernel(x_ref, prev_ref, next_ref, w_ref, q_ref, k_ref, v_ref, *, tiles_per_batch, n_heads):
    i = pl.program_id(0) % tiles_per_batch
    x = x_ref[...].astype(F32)
    tm = x.shape[0]
    row = lax.broadcasted_iota(jnp.int32, x.shape, 0)
    halo = prev_ref.shape[0]
    prev_row = jnp.where(i == 0, 0.0, prev_ref[halo - 1:halo, :].astype(F32))
    next_row = jnp.where(i == tiles_per_batch - 1, 0.0, next_ref[0:1, :].astype(F32))
    x_prev = jnp.where(row == 0, prev_row, pltpu.roll(x, 1, axis=0))
    x_next = jnp.where(row == tm - 1, next_row, pltpu.roll(x, tm - 1, axis=0))
    y = _silu(w_ref[0:1, :] * x_prev + w_ref[1:2, :] * x + w_ref[2:3, :] * x_next)
    w = n_heads * HEAD_DIM
    for h in range(n_heads):
        sl = slice(h * HEAD_DIM, (h + 1) * HEAD_DIM)
        qh = y[:, h * HEAD_DIM:(h + 1) * HEAD_DIM]
        kh = y[:, w + h * HEAD_DIM:w + (h + 1) * HEAD_DIM]
        qn = qh * lax.rsqrt(jnp.sum(qh * qh, axis=-1, keepdims=True) + EPS) * HEAD_DIM ** -0.5
        q_ref[:, sl] = qn.astype(q_ref.dtype)
        k_ref[:, sl] = (kh * lax.rsqrt(jnp.sum(kh * kh, axis=-1, keepdims=True) + EPS)).astype(k_ref.dtype)
    v_ref[...] = y[:, 2 * w:].astype(v_ref.dtype)


def _dnprep_call(main, conv_w, *, col, n_heads, tm, tiles_per_batch):
    m = main.shape[0]
    w3 = 3 * n_heads * HEAD_DIM
    halo = 16
    hb = tm // halo
    n_halo = m // halo
    out = jax.ShapeDtypeStruct((m, n_heads * HEAD_DIM), BF16)
    o_spec = pl.BlockSpec((tm, n_heads * HEAD_DIM), lambda i: (i, 0))
    return pl.pallas_call(
        functools.partial(_dnprep_kernel, tiles_per_batch=tiles_per_batch, n_heads=n_heads),
        grid=(m // tm,),
        in_specs=[pl.BlockSpec((tm, w3), lambda i: (i, col // w3)),
                  pl.BlockSpec((halo, w3), lambda i: (jnp.maximum(i * hb - 1, 0), col // w3)),
                  pl.BlockSpec((halo, w3), lambda i: (jnp.minimum((i + 1) * hb, n_halo - 1), col // w3)),
                  pl.BlockSpec((CONV_K, w3), lambda i: (0, 0))],
        out_specs=[o_spec, o_spec, o_spec],
        out_shape=[out, out, out],
        compiler_params=_params(("arbitrary",)),
        name="dn_prep",
    )(main, main, main, conv_w)


def _tri_masks(n, upper):
    i = lax.broadcasted_iota(jnp.int32, (n, n), 0)
    j = lax.broadcasted_iota(jnp.int32, (n, n), 1)
    incl = (j >= i) if upper else (j <= i)
    strict = (j > i) if upper else (j < i)
    eye = (i == j).astype(F32)
    base = 8
    same_base = (i // base) == (j // base)
    levels = []
    s = base
    while s < n:
        levels.append(((i // (2 * s)) == (j // (2 * s))) & ((i // s) != (j // s)))
        s *= 2
    return incl, strict, eye, same_base, levels


def _unit_tri_inverse(lms, masks, precision):
    mm = lambda a, b: _dot(a, b, precision) if precision is not None else _dot(a.astype(BF16), b.astype(BF16))
    eyes = [m[2] for m in masks]
    l1 = [jnp.where(m[3], lm, 0.0) for lm, m in zip(lms, masks)]
    l2 = [mm(a, a) for a in l1]
    l4 = [mm(a, a) for a in l2]
    t = [mm(e - a, e + b) for e, a, b in zip(eyes, l1, l2)]
    t = [mm(a, e + b) for e, a, b in zip(eyes, t, l4)]
    for lvl in range(len(masks[0][4])):
        cs = [jnp.where(m[4][lvl], lm, 0.0) for lm, m in zip(lms, masks)]
        tc = [mm(a, b) for a, b in zip(t, cs)]
        t = [a - mm(b, a) for a, b in zip(t, tc)]
    return t


def _gdn_local_kernel(q_ref, k_ref, v_ref, ab_ref, alr_ref, dtr_ref,
                      wq_ref, u_ref, kdt_ref, qk_ref, eg_ref, *, n_heads, n_chunks, inv_precision):
    c = DN_CHUNK
    masks = [_tri_masks(c, upper) for upper in (False, True)]
    gates = {}
    for d in range(2):
        cum = masks[d][0].astype(F32)
        for ci in range(n_chunks):
            ab = ab_ref[0, ci * c:(ci + 1) * c, :]
            g_tok = -jnp.exp(alr_ref[...]) * _softplus(ab + dtr_ref[...])
            gc_tok = _dot(cum, g_tok, HIGHEST)
            gates[ci, d] = (jax.nn.sigmoid(ab), gc_tok, gc_tok.T)

    streams = [(ci, d, h) for ci in range(n_chunks) for d in range(2) for h in range(n_heads)]
    pre = []
    for ci, d, h in streams:
        incl, strict = masks[d][0], masks[d][1]
        rows = slice(ci * c, (ci + 1) * c)
        sl = slice(h * HEAD_DIM, (h + 1) * HEAD_DIM)
        cb, cg = d * 2 * n_heads + h, d * 2 * n_heads + n_heads + h
        beta_tok, gc_tok, gc_rows = gates[ci, d]
        q, k, v = q_ref[0, rows, sl], k_ref[0, rows, sl], v_ref[0, rows, sl]
        beta_c = beta_tok[:, cb:cb + 1]
        gc_c = gc_tok[:, cg:cg + 1]
        last = 0 if d == 1 else c - 1
        g_last = gc_c[last:last + 1, :]
        decay = jnp.where(incl, jnp.exp(jnp.where(incl, gc_c - gc_rows[cg:cg + 1, :], 0.0)), 0.0)
        kbeta = k * beta_c
        e_c = jnp.exp(gc_c)
        wq_ref[0, d, ci, c:2 * c, sl] = (q * e_c).astype(wq_ref.dtype)
        kdt_ref[0, d, ci, h] = (k * jnp.exp(g_last - gc_c)).T.astype(kdt_ref.dtype)
        eg_ref[0, d, ci, h:h + 1, :] = jnp.broadcast_to(jnp.exp(g_last), (1, HEAD_DIM))
        lhs = jnp.concatenate([kbeta, q], axis=0).astype(BF16)
        rhs = jnp.concatenate([v * beta_c, kbeta * e_c], axis=1).astype(BF16)
        pre.append((lhs, k.astype(BF16), rhs, decay))
    kq = [_dot_nt(lhs, kb) for lhs, kb, _, _ in pre]
    lms = []
    for (ci, d, h), x, (_, _, _, decay) in zip(streams, kq, pre):
        qk_ref[0, d, ci, h] = (x[c:] * decay).astype(qk_ref.dtype)
        lms.append(jnp.where(masks[d][1], x[:c] * decay, 0.0))
    ts = _unit_tri_inverse(lms, [masks[d] for _, d, _ in streams], inv_precision)
    uws = [_dot(t.astype(BF16), rhs) for t, (_, _, rhs, _) in zip(ts, pre)]
    for (ci, d, h), uw in zip(streams, uws):
        sl = slice(h * HEAD_DIM, (h + 1) * HEAD_DIM)
        u_ref[0, d, ci * c:(ci + 1) * c, sl] = uw[:, :HEAD_DIM]
        wq_ref[0, d, ci, 0:c, sl] = uw[:, HEAD_DIM:].astype(wq_ref.dtype)


def _gdn_local_call(qn, kn, vv, ab, gate_consts, *, n_heads, n_chunks, inv_precision):
    b, t, w = qn.shape
    c = DN_CHUNK
    n = t // c
    tm = n_chunks * c
    tok = pl.BlockSpec((1, tm, w), lambda bi, i: (bi, i, 0))
    const = lambda a: pl.BlockSpec(a.shape, lambda bi, i: (0,) * a.ndim)
    return pl.pallas_call(
        functools.partial(_gdn_local_kernel, n_heads=n_heads, n_chunks=n_chunks, inv_precision=inv_precision),
        grid=(b, n // n_chunks),
        in_specs=[tok, tok, tok, pl.BlockSpec((1, tm, HEAD_DIM), lambda bi, i: (bi, i, 0))]
                 + [const(a) for a in gate_consts],
        out_specs=[pl.BlockSpec((1, 2, n_chunks, 2 * c, w), lambda bi, i: (bi, 0, i, 0, 0)),
                   pl.BlockSpec((1, 2, tm, w), lambda bi, i: (bi, 0, i, 0)),
                   pl.BlockSpec((1, 2, n_chunks, n_heads, HEAD_DIM, c), lambda bi, i: (bi, 0, i, 0, 0, 0)),
                   pl.BlockSpec((1, 2, n_chunks, n_heads, c, c), lambda bi, i: (bi, 0, i, 0, 0, 0)),
                   pl.BlockSpec((1, 2, n_chunks, n_heads, HEAD_DIM), lambda bi, i: (bi, 0, i, 0, 0))],
        out_shape=[jax.ShapeDtypeStruct((b, 2, n, 2 * c, w), BF16),
                   jax.ShapeDtypeStruct((b, 2, t, w), F32),
                   jax.ShapeDtypeStruct((b, 2, n, n_heads, HEAD_DIM, c), BF16),
                   jax.ShapeDtypeStruct((b, 2, n, n_heads, c, c), BF16),
                   jax.ShapeDtypeStruct((b, 2, n, n_heads, HEAD_DIM), F32)],
        compiler_params=_params(("arbitrary", "arbitrary")),
        name="gdn_local",
    )(qn, kn, vv, ab, *gate_consts)


def _gdn_scan_kernel(wqf_ref, wqb_ref, uf_ref, ub_ref, kdtf_ref, kdtb_ref, qkf_ref, qkb_ref, egf_ref,
                     egb_ref, s0_ref, of_ref, ob_ref, sfin_ref, s_ref, *, n_heads, bsz, n_sub):
    n = pl.program_id(0)
    c = DN_CHUNK

    @pl.when(n == 0)
    def _():
        s_ref[...] = s0_ref[...]

    refs = ((wqf_ref, uf_ref, kdtf_ref, qkf_ref, egf_ref, of_ref),
            (wqb_ref, ub_ref, kdtb_ref, qkb_ref, egb_ref, ob_ref))
    streams = [(b, d, h) for b in range(bsz) for d in range(2) for h in range(n_heads)]
    cols = lambda h: slice(h * HEAD_DIM, (h + 1) * HEAD_DIM)
    for sub in range(n_sub):
        ck = lambda d: n_sub - 1 - sub if d == 1 else sub
        tok = lambda d: slice(ck(d) * c, (ck(d) + 1) * c)
        ws = [_dot(refs[d][0][b, 0, ck(d), :, cols(h)], s_ref[b, d * n_heads + h].astype(BF16))
              for b, d, h in streams]
        v_new = [(refs[d][1][b, 0, tok(d), cols(h)] - x[:c]).astype(BF16) for (b, d, h), x in zip(streams, ws)]
        intra = [_dot(refs[d][3][b, 0, ck(d), h], v) for (b, d, h), v in zip(streams, v_new)]
        upd = [_dot(refs[d][2][b, 0, ck(d), h], v) for (b, d, h), v in zip(streams, v_new)]
        for (b, d, h), x, y, z in zip(streams, ws, intra, upd):
            refs[d][5][b, tok(d), cols(h)] = x[c:] + y
            i = d * n_heads + h
            s_ref[b, i] = s_ref[b, i] * refs[d][4][b, 0, ck(d), h:h + 1, :] + z

    @pl.when(n == pl.num_programs(0) - 1)
    def _():
        sfin_ref[...] = s_ref[...]


def _gdn_scan_call(wq, u, kdt, qk, eg, s0, *, n_heads, n_sub):
    b, _, n, _, w = wq.shape
    c = DN_CHUNK
    t = n * c
    steps = n // n_sub
    assert n % n_sub == 0

    def pair(arr, blk):
        nd = len(blk)
        fwd = pl.BlockSpec((b, 1) + blk, lambda i: (0, 0, i) + (0,) * (nd - 1))
        bwd = pl.BlockSpec((b, 1) + blk, lambda i: (0, 1, steps - 1 - i) + (0,) * (nd - 1))
        return [fwd, bwd], [arr, arr]

    specs, args = [], []
    for arr, blk in ((wq, (n_sub, 2 * c, w)), (u, (n_sub * c, w)), (kdt, (n_sub, n_heads, HEAD_DIM, c)),
                     (qk, (n_sub, n_heads, c, c)), (eg, (n_sub, n_heads, HEAD_DIM))):
        sp, ar = pair(arr, blk)
        specs += sp
        args += ar
    st_spec = pl.BlockSpec(s0.shape, lambda i: (0, 0, 0, 0))
    o_shape = jax.ShapeDtypeStruct((b, t, w), F32)
    return pl.pallas_call(
        functools.partial(_gdn_scan_kernel, n_heads=n_heads, bsz=b, n_sub=n_sub),
        grid=(steps,),
        in_specs=specs + [st_spec],
        out_specs=[pl.BlockSpec((b, n_sub * c, w), lambda i: (0, i, 0)),
                   pl.BlockSpec((b, n_sub * c, w), lambda i: (0, steps - 1 - i, 0)), st_spec],
        out_shape=[o_shape, o_shape, jax.ShapeDtypeStruct(s0.shape, F32)],
        scratch_shapes=[pltpu.VMEM(s0.shape, F32)],
        compiler_params=_params(("arbitrary",)),
        name="gdn_scan",
    )(*args, s0)


def _gate_consts(a_log, dt_bias):
    n_heads = a_log.shape[1]
    z = jnp.zeros((2, n_heads), F32)
    lay = lambda p: jnp.concatenate([z, p.astype(F32)], axis=1).reshape(-1)
    al, dt = lay(a_log), lay(dt_bias)
    pad = lambda vec: jnp.pad(vec, (0, HEAD_DIM - vec.shape[0])).reshape(1, HEAD_DIM)
    return pad(al), pad(dt)


OUT_ROWS = 256


def _outproj_kernel(x_ref, gate_ref, ya_ref, za0_ref, za1_ref, yb_ref, zb_ref, ocf_ref, ocb_ref, zc_ref, on_ref,
                    w_ref, lng_ref, lnb_ref, o_ref, mix_ref, y_ref, *, alpha, wa, wb, n_c_heads):
    blocks = [slice(r0, r0 + OUT_ROWS) for r0 in range(0, x_ref.shape[0], OUT_ROWS)]

    def project(r):
        for i, za_ref in enumerate((za0_ref, za1_ref)):
            cs = slice(i * wa // 2, (i + 1) * wa // 2)
            mix_ref[r, cs] = (ya_ref[r, cs].astype(F32) * _silu(za_ref[r, :].astype(F32))).astype(BF16)
        mix_ref[r, wa:wa + wb] = (yb_ref[r, :].astype(F32) * _silu(zb_ref[r, :].astype(F32))).astype(BF16)
        for h in range(n_c_heads):
            sl = slice(h * HEAD_DIM, (h + 1) * HEAD_DIM)
            oc = ocf_ref[r, sl] + ocb_ref[r, sl]
            yc = oc * lax.rsqrt(jnp.mean(oc * oc, axis=-1, keepdims=True) + EPS) * on_ref[...]
            off = wa + wb + h * HEAD_DIM
            mix_ref[r, off:off + HEAD_DIM] = (yc * _silu(zc_ref[r, sl].astype(F32))).astype(BF16)
        y_ref[r, :] = _dot(mix_ref[r, :], w_ref[...])

    def finish(r):
        t = alpha * x_ref[r, :] + gate_ref[0] * y_ref[r, :]
        tc = t - jnp.mean(t, axis=-1, keepdims=True)
        scale = lax.rsqrt(jnp.mean(tc * tc, axis=-1, keepdims=True) + EPS)
        o_ref[r, :] = tc * scale * lng_ref[...] + lnb_ref[...]

    for r in blocks:
        project(r)
    for r in blocks:
        finish(r)


def _outproj_call(x2d, mod_l, ya, main, yb, ocf, ocb, o_norm, w_out, ln_g, ln_b, *, col_za, col_zb,
                  col_zc, alpha, tiles_per_batch, ctx_row, tm):
    m, d = x2d.shape
    wa, wb, wc = ya.shape[1], yb.shape[1], ocf.shape[1]
    if ctx_row is None:
        row = lambda i: i // tiles_per_batch
    else:
        row = lambda i: ctx_row
    rows = lambda w, cb=0: pl.BlockSpec((tm, w), lambda i: (i, cb))
    one = lambda w: pl.BlockSpec((1, w), lambda i: (0, 0))
    wz = wa // 2
    assert col_za % wz == 0 and col_zb % wb == 0 and col_zc % wc == 0 and tm % OUT_ROWS == 0
    return pl.pallas_call(
        functools.partial(_outproj_kernel, alpha=alpha, wa=wa, wb=wb, n_c_heads=wc // HEAD_DIM),
        grid=(m // tm,),
        in_specs=[rows(d), pl.BlockSpec((1, 1, d), lambda i: (row(i), 0, 2)),
                  rows(wa), rows(wz, col_za // wz), rows(wz, col_za // wz + 1), rows(wb), rows(wb, col_zb // wb),
                  rows(wc), rows(wc), rows(wc, col_zc // wc), one(HEAD_DIM),
                  pl.BlockSpec(w_out.shape, lambda i: (0, 0)), one(d), one(d)],
        out_specs=rows(d),
        out_shape=jax.ShapeDtypeStruct((m, d), F32),
        scratch_shapes=[pltpu.VMEM((tm, wa + wb + wc), BF16), pltpu.VMEM((tm, d), F32)],
        compiler_params=_params(("arbitrary",)),
        name="outproj",
    )(x2d, mod_l, ya, main, main, yb, main, ocf, ocb, main, o_norm.reshape(1, HEAD_DIM), w_out,
      ln_g.reshape(1, d), ln_b.reshape(1, d))


def kernel(x, c, ctx, c_ctx, w_mod, b_mod, w_in, q_norm, k_norm, rpb, conv_w, a_log, dt_bias, o_norm,
           w_out, ln_g, ln_b):
    bsz, seq, d = x.shape
    ctx_len = ctx.shape[1]
    depth = w_mod.shape[0]
    a_heads = d // (2 * HEAD_DIM)
    a_kv = a_heads // 4
    b_heads = d // (4 * HEAD_DIM)
    c_heads = d // (4 * HEAD_DIM)
    wa, wkv, wb, wc = a_heads * HEAD_DIM, a_kv * HEAD_DIM, b_heads * HEAD_DIM, c_heads * HEAD_DIM
    assert bsz < MOD_ROWS and seq % (NBR_Q_ROWS * GRID_W) == 0 and seq // GRID_W >= NBR_K_ROWS
    assert ctx_len % DN_CHUNK == 0 and 4 * c_heads <= 16
    alpha = (2 * depth) ** 0.25

    sizes = dict(qa=wa, ka=wkv, va=wkv, za=wa, qb=wb, kb=wb, vb=wb, zb=wb, qkvc=3 * wc, zc=wc)
    col, off = {}, 0
    for name, size in sizes.items():
        col[name] = off
        off += size
    n_main, n_ab = off, 4 * c_heads

    cc = jnp.concatenate([c, c_ctx[None], jnp.zeros((MOD_ROWS - bsz - 1, d), F32)], axis=0)
    mod = _mod_call(cc, w_mod, b_mod)
    tables = _rope_tables(seq)
    bias = _bias_call(rpb)

    tm_lat = 512
    tm_proj = 2048 if seq % 2048 == 0 else tm_lat
    tm_ctx = min(256, ctx_len)
    tm_proj_ctx = 1024 if (bsz * ctx_len) % 1024 == 0 else tm_ctx
    tn_proj = 512
    main_dtype = BF16
    xl = x.reshape(bsz * seq, d)
    xc = ctx.reshape(bsz * ctx_len, d)
    for l in range(depth):
        with_ctx_out = l < depth - 1
        mod_l = mod[l].reshape(MOD_ROWS, 1, 3 * d)
        w_all = w_in[l].astype(BF16)
        w_ab = jnp.pad(w_all[:, n_main:n_main + n_ab], ((0, 0), (0, HEAD_DIM - n_ab)))
        w_out_l = w_out[l].astype(BF16)
        gate_consts = _gate_consts(a_log[l], dt_bias[l])

        main_x, ab_x = _inproj_call(xc, mod_l, w_all, w_ab, n_main=n_main, tiles_per_batch=None,
                                    ctx_row=bsz, tm=tm_proj_ctx, tn=tn_proj, out_dtype=main_dtype)
        main, ab = _inproj_call(xl, mod_l, w_all, w_ab, n_main=n_main, tiles_per_batch=seq // tm_proj,
                                ctx_row=None, tm=tm_proj, tn=tn_proj, out_dtype=main_dtype)
        main3 = main.reshape(bsz, seq, n_main)
        main_x3 = main_x.reshape(bsz, ctx_len, n_main)

        qa_x, ka_x = _aprep_call(main_x, q_norm[l], k_norm[l], None, col_q=col["qa"], col_k=col["ka"],
                                 nq=a_heads, nk=a_kv, tm=tm_ctx, tiles_per_batch=ctx_len // tm_ctx)
        qa, ka = _aprep_call(main, q_norm[l], k_norm[l], tables, col_q=col["qa"], col_k=col["ka"],
                             nq=a_heads, nk=a_kv, tm=tm_lat, tiles_per_batch=seq // tm_lat)
        ka_x3 = ka_x.reshape(bsz, ctx_len, wkv)
        ya = _gqa_call(qa.reshape(bsz, seq, wa), ka_x3, main_x3, col["va"], ka.reshape(bsz, seq, wkv), main3,
                       col["va"], n_kv_heads=a_kv, group=a_heads // a_kv, tq=256, tk=1024, out_dtype=BF16)

        yb = _nbr_call(main3, main_x3, bias, layer=l, col_q=col["qb"], col_k=col["kb"], col_v=col["vb"],
                       n_heads=b_heads, out_dtype=BF16)

        def delta(main2d, ab2d, t_len, tm, s0):
            qn, kn, vv = _dnprep_call(main2d, conv_w[l], col=col["qkvc"], n_heads=c_heads, tm=tm,
                                      tiles_per_batch=t_len // tm)
            r3 = lambda a: a.reshape(bsz, t_len, a.shape[-1])
            local = _gdn_local_call(r3(qn), r3(kn), r3(vv), r3(ab2d), gate_consts, n_heads=c_heads,
                                    n_chunks=4, inv_precision=INV_PRECISION)
            return _gdn_scan_call(*local, s0, n_heads=c_heads, n_sub=2)

        s_zero = jnp.zeros((bsz, 2 * c_heads, HEAD_DIM, HEAD_DIM), F32)
        ocf_x, ocb_x, s_ctx = delta(main_x, ab_x, ctx_len, tm_ctx, s_zero)
        ocf, ocb, _ = delta(main, ab, seq, tm_lat, s_ctx)

        x_new = _outproj_call(xl, mod_l, ya.reshape(bsz * seq, wa), main, yb.reshape(bsz * seq, wb),
                              ocf.reshape(bsz * seq, wc), ocb.reshape(bsz * seq, wc), o_norm[l], w_out_l,
                              ln_g[l], ln_b[l], col_za=col["za"], col_zb=col["zb"], col_zc=col["zc"],
                              alpha=alpha, tiles_per_batch=seq // tm_lat, ctx_row=None, tm=tm_lat)

        if with_ctx_out:
            ya_x = _flash_call(qa_x.reshape(bsz, ctx_len, wa), [(ka_x3, 0, main_x3, col["va"])],
                               n_kv_heads=a_kv, group=a_heads // a_kv, q_col=0, tq=ctx_len, tk=ctx_len,
                               q_scale=None, out_dtype=F32)
            yb_x = _flash_call(main_x3, [(main_x3, col["kb"], main_x3, col["vb"])],
                               n_kv_heads=b_heads, group=1, q_col=col["qb"], tq=ctx_len, tk=ctx_len,
                               q_scale=SCORE_SCALE, out_dtype=F32)
            xc = _outproj_call(xc, mod_l, ya_x.reshape(bsz * ctx_len, wa), main_x,
                               yb_x.reshape(bsz * ctx_len, wb), ocf_x.reshape(bsz * ctx_len, wc),
                               ocb_x.reshape(bsz * ctx_len, wc), o_norm[l], w_out_l, ln_g[l], ln_b[l],
                               col_za=col["za"], col_zb=col["zb"], col_zc=col["zc"], alpha=alpha,
                               tiles_per_batch=None, ctx_row=bsz, tm=tm_ctx)
        xl = x_new
    return xl.reshape(bsz, seq, d)
```

```python
import functools

import jax
import jax.numpy as jnp
from jax import lax
from jax.experimental import pallas as pl
from jax.experimental.pallas import tpu as pltpu

F32 = jnp.float32
BF16 = jnp.bfloat16

HEAD_DIM = 128
GRID_W = 64
ROPE_THETA = 10000.0
NA_WIN_H = 8
NA_WIN_W = 16
CONV_K = 3
DN_CHUNK = 64
EPS = 1e-6
NEG = -1e30
MOD_ROWS = 8
NBR_Q_ROWS = 8
NBR_K_ROWS = 16
NBR_HALF_Q = 4
NBR_HALF_K = 12
VMEM_LIMIT = 56 * 1024 * 1024
INPROJ_VMEM_LIMIT = 60 * 1024 * 1024
LOG2E = 1.4426950408889634
SCORE_SCALE = HEAD_DIM ** -0.5 * LOG2E

HIGHEST = lax.Precision.HIGHEST
INV_PRECISION = None


def _silu(z):
    return z * jax.nn.sigmoid(z)


def _softplus(z):
    return jnp.maximum(z, 0.0) + jnp.log(1.0 + jnp.exp(-jnp.abs(z)))


def _dot(a, b, precision=None):
    return jnp.dot(a, b, preferred_element_type=F32, precision=precision)


def _dot_nt(a, b, precision=None):
    return lax.dot_general(a, b, (((1,), (1,)), ((), ())), preferred_element_type=F32,
                           precision=precision)


def _params(sem, vmem_limit=VMEM_LIMIT):
    return pltpu.CompilerParams(dimension_semantics=sem, vmem_limit_bytes=vmem_limit)


def _mod_kernel(c_ref, w_ref, b_ref, o_ref):
    s = _silu(c_ref[...]).astype(BF16)
    o_ref[0] = _dot(s, w_ref[0].astype(BF16)) + b_ref[0]


def _mod_call(cc, w_mod, b_mod):
    depth, d, n = w_mod.shape
    tn = 768
    return pl.pallas_call(
        _mod_kernel,
        grid=(depth, n // tn),
        in_specs=[pl.BlockSpec((MOD_ROWS, d), lambda l, j: (0, 0)),
                  pl.BlockSpec((1, d, tn), lambda l, j: (l, 0, j)),
                  pl.BlockSpec((1, 1, tn), lambda l, j: (l, 0, j))],
        out_specs=pl.BlockSpec((1, MOD_ROWS, tn), lambda l, j: (l, 0, j)),
        out_shape=jax.ShapeDtypeStruct((depth, MOD_ROWS, n), F32),
        compiler_params=_params(("arbitrary", "arbitrary")),
        name="mod",
    )(cc, w_mod, b_mod.reshape(depth, 1, n))


LN_ROWS = 256


def _inproj_kernel(x_ref, shift_ref, scale_ref, w_ref, wab_ref, o_ref, oab_ref, h_ref):
    first = pl.program_id(1) == 0

    @pl.when(first)
    def _():
        for r0 in range(0, x_ref.shape[0], LN_ROWS):
            r = slice(r0, r0 + LN_ROWS)
            x = x_ref[r, :]
            xc = x - jnp.mean(x, axis=-1, keepdims=True)
            y = xc * lax.rsqrt(jnp.mean(xc * xc, axis=-1, keepdims=True) + EPS)
            h_ref[r, :] = (y * (1.0 + scale_ref[0]) + shift_ref[0]).astype(BF16)
            oab_ref[r, :] = _dot(h_ref[r, :], wab_ref[...])
            o_ref[r, :] = _dot(h_ref[r, :], w_ref[...]).astype(o_ref.dtype)

    @pl.when(jnp.logical_not(first))
    def _():
        o_ref[...] = _dot(h_ref[...], w_ref[...]).astype(o_ref.dtype)


def _inproj_call(x2d, mod_l, w_all, w_ab, *, n_main, tiles_per_batch, ctx_row, tm, tn, out_dtype):
    m, d = x2d.shape
    n = n_main
    assert n % tn == 0 and tm % LN_ROWS == 0
    if ctx_row is None:
        row = lambda i: i // tiles_per_batch
    else:
        row = lambda i: ctx_row
    return pl.pallas_call(
        _inproj_kernel,
        grid=(m // tm, n // tn),
        in_specs=[pl.BlockSpec((tm, d), lambda i, j: (i, 0)),
                  pl.BlockSpec((1, 1, d), lambda i, j: (row(i), 0, 0)),
                  pl.BlockSpec((1, 1, d), lambda i, j: (row(i), 0, 1)),
                  pl.BlockSpec((d, tn), lambda i, j: (0, j)),
                  pl.BlockSpec((d, HEAD_DIM), lambda i, j: (0, 0))],
        out_specs=[pl.BlockSpec((tm, tn), lambda i, j: (i, j)),
                   pl.BlockSpec((tm, HEAD_DIM), lambda i, j: (i, 0))],
        out_shape=[jax.ShapeDtypeStruct((m, n), out_dtype),
                   jax.ShapeDtypeStruct((m, HEAD_DIM), F32)],
        scratch_shapes=[pltpu.VMEM((tm, d), BF16)],
        compiler_params=_params(("arbitrary", "arbitrary"), INPROJ_VMEM_LIMIT),
        name="inproj",
    )(x2d, mod_l, mod_l, w_all, w_ab)


def _swap_quarters(x):
    lane = lax.broadcasted_iota(jnp.int32, x.shape, 1)
    fwd = pltpu.roll(x, 3 * HEAD_DIM // 4, axis=1)
    bwd = pltpu.roll(x, HEAD_DIM // 4, axis=1)
    return jnp.where((lane % (HEAD_DIM // 2)) < HEAD_DIM // 4, fwd, bwd)


def _aprep_kernel(*refs, nq, nk, rope):
    if rope:
        q_ref, k_ref, qw_ref, kw_ref, cos_ref, sin_ref, qo_ref, ko_ref = refs
        cos, sin = cos_ref[...], sin_ref[...]
    else:
        q_ref, k_ref, qw_ref, kw_ref, qo_ref, ko_ref = refs

    ones = jnp.ones((2 * HEAD_DIM, HEAD_DIM), BF16)

    def norm(x, w):
        sq = x * x
        hi = sq.astype(BF16)
        lo = (sq - hi.astype(F32)).astype(BF16)
        ss = _dot(jnp.concatenate([hi, lo], axis=1), ones)
        y = x * lax.rsqrt(ss * (1.0 / HEAD_DIM) + EPS) * w
        if rope:
            y = y * cos + _swap_quarters(y) * sin
        return y

    for h in range(nq):
        sl = slice(h * HEAD_DIM, (h + 1) * HEAD_DIM)
        y = norm(q_ref[:, sl].astype(F32), qw_ref[...])
        qo_ref[:, sl] = (y * SCORE_SCALE).astype(qo_ref.dtype)
    for h in range(nk):
        sl = slice(h * HEAD_DIM, (h + 1) * HEAD_DIM)
        ko_ref[:, sl] = norm(k_ref[:, sl].astype(F32), kw_ref[...]).astype(ko_ref.dtype)


def _aprep_call(main, q_norm, k_norm, tables, *, col_q, col_k, nq, nk, tm, tiles_per_batch):
    m = main.shape[0]
    wq, wk = nq * HEAD_DIM, nk * HEAD_DIM
    in_specs = [pl.BlockSpec((tm, wq), lambda i: (i, col_q // wq)),
                pl.BlockSpec((tm, wk), lambda i: (i, col_k // wk)),
                pl.BlockSpec((1, HEAD_DIM), lambda i: (0, 0)),
                pl.BlockSpec((1, HEAD_DIM), lambda i: (0, 0))]
    args = [main, main, q_norm.reshape(1, HEAD_DIM), k_norm.reshape(1, HEAD_DIM)]
    if tables is not None:
        in_specs += [pl.BlockSpec((tm, HEAD_DIM), lambda i: (i % tiles_per_batch, 0))] * 2
        args += list(tables)
    return pl.pallas_call(
        functools.partial(_aprep_kernel, nq=nq, nk=nk, rope=tables is not None),
        grid=(m // tm,),
        in_specs=in_specs,
        out_specs=[pl.BlockSpec((tm, wq), lambda i: (i, 0)),
                   pl.BlockSpec((tm, wk), lambda i: (i, 0))],
        out_shape=[jax.ShapeDtypeStruct((m, wq), BF16), jax.ShapeDtypeStruct((m, wk), BF16)],
        compiler_params=_params(("arbitrary",)),
        name="aprep",
    )(*args)


def _rope_tables(n_tokens):
    t = jnp.arange(n_tokens, dtype=jnp.int32)
    row = (t // GRID_W).astype(F32)
    col = (t % GRID_W).astype(F32)
    half = HEAD_DIM // 2
    inv_freq = ROPE_THETA ** (-jnp.arange(0, half, 2, dtype=F32) / half)
    ar, ac = row[:, None] * inv_freq, col[:, None] * inv_freq
    cos = jnp.concatenate([jnp.cos(ar), jnp.cos(ar), jnp.cos(ac), jnp.cos(ac)], -1)
    sin = jnp.concatenate([-jnp.sin(ar), jnp.sin(ar), -jnp.sin(ac), jnp.sin(ac)], -1)
    return cos, sin


def _flash_kernel(*refs, group, tq, sources, q_scale):
    q_ref = refs[0]
    o_ref = refs[1 + 2 * len(sources)]
    q = jnp.concatenate([q_ref[0, :, g * HEAD_DIM:(g + 1) * HEAD_DIM] for g in range(group)], axis=0)
    if q_scale is not None:
        q = q.astype(F32) * q_scale
    q = q.astype(BF16)
    rows = group * tq

    def step(carry, k, v):
        m, l, acc = carry
        s = _dot_nt(q, k.astype(BF16))
        m_new = jnp.maximum(m, jnp.max(s, axis=-1, keepdims=True))
        alpha = jnp.exp2(m - m_new)
        p = jnp.exp2(s - m_new)
        l = alpha * l + jnp.sum(p, axis=-1, keepdims=True)
        acc = alpha * acc + _dot(p.astype(BF16), v.astype(BF16))
        return m_new, l, acc

    carry = (jnp.full((rows, 1), -jnp.inf, F32), jnp.zeros((rows, 1), F32),
             jnp.zeros((rows, HEAD_DIM), F32))
    for si, (length, tk) in enumerate(sources):
        k_ref, v_ref = refs[1 + 2 * si], refs[2 + 2 * si]
        if length == tk:
            carry = step(carry, k_ref[0], v_ref[0])
        else:
            def body(i, c, k_ref=k_ref, v_ref=v_ref, tk=tk):
                st = pl.multiple_of(i * tk, tk)
                return step(c, k_ref[0, pl.ds(st, tk), :], v_ref[0, pl.ds(st, tk), :])
            carry = lax.fori_loop(0, length // tk, body, carry)
    _, l, acc = carry
    o = acc / l
    for g in range(group):
        o_ref[0, :, g * HEAD_DIM:(g + 1) * HEAD_DIM] = o[g * tq:(g + 1) * tq].astype(o_ref.dtype)


SOFTMAX_UNIT = 16


def _gqa_kernel(q_ref, kc_ref, vc_ref, kl_ref, vl_ref, o_ref, q4_ref, s0_ref, s1_ref, sc_ref, p_ref,
                m_ref, l_ref, a_ref, acc_ref, *, group, tq, tk):
    rows = group * tq
    half = rows // 2
    n_lat = kl_ref.shape[1] // tk
    for g in range(group):
        q4_ref[g * tq:(g + 1) * tq, :] = q_ref[0, :, g * HEAD_DIM:(g + 1) * HEAD_DIM]
    m_ref[...] = jnp.full(m_ref.shape, -jnp.inf, F32)
    l_ref[...] = jnp.zeros(l_ref.shape, F32)
    acc_ref[...] = jnp.zeros(acc_ref.shape, F32)
    halves = (slice(0, half), slice(half, rows))

    def issue(dst_ref, k, rs):
        dst_ref[rs, :] = _dot_nt(q4_ref[rs, :], k)

    def softmax(src_ref, rs, width):
        for r0 in range(rs.start, rs.stop, SOFTMAX_UNIT):
            r = slice(r0, r0 + SOFTMAX_UNIT)
            cols = [slice(j, j + HEAD_DIM) for j in range(0, width, HEAD_DIM)]
            s = [src_ref[r, c] for c in cols]
            mx = functools.reduce(jnp.maximum, s)
            m_old = m_ref[r, :]
            m_new = jnp.maximum(m_old, jnp.max(mx, axis=-1, keepdims=True))
            alpha = jnp.exp2(m_old - m_new)
            p = [jnp.exp2(x - m_new) for x in s]
            for c, x in zip(cols, p):
                p_ref[r, c] = x.astype(BF16)
            m_ref[r, :] = m_new
            a_ref[r, :] = alpha
            l_ref[r, :] = alpha * l_ref[r, :] + functools.reduce(jnp.add, p)

    def accumulate(rs, width, v):
        pv = _dot(p_ref[rs, 0:width], v)
        acc_ref[rs, :] = acc_ref[rs, :] * a_ref[rs, :] + pv

    def chunk_step(cur_ref, width, v, nxt_ref, k_next):
        for rs in halves:
            if nxt_ref is not None:
                issue(nxt_ref, k_next, rs)
            softmax(cur_ref, rs, width)
            accumulate(rs, width, v)

    lat = lambda ref, i: ref[0, pl.ds(pl.multiple_of(i * tk, tk), tk), :]
    for rs in halves:
        issue(s0_ref, lat(kl_ref, 0), rs)

    def pair(j, carry):
        i = 2 * j
        chunk_step(s0_ref, tk, lat(vl_ref, i), s1_ref, lat(kl_ref, i + 1))
        chunk_step(s1_ref, tk, lat(vl_ref, i + 1), s0_ref, lat(kl_ref, i + 2))
        return carry

    lax.fori_loop(0, n_lat // 2 - 1, pair, 0)
    chunk_step(s0_ref, tk, lat(vl_ref, n_lat - 2), s1_ref, lat(kl_ref, n_lat - 1))
    chunk_step(s1_ref, tk, lat(vl_ref, n_lat - 1), sc_ref, kc_ref[0])
    chunk_step(sc_ref, kc_ref.shape[1], vc_ref[0], None, None)

    o = acc_ref[...] / jnp.sum(l_ref[...], axis=-1, keepdims=True)
    for g in range(group):
        o_ref[0, :, g * HEAD_DIM:(g + 1) * HEAD_DIM] = o[g * tq:(g + 1) * tq].astype(o_ref.dtype)


def _gqa_call(q, k_ctx, v_ctx, v_ctx_col, k_lat, v_lat, v_lat_col, *, n_kv_heads, group, tq, tk, out_dtype):
    b, s, _ = q.shape
    ctx_len = k_ctx.shape[1]
    wq = group * HEAD_DIM
    rows = group * tq
    assert s % (2 * tk) == 0 and s // tk >= 2 and ctx_len % HEAD_DIM == 0 and ctx_len <= tk
    assert rows % (2 * SOFTMAX_UNIT) == 0
    cvc, cvl = v_ctx_col // HEAD_DIM, v_lat_col // HEAD_DIM
    kv_spec = lambda t, c: pl.BlockSpec((1, t, HEAD_DIM), lambda bi, g, qi: (bi, 0, c + g))
    stat = pltpu.VMEM((rows, HEAD_DIM), F32)
    return pl.pallas_call(
        functools.partial(_gqa_kernel, group=group, tq=tq, tk=tk),
        grid=(b, n_kv_heads, s // tq),
        in_specs=[pl.BlockSpec((1, tq, wq), lambda bi, g, qi: (bi, qi, g)),
                  kv_spec(ctx_len, 0), kv_spec(ctx_len, cvc), kv_spec(s, 0), kv_spec(s, cvl)],
        out_specs=pl.BlockSpec((1, tq, wq), lambda bi, g, qi: (bi, qi, g)),
        out_shape=jax.ShapeDtypeStruct((b, s, n_kv_heads * wq), out_dtype),
        scratch_shapes=[pltpu.VMEM((rows, HEAD_DIM), BF16), pltpu.VMEM((rows, tk), F32),
                        pltpu.VMEM((rows, tk), F32), pltpu.VMEM((rows, ctx_len), F32),
                        pltpu.VMEM((rows, tk), BF16), stat, stat, stat, stat],
        compiler_params=_params(("arbitrary", "arbitrary", "arbitrary")),
        name="gqa",
    )(q, k_ctx, v_ctx, k_lat, v_lat)


def _flash_call(q, kv, *, n_kv_heads, group, q_col, tq, tk, q_scale, out_dtype):
    b, t_q, _ = q.shape
    wq = group * HEAD_DIM
    in_specs = [pl.BlockSpec((1, tq, wq), lambda bi, g, qi: (bi, qi, q_col // wq + g))]
    args = [q]
    sources = []
    for k_arr, k_col, v_arr, v_col in kv:
        t_k = k_arr.shape[1]
        in_specs.append(pl.BlockSpec((1, t_k, HEAD_DIM),
                                     lambda bi, g, qi, c=k_col // HEAD_DIM: (bi, 0, c + g)))
        in_specs.append(pl.BlockSpec((1, t_k, HEAD_DIM),
                                     lambda bi, g, qi, c=v_col // HEAD_DIM: (bi, 0, c + g)))
        args += [k_arr, v_arr]
        sources.append((t_k, min(tk, t_k)))
    return pl.pallas_call(
        functools.partial(_flash_kernel, group=group, tq=tq, sources=tuple(sources), q_scale=q_scale),
        grid=(b, n_kv_heads, t_q // tq),
        in_specs=in_specs,
        out_specs=pl.BlockSpec((1, tq, wq), lambda bi, g, qi: (bi, qi, g)),
        out_shape=jax.ShapeDtypeStruct((b, t_q, n_kv_heads * wq), out_dtype),
        compiler_params=_params(("arbitrary", "arbitrary", "arbitrary")),
        name="flash",
    )(*args)


def _nbr_window(cls, qr, kr):
    if cls == 0:
        lo, dr = max(qr - NA_WIN_H // 2, 0), kr - qr
    elif cls == 1:
        lo, dr = qr, kr - NA_WIN_H // 2 - qr
    else:
        lo, dr = min(NA_WIN_H // 2 + qr, NBR_K_ROWS - NA_WIN_H), kr - NBR_Q_ROWS - qr
    if lo <= kr < lo + NA_WIN_H:
        assert -NA_WIN_H < dr < NA_WIN_H
        return dr + NA_WIN_H - 1
    return None


def _nbr_half_offset(cls, hf):
    return (0, (0, NBR_HALF_Q)[hf], NBR_HALF_Q)[cls]


def _bias_kernel(rpb_ref, o_ref, *, n_heads):
    nrow, ncol = 2 * NA_WIN_H - 1, 2 * NA_WIN_W - 1
    base = (pl.program_id(0) * n_heads + pl.program_id(1)) * (nrow * ncol)
    shape = (GRID_W, 2 * GRID_W)
    c = lax.broadcasted_iota(jnp.int32, shape, 0)
    lane = lax.broadcasted_iota(jnp.int32, shape, 1)
    right = lane >= GRID_W
    kc = jnp.where(right, lane - GRID_W, lane)
    c0 = jnp.clip(c - NA_WIN_W // 2, 0, GRID_W - NA_WIN_W)
    col_ok = (kc >= c0) & (kc < c0 + NA_WIN_W)
    idx = kc - c + (NA_WIN_W - 1)
    cache = {}

    def tile(a_left, a_right):
        key = (a_left, a_right)
        if key not in cache:
            t = jnp.full(shape, NEG, F32)
            if key != (None, None):
                for j in range(ncol):
                    sl = NEG if a_left is None else rpb_ref[base + a_left * ncol + j]
                    sr = NEG if a_right is None else rpb_ref[base + a_right * ncol + j]
                    t = jnp.where(idx == j, jnp.where(right, sr, sl), t)
                t = jnp.where(col_ok, t * LOG2E, NEG)
            cache[key] = t
        return cache[key]

    for cls in range(3):
        for hf in range(2):
            off = _nbr_half_offset(cls, hf)
            assert all(off <= kr < off + NBR_HALF_K for q in range(hf * NBR_HALF_Q, (hf + 1) * NBR_HALF_Q)
                       for kr in range(NBR_K_ROWS) if _nbr_window(cls, q, kr) is not None)
            for qr in range(NBR_HALF_Q):
                for p in range(NBR_HALF_K // 2):
                    kr = _nbr_half_offset(cls, hf) + 2 * p
                    t = tile(_nbr_window(cls, hf * NBR_HALF_Q + qr, kr),
                             _nbr_window(cls, hf * NBR_HALF_Q + qr, kr + 1))
                    o_ref[0, 0, cls, hf, qr * GRID_W:(qr + 1) * GRID_W, p * 2 * GRID_W:(p + 1) * 2 * GRID_W] = t


def _bias_call(rpb):
    depth, n_heads = rpb.shape[:2]
    shape = (depth, n_heads, 3, 2, NBR_HALF_Q * GRID_W, NBR_HALF_K * GRID_W)
    return pl.pallas_call(
        functools.partial(_bias_kernel, n_heads=n_heads),
        grid=(depth, n_heads),
        in_specs=[pl.BlockSpec(memory_space=pltpu.SMEM)],
        out_specs=pl.BlockSpec((1, 1) + shape[2:], lambda l, h: (l, h, 0, 0, 0, 0)),
        out_shape=jax.ShapeDtypeStruct(shape, F32),
        compiler_params=_params(("arbitrary", "arbitrary")),
        name="nbr_bias",
    )(rpb.reshape(-1))


def _nbr_kernel(q_ref, k_ref, v_ref, kc_ref, vc_ref, bias_ref, o_ref, *, rows):
    g = pl.program_id(2)
    n_steps = rows // NBR_Q_ROWS
    start_row = jnp.clip(g * NBR_Q_ROWS - NA_WIN_H // 2, 0, rows - NBR_K_ROWS)
    start = pl.multiple_of(start_row * GRID_W, (NA_WIN_H // 2) * GRID_W)
    cls = jnp.where(g == 0, 0, jnp.where(g == n_steps - 1, 2, 1))
    nq, nk = NBR_HALF_Q * GRID_W, NBR_HALF_K * GRID_W
    kc, vc = kc_ref[0].astype(BF16), vc_ref[0].astype(BF16)
    scores = []
    for hf in range(2):
        off = jnp.where(cls == 0, 0, jnp.where(cls == 1, hf * NBR_HALF_Q, NBR_HALF_Q))
        st = pl.multiple_of(start + off * GRID_W, NBR_HALF_Q * GRID_W)
        q = (q_ref[0, hf * nq:(hf + 1) * nq, :].astype(F32) * SCORE_SCALE).astype(BF16)
        s_win = _dot_nt(q, k_ref[0, pl.ds(st, nk), :].astype(BF16)) + bias_ref[0, 0, cls, hf]
        scores.append((st, s_win, _dot_nt(q, kc)))
    for hf, (st, s_win, s_ctx) in enumerate(scores):
        m = jnp.maximum(jnp.max(s_win, axis=-1, keepdims=True), jnp.max(s_ctx, axis=-1, keepdims=True))
        p_win = jnp.exp2(s_win - m)
        p_ctx = jnp.exp2(s_ctx - m)
        l = jnp.sum(p_win, axis=-1, keepdims=True) + jnp.sum(p_ctx, axis=-1, keepdims=True)
        o = _dot(p_win.astype(BF16), v_ref[0, pl.ds(st, nk), :].astype(BF16)) + _dot(p_ctx.astype(BF16), vc)
        o_ref[0, hf * nq:(hf + 1) * nq, :] = (o / l).astype(o_ref.dtype)


def _nbr_call(main, main_ctx, bias, *, layer, col_q, col_k, col_v, n_heads, out_dtype):
    b, s, _ = main.shape
    ctx_len = main_ctx.shape[1]
    rows = s // GRID_W
    tq = NBR_Q_ROWS * GRID_W
    cq, ck, cv = col_q // HEAD_DIM, col_k // HEAD_DIM, col_v // HEAD_DIM
    return pl.pallas_call(
        functools.partial(_nbr_kernel, rows=rows),
        grid=(n_heads, b, rows // NBR_Q_ROWS),
        in_specs=[pl.BlockSpec((1, tq, HEAD_DIM), lambda h, bi, g: (bi, g, cq + h)),
                  pl.BlockSpec((1, s, HEAD_DIM), lambda h, bi, g: (bi, 0, ck + h)),
                  pl.BlockSpec((1, s, HEAD_DIM), lambda h, bi, g: (bi, 0, cv + h)),
                  pl.BlockSpec((1, ctx_len, HEAD_DIM), lambda h, bi, g: (bi, 0, ck + h)),
                  pl.BlockSpec((1, ctx_len, HEAD_DIM), lambda h, bi, g: (bi, 0, cv + h)),
                  pl.BlockSpec((1, 1) + bias.shape[2:], lambda h, bi, g: (layer, h, 0, 0, 0, 0))],
        out_specs=pl.BlockSpec((1, tq, HEAD_DIM), lambda h, bi, g: (bi, g, h)),
        out_shape=jax.ShapeDtypeStruct((b, s, n_heads * HEAD_DIM), out_dtype),
        compiler_params=_params(("arbitrary", "arbitrary", "arbitrary")),
        name="nbr_attn",
    )(main, main, main, main_ctx, main_ctx, bias)


def _dnprep_kernel(x_ref, prev_ref, next_ref, w_ref, q_ref, k_ref, v_ref, *, tiles_per_batch, n_heads):
    i = pl.program_id(0) % tiles_per_batch
    x = x_ref[...].astype(F32)
    tm = x.shape[0]
    row = lax.broadcasted_iota(jnp.int32, x.shape, 0)
    halo = prev_ref.shape[0]
    prev_row = jnp.where(i == 0, 0.0, prev_ref[halo - 1:halo, :].astype(F32))
    next_row = jnp.where(i == tiles_per_batch - 1, 0.0, next_ref[0:1, :].astype(F32))
    x_prev = jnp.where(row == 0, prev_row, pltpu.roll(x, 1, axis=0))
    x_next = jnp.where(row == tm - 1, next_row, pltpu.roll(x, tm - 1, axis=0))
    y = _silu(w_ref[0:1, :] * x_prev + w_ref[1:2, :] * x + w_ref[2:3, :] * x_next)
    w = n_heads * HEAD_DIM
    for h in range(n_heads):
        sl = slice(h * HEAD_DIM, (h + 1) * HEAD_DIM)
        qh = y[:, h * HEAD_DIM:(h + 1) * HEAD_DIM]
        kh = y[:, w + h * HEAD_DIM:w + (h + 1) * HEAD_DIM]
        qn = qh * lax.rsqrt(jnp.sum(qh * qh, axis=-1, keepdims=True) + EPS) * HEAD_DIM ** -0.5
        q_ref[:, sl] = qn.astype(q_ref.dtype)
        k_ref[:, sl] = (kh * lax.rsqrt(jnp.sum(kh * kh, axis=-1, keepdims=True) + EPS)).astype(k_ref.dtype)
    v_ref[...] = y[:, 2 * w:].astype(v_ref.dtype)


def _dnprep_call(main, conv_w, *, col, n_heads, tm, tiles_per_batch):
    m = main.shape[0]
    w3 = 3 * n_heads * HEAD_DIM
    halo = 16
    hb = tm // halo
    n_halo = m // halo
    out = jax.ShapeDtypeStruct((m, n_heads * HEAD_DIM), BF16)
    o_spec = pl.BlockSpec((tm, n_heads * HEAD_DIM), lambda i: (i, 0))
    return pl.pallas_call(
        functools.partial(_dnprep_kernel, tiles_per_batch=tiles_per_batch, n_heads=n_heads),
        grid=(m // tm,),
        in_specs=[pl.BlockSpec((tm, w3), lambda i: (i, col // w3)),
                  pl.BlockSpec((halo, w3), lambda i: (jnp.maximum(i * hb - 1, 0), col // w3)),
                  pl.BlockSpec((halo, w3), lambda i: (jnp.minimum((i + 1) * hb, n_halo - 1), col // w3)),
                  pl.BlockSpec((CONV_K, w3), lambda i: (0, 0))],
        out_specs=[o_spec, o_spec, o_spec],
        out_shape=[out, out, out],
        compiler_params=_params(("arbitrary",)),
        name="dn_prep",
    )(main, main, main, conv_w)


def _tri_masks(n, upper):
    i = lax.broadcasted_iota(jnp.int32, (n, n), 0)
    j = lax.broadcasted_iota(jnp.int32, (n, n), 1)
    incl = (j >= i) if upper else (j <= i)
    strict = (j > i) if upper else (j < i)
    eye = (i == j).astype(F32)
    base = 8
    same_base = (i // base) == (j // base)
    levels = []
    s = base
    while s < n:
        levels.append(((i // (2 * s)) == (j // (2 * s))) & ((i // s) != (j // s)))
        s *= 2
    return incl, strict, eye, same_base, levels


def _unit_tri_inverse(lms, masks, precision):
    mm = lambda a, b: _dot(a, b, precision) if precision is not None else _dot(a.astype(BF16), b.astype(BF16))
    eyes = [m[2] for m in masks]
    l1 = [jnp.where(m[3], lm, 0.0) for lm, m in zip(lms, masks)]
    l2 = [mm(a, a) for a in l1]
    l4 = [mm(a, a) for a in l2]
    t = [mm(e - a, e + b) for e, a, b in zip(eyes, l1, l2)]
    t = [mm(a, e + b) for e, a, b in zip(eyes, t, l4)]
    for lvl in range(len(masks[0][4])):
        cs = [jnp.where(m[4][lvl], lm, 0.0) for lm, m in zip(lms, masks)]
        tc = [mm(a, b) for a, b in zip(t, cs)]
        t = [a - mm(b, a) for a, b in zip(t, tc)]
    return t


def _gdn_local_kernel(q_ref, k_ref, v_ref, ab_ref, alr_ref, dtr_ref,
                      wq_ref, u_ref, kdt_ref, qk_ref, eg_ref, *, n_heads, n_chunks, inv_precision):
    c = DN_CHUNK
    masks = [_tri_masks(c, upper) for upper in (False, True)]
    gates = {}
    for d in range(2):
        cum = masks[d][0].astype(F32)
        for ci in range(n_chunks):
            ab = ab_ref[0, ci * c:(ci + 1) * c, :]
            g_tok = -jnp.exp(alr_ref[...]) * _softplus(ab + dtr_ref[...])
            gc_tok = _dot(cum, g_tok, HIGHEST)
            gates[ci, d] = (jax.nn.sigmoid(ab), gc_tok, gc_tok.T)

    streams = [(ci, d, h) for ci in range(n_chunks) for d in range(2) for h in range(n_heads)]
    pre = []
    for ci, d, h in streams:
        incl, strict = masks[d][0], masks[d][1]
        rows = slice(ci * c, (ci + 1) * c)
        sl = slice(h * HEAD_DIM, (h + 1) * HEAD_DIM)
        cb, cg = d * 2 * n_heads + h, d * 2 * n_heads + n_heads + h
        beta_tok, gc_tok, gc_rows = gates[ci, d]
        q, k, v = q_ref[0, rows, sl], k_ref[0, rows, sl], v_ref[0, rows, sl]
        beta_c = beta_tok[:, cb:cb + 1]
        gc_c = gc_tok[:, cg:cg + 1]
        last = 0 if d == 1 else c - 1
        g_last = gc_c[last:last + 1, :]
        decay = jnp.where(incl, jnp.exp(jnp.where(incl, gc_c - gc_rows[cg:cg + 1, :], 0.0)), 0.0)
        kbeta = k * beta_c
        e_c = jnp.exp(gc_c)
        wq_ref[0, d, ci, c:2 * c, sl] = (q * e_c).astype(wq_ref.dtype)
        kdt_ref[0, d, ci, h] = (k * jnp.exp(g_last - gc_c)).T.astype(kdt_ref.dtype)
        eg_ref[0, d, ci, h:h + 1, :] = jnp.broadcast_to(jnp.exp(g_last), (1, HEAD_DIM))
        lhs = jnp.concatenate([kbeta, q], axis=0).astype(BF16)
        rhs = jnp.concatenate([v * beta_c, kbeta * e_c], axis=1).astype(BF16)
        pre.append((lhs, k.astype(BF16), rhs, decay))
    kq = [_dot_nt(lhs, kb) for lhs, kb, _, _ in pre]
    lms = []
    for (ci, d, h), x, (_, _, _, decay) in zip(streams, kq, pre):
        qk_ref[0, d, ci, h] = (x[c:] * decay).astype(qk_ref.dtype)
        lms.append(jnp.where(masks[d][1], x[:c] * decay, 0.0))
    ts = _unit_tri_inverse(lms, [masks[d] for _, d, _ in streams], inv_precision)
    uws = [_dot(t.astype(BF16), rhs) for t, (_, _, rhs, _) in zip(ts, pre)]
    for (ci, d, h), uw in zip(streams, uws):
        sl = slice(h * HEAD_DIM, (h + 1) * HEAD_DIM)
        u_ref[0, d, ci * c:(ci + 1) * c, sl] = uw[:, :HEAD_DIM]
        wq_ref[0, d, ci, 0:c, sl] = uw[:, HEAD_DIM:].astype(wq_ref.dtype)


def _gdn_local_call(qn, kn, vv, ab, gate_consts, *, n_heads, n_chunks, inv_precision):
    b, t, w = qn.shape
    c = DN_CHUNK
    n = t // c
    tm = n_chunks * c
    tok = pl.BlockSpec((1, tm, w), lambda bi, i: (bi, i, 0))
    const = lambda a: pl.BlockSpec(a.shape, lambda bi, i: (0,) * a.ndim)
    return pl.pallas_call(
        functools.partial(_gdn_local_kernel, n_heads=n_heads, n_chunks=n_chunks, inv_precision=inv_precision),
        grid=(b, n // n_chunks),
        in_specs=[tok, tok, tok, pl.BlockSpec((1, tm, HEAD_DIM), lambda bi, i: (bi, i, 0))]
                 + [const(a) for a in gate_consts],
        out_specs=[pl.BlockSpec((1, 2, n_chunks, 2 * c, w), lambda bi, i: (bi, 0, i, 0, 0)),
                   pl.BlockSpec((1, 2, tm, w), lambda bi, i: (bi, 0, i, 0)),
                   pl.BlockSpec((1, 2, n_chunks, n_heads, HEAD_DIM, c), lambda bi, i: (bi, 0, i, 0, 0, 0)),
                   pl.BlockSpec((1, 2, n_chunks, n_heads, c, c), lambda bi, i: (bi, 0, i, 0, 0, 0)),
                   pl.BlockSpec((1, 2, n_chunks, n_heads, HEAD_DIM), lambda bi, i: (bi, 0, i, 0, 0))],
        out_shape=[jax.ShapeDtypeStruct((b, 2, n, 2 * c, w), BF16),
                   jax.ShapeDtypeStruct((b, 2, t, w), F32),
                   jax.ShapeDtypeStruct((b, 2, n, n_heads, HEAD_DIM, c), BF16),
                   jax.ShapeDtypeStruct((b, 2, n, n_heads, c, c), BF16),
                   jax.ShapeDtypeStruct((b, 2, n, n_heads, HEAD_DIM), F32)],
        compiler_params=_params(("arbitrary", "arbitrary")),
        name="gdn_local",
    )(qn, kn, vv, ab, *gate_consts)


def _gdn_scan_kernel(wqf_ref, wqb_ref, uf_ref, ub_ref, kdtf_ref, kdtb_ref, qkf_ref, qkb_ref, egf_ref,
                     egb_ref, s0_ref, of_ref, ob_ref, sfin_ref, s_ref, *, n_heads, bsz, n_sub):
    n = pl.program_id(0)
    c = DN_CHUNK

    @pl.when(n == 0)
    def _():
        s_ref[...] = s0_ref[...]

    refs = ((wqf_ref, uf_ref, kdtf_ref, qkf_ref, egf_ref, of_ref),
            (wqb_ref, ub_ref, kdtb_ref, qkb_ref, egb_ref, ob_ref))
    streams = [(b, d, h) for b in range(bsz) for d in range(2) for h in range(n_heads)]
    cols = lambda h: slice(h * HEAD_DIM, (h + 1) * HEAD_DIM)
    for sub in range(n_sub):
        ck = lambda d: n_sub - 1 - sub if d == 1 else sub
        tok = lambda d: slice(ck(d) * c, (ck(d) + 1) * c)
        ws = [_dot(refs[d][0][b, 0, ck(d), :, cols(h)], s_ref[b, d * n_heads + h].astype(BF16))
              for b, d, h in streams]
        v_new = [(refs[d][1][b, 0, tok(d), cols(h)] - x[:c]).astype(BF16) for (b, d, h), x in zip(streams, ws)]
        intra = [_dot(refs[d][3][b, 0, ck(d), h], v) for (b, d, h), v in zip(streams, v_new)]
        upd = [_dot(refs[d][2][b, 0, ck(d), h], v) for (b, d, h), v in zip(streams, v_new)]
        for (b, d, h), x, y, z in zip(streams, ws, intra, upd):
            refs[d][5][b, tok(d), cols(h)] = x[c:] + y
            i = d * n_heads + h
            s_ref[b, i] = s_ref[b, i] * refs[d][4][b, 0, ck(d), h:h + 1, :] + z

    @pl.when(n == pl.num_programs(0) - 1)
    def _():
        sfin_ref[...] = s_ref[...]


def _gdn_scan_call(wq, u, kdt, qk, eg, s0, *, n_heads, n_sub):
    b, _, n, _, w = wq.shape
    c = DN_CHUNK
    t = n * c
    steps = n // n_sub
    assert n % n_sub == 0

    def pair(arr, blk):
        nd = len(blk)
        fwd = pl.BlockSpec((b, 1) + blk, lambda i: (0, 0, i) + (0,) * (nd - 1))
        bwd = pl.BlockSpec((b, 1) + blk, lambda i: (0, 1, steps - 1 - i) + (0,) * (nd - 1))
        return [fwd, bwd], [arr, arr]

    specs, args = [], []
    for arr, blk in ((wq, (n_sub, 2 * c, w)), (u, (n_sub * c, w)), (kdt, (n_sub, n_heads, HEAD_DIM, c)),
                     (qk, (n_sub, n_heads, c, c)), (eg, (n_sub, n_heads, HEAD_DIM))):
        sp, ar = pair(arr, blk)
        specs += sp
        args += ar
    st_spec = pl.BlockSpec(s0.shape, lambda i: (0, 0, 0, 0))
    o_shape = jax.ShapeDtypeStruct((b, t, w), F32)
    return pl.pallas_call(
        functools.partial(_gdn_scan_kernel, n_heads=n_heads, bsz=b, n_sub=n_sub),
        grid=(steps,),
        in_specs=specs + [st_spec],
        out_specs=[pl.BlockSpec((b, n_sub * c, w), lambda i: (0, i, 0)),
                   pl.BlockSpec((b, n_sub * c, w), lambda i: (0, steps - 1 - i, 0)), st_spec],
        out_shape=[o_shape, o_shape, jax.ShapeDtypeStruct(s0.shape, F32)],
        scratch_shapes=[pltpu.VMEM(s0.shape, F32)],
        compiler_params=_params(("arbitrary",)),
        name="gdn_scan",
    )(*args, s0)


def _gate_consts(a_log, dt_bias):
    n_heads = a_log.shape[1]
    z = jnp.zeros((2, n_heads), F32)
    lay = lambda p: jnp.concatenate([z, p.astype(F32)], axis=1).reshape(-1)
    al, dt = lay(a_log), lay(dt_bias)
    pad = lambda vec: jnp.pad(vec, (0, HEAD_DIM - vec.shape[0])).reshape(1, HEAD_DIM)
    return pad(al), pad(dt)


OUT_ROWS = 256


def _outproj_kernel(x_ref, gate_ref, ya_ref, za0_ref, za1_ref, yb_ref, zb_ref, ocf_ref, ocb_ref, zc_ref, on_ref,
                    w_ref, lng_ref, lnb_ref, o_ref, mix_ref, y_ref, *, alpha, wa, wb, n_c_heads):
    blocks = [slice(r0, r0 + OUT_ROWS) for r0 in range(0, x_ref.shape[0], OUT_ROWS)]

    def project(r):
        for i, za_ref in enumerate((za0_ref, za1_ref)):
            cs = slice(i * wa // 2, (i + 1) * wa // 2)
            mix_ref[r, cs] = (ya_ref[r, cs].astype(F32) * _silu(za_ref[r, :].astype(F32))).astype(BF16)
        mix_ref[r, wa:wa + wb] = (yb_ref[r, :].astype(F32) * _silu(zb_ref[r, :].astype(F32))).astype(BF16)
        for h in range(n_c_heads):
            sl = slice(h * HEAD_DIM, (h + 1) * HEAD_DIM)
            oc = ocf_ref[r, sl] + ocb_ref[r, sl]
            yc = oc * lax.rsqrt(jnp.mean(oc * oc, axis=-1, keepdims=True) + EPS) * on_ref[...]
            off = wa + wb + h * HEAD_DIM
            mix_ref[r, off:off + HEAD_DIM] = (yc * _silu(zc_ref[r, sl].astype(F32))).astype(BF16)
        y_ref[r, :] = _dot(mix_ref[r, :], w_ref[...])

    def finish(r):
        t = alpha * x_ref[r, :] + gate_ref[0] * y_ref[r, :]
        tc = t - jnp.mean(t, axis=-1, keepdims=True)
        scale = lax.rsqrt(jnp.mean(tc * tc, axis=-1, keepdims=True) + EPS)
        o_ref[r, :] = tc * scale * lng_ref[...] + lnb_ref[...]

    for r in blocks:
        project(r)
    for r in blocks:
        finish(r)


def _outproj_call(x2d, mod_l, ya, main, yb, ocf, ocb, o_norm, w_out, ln_g, ln_b, *, col_za, col_zb,
                  col_zc, alpha, tiles_per_batch, ctx_row, tm):
    m, d = x2d.shape
    wa, wb, wc = ya.shape[1], yb.shape[1], ocf.shape[1]
    if ctx_row is None:
        row = lambda i: i // tiles_per_batch
    else:
        row = lambda i: ctx_row
    rows = lambda w, cb=0: pl.BlockSpec((tm, w), lambda i: (i, cb))
    one = lambda w: pl.BlockSpec((1, w), lambda i: (0, 0))
    wz = wa // 2
    assert col_za % wz == 0 and col_zb % wb == 0 and col_zc % wc == 0 and tm % OUT_ROWS == 0
    return pl.pallas_call(
        functools.partial(_outproj_kernel, alpha=alpha, wa=wa, wb=wb, n_c_heads=wc // HEAD_DIM),
        grid=(m // tm,),
        in_specs=[rows(d), pl.BlockSpec((1, 1, d), lambda i: (row(i), 0, 2)),
                  rows(wa), rows(wz, col_za // wz), rows(wz, col_za // wz + 1), rows(wb), rows(wb, col_zb // wb),
                  rows(wc), rows(wc), rows(wc, col_zc // wc), one(HEAD_DIM),
                  pl.BlockSpec(w_out.shape, lambda i: (0, 0)), one(d), one(d)],
        out_specs=rows(d),
        out_shape=jax.ShapeDtypeStruct((m, d), F32),
        scratch_shapes=[pltpu.VMEM((tm, wa + wb + wc), BF16), pltpu.VMEM((tm, d), F32)],
        compiler_params=_params(("arbitrary",)),
        name="outproj",
    )(x2d, mod_l, ya, main, main, yb, main, ocf, ocb, main, o_norm.reshape(1, HEAD_DIM), w_out,
      ln_g.reshape(1, d), ln_b.reshape(1, d))


def kernel(x, c, ctx, c_ctx, w_mod, b_mod, w_in, q_norm, k_norm, rpb, conv_w, a_log, dt_bias, o_norm,
           w_out, ln_g, ln_b):
    bsz, seq, d = x.shape
    ctx_len = ctx.shape[1]
    depth = w_mod.shape[0]
    a_heads = d // (2 * HEAD_DIM)
    a_kv = a_heads // 4
    b_heads = d // (4 * HEAD_DIM)
    c_heads = d // (4 * HEAD_DIM)
    wa, wkv, wb, wc = a_heads * HEAD_DIM, a_kv * HEAD_DIM, b_heads * HEAD_DIM, c_heads * HEAD_DIM
    assert bsz < MOD_ROWS and seq % (NBR_Q_ROWS * GRID_W) == 0 and seq // GRID_W >= NBR_K_ROWS
    assert ctx_len % DN_CHUNK == 0 and 4 * c_heads <= 16
    alpha = (2 * depth) ** 0.25

    sizes = dict(qa=wa, ka=wkv, va=wkv, za=wa, qb=wb, kb=wb, vb=wb, zb=wb, qkvc=3 * wc, zc=wc)
    col, off = {}, 0
    for name, size in sizes.items():
        col[name] = off
        off += size
    n_main, n_ab = off, 4 * c_heads

    cc = jnp.concatenate([c, c_ctx[None], jnp.zeros((MOD_ROWS - bsz - 1, d), F32)], axis=0)
    mod = _mod_call(cc, w_mod, b_mod)
    tables = _rope_tables(seq)
    bias = _bias_call(rpb)

    tm_lat = 512
    tm_proj = 2048 if seq % 2048 == 0 else tm_lat
    tm_ctx = min(256, ctx_len)
    tm_proj_ctx = 1024 if (bsz * ctx_len) % 1024 == 0 else tm_ctx
    tn_proj = 512
    main_dtype = BF16
    xl = x.reshape(bsz * seq, d)
    xc = ctx.reshape(bsz * ctx_len, d)
    for l in range(depth):
        with_ctx_out = l < depth - 1
        mod_l = mod[l].reshape(MOD_ROWS, 1, 3 * d)
        w_all = w_in[l].astype(BF16)
        w_ab = jnp.pad(w_all[:, n_main:n_main + n_ab], ((0, 0), (0, HEAD_DIM - n_ab)))
        w_out_l = w_out[l].astype(BF16)
        gate_consts = _gate_consts(a_log[l], dt_bias[l])

        main_x, ab_x = _inproj_call(xc, mod_l, w_all, w_ab, n_main=n_main, tiles_per_batch=None,
                                    ctx_row=bsz, tm=tm_proj_ctx, tn=tn_proj, out_dtype=main_dtype)
        main, ab = _inproj_call(xl, mod_l, w_all, w_ab, n_main=n_main, tiles_per_batch=seq // tm_proj,
                                ctx_row=None, tm=tm_proj, tn=tn_proj, out_dtype=main_dtype)
        main3 = main.reshape(bsz, seq, n_main)
        main_x3 = main_x.reshape(bsz, ctx_len, n_main)

        qa_x, ka_x = _aprep_call(main_x, q_norm[l], k_norm[l], None, col_q=col["qa"], col_k=col["ka"],
                                 nq=a_heads, nk=a_kv, tm=tm_ctx, tiles_per_batch=ctx_len // tm_ctx)
        qa, ka = _aprep_call(main, q_norm[l], k_norm[l], tables, col_q=col["qa"], col_k=col["ka"],
                             nq=a_heads, nk=a_kv, tm=tm_lat, tiles_per_batch=seq // tm_lat)
        ka_x3 = ka_x.reshape(bsz, ctx_len, wkv)
        ya = _gqa_call(qa.reshape(bsz, seq, wa), ka_x3, main_x3, col["va"], ka.reshape(bsz, seq, wkv), main3,
                       col["va"], n_kv_heads=a_kv, group=a_heads // a_kv, tq=256, tk=1024, out_dtype=BF16)

        yb = _nbr_call(main3, main_x3, bias, layer=l, col_q=col["qb"], col_k=col["kb"], col_v=col["vb"],
                       n_heads=b_heads, out_dtype=BF16)

        def delta(main2d, ab2d, t_len, tm, s0):
            qn, kn, vv = _dnprep_call(main2d, conv_w[l], col=col["qkvc"], n_heads=c_heads, tm=tm,
                                      tiles_per_batch=t_len // tm)
            r3 = lambda a: a.reshape(bsz, t_len, a.shape[-1])
            local = _gdn_local_call(r3(qn), r3(kn), r3(vv), r3(ab2d), gate_consts, n_heads=c_heads,
                                    n_chunks=4, inv_precision=INV_PRECISION)
            return _gdn_scan_call(*local, s0, n_heads=c_heads, n_sub=2)

        s_zero = jnp.zeros((bsz, 2 * c_heads, HEAD_DIM, HEAD_DIM), F32)
        ocf_x, ocb_x, s_ctx = delta(main_x, ab_x, ctx_len, tm_ctx, s_zero)
        ocf, ocb, _ = delta(main, ab, seq, tm_lat, s_ctx)

        x_new = _outproj_call(xl, mod_l, ya.reshape(bsz * seq, wa), main, yb.reshape(bsz * seq, wb),
                              ocf.reshape(bsz * seq, wc), ocb.reshape(bsz * seq, wc), o_norm[l], w_out_l,
                              ln_g[l], ln_b[l], col_za=col["za"], col_zb=col["zb"], col_zc=col["zc"],
                              alpha=alpha, tiles_per_batch=seq // tm_lat, ctx_row=None, tm=tm_lat)

        if with_ctx_out:
            ya_x = _flash_call(qa_x.reshape(bsz, ctx_len, wa), [(ka_x3, 0, main_x3, col["va"])],
                               n_kv_heads=a_kv, group=a_heads // a_kv, q_col=0, tq=ctx_len, tk=ctx_len,
                               q_scale=None, out_dtype=F32)
            yb_x = _flash_call(main_x3, [(main_x3, col["kb"], main_x3, col["vb"])],
                               n_kv_heads=b_heads, group=1, q_col=col["qb"], tq=ctx_len, tk=ctx_len,
                               q_scale=SCORE_SCALE, out_dtype=F32)
            xc = _outproj_call(xc, mod_l, ya_x.reshape(bsz * ctx_len, wa), main_x,
                               yb_x.reshape(bsz * ctx_len, wb), ocf_x.reshape(bsz * ctx_len, wc),
                               ocb_x.reshape(bsz * ctx_len, wc), o_norm[l], w_out_l, ln_g[l], ln_b[l],
                               col_za=col["za"], col_zb=col["zb"], col_zc=col["zc"], alpha=alpha,
                               tiles_per_batch=None, ctx_row=bsz, tm=tm_ctx)
        xl = x_new
    return xl.reshape(bsz, seq, d)
```

```python
import functools

import jax
import jax.numpy as jnp
from jax import lax
from jax.experimental import pallas as pl
from jax.experimental.pallas import tpu as pltpu

F32 = jnp.float32
BF16 = jnp.bfloat16

HEAD_DIM = 128
GRID_W = 64
ROPE_THETA = 10000.0
NA_WIN_H = 8
NA_WIN_W = 16
CONV_K = 3
DN_CHUNK = 64
EPS = 1e-6
NEG = -1e30
MOD_ROWS = 8
NBR_Q_ROWS = 8
NBR_K_ROWS = 16
NBR_HALF_Q = 4
NBR_HALF_K = 12
VMEM_LIMIT = 56 * 1024 * 1024
INPROJ_VMEM_LIMIT = 60 * 1024 * 1024
LOG2E = 1.4426950408889634
SCORE_SCALE = HEAD_DIM ** -0.5 * LOG2E

HIGHEST = lax.Precision.HIGHEST


def _silu(z):
    return z * jax.nn.sigmoid(z)


def _softplus(z):
    return jnp.maximum(z, 0.0) + jnp.log(1.0 + jnp.exp(-jnp.abs(z)))


def _dot(a, b, precision=None):
    return jnp.dot(a, b, preferred_element_type=F32, precision=precision)


def _dot_nt(a, b, precision=None):
    return lax.dot_general(a, b, (((1,), (1,)), ((), ())), preferred_element_type=F32,
                           precision=precision)


def _params(sem, vmem_limit=VMEM_LIMIT):
    return pltpu.CompilerParams(dimension_semantics=sem, vmem_limit_bytes=vmem_limit)


def _mod_kernel(c_ref, w_ref, b_ref, o_ref):
    s = _silu(c_ref[...]).astype(BF16)
    o_ref[0] = _dot(s, w_ref[0].astype(BF16)) + b_ref[0]


def _mod_call(cc, w_mod, b_mod):
    depth, d, n = w_mod.shape
    tn = 768
    return pl.pallas_call(
        _mod_kernel,
        grid=(depth, n // tn),
        in_specs=[pl.BlockSpec((MOD_ROWS, d), lambda l, j: (0, 0)),
                  pl.BlockSpec((1, d, tn), lambda l, j: (l, 0, j)),
                  pl.BlockSpec((1, 1, tn), lambda l, j: (l, 0, j))],
        out_specs=pl.BlockSpec((1, MOD_ROWS, tn), lambda l, j: (l, 0, j)),
        out_shape=jax.ShapeDtypeStruct((depth, MOD_ROWS, n), F32),
        compiler_params=_params(("arbitrary", "arbitrary")),
        name="mod",
    )(cc, w_mod, b_mod.reshape(depth, 1, n))


LN_ROWS = 256


def _inproj_kernel(x_ref, shift_ref, scale_ref, w_ref, wab_ref, o_ref, oab_ref, h_ref):
    first = pl.program_id(1) == 0

    @pl.when(first)
    def _():
        for r0 in range(0, x_ref.shape[0], LN_ROWS):
            r = slice(r0, r0 + LN_ROWS)
            x = x_ref[r, :]
            xc = x - jnp.mean(x, axis=-1, keepdims=True)
            y = xc * lax.rsqrt(jnp.mean(xc * xc, axis=-1, keepdims=True) + EPS)
            h_ref[r, :] = (y * (1.0 + scale_ref[0]) + shift_ref[0]).astype(BF16)
            oab_ref[r, :] = _dot(h_ref[r, :], wab_ref[...])
            o_ref[r, :] = _dot(h_ref[r, :], w_ref[...]).astype(o_ref.dtype)

    @pl.when(jnp.logical_not(first))
    def _():
        o_ref[...] = _dot(h_ref[...], w_ref[...]).astype(o_ref.dtype)


def _inproj_call(x2d, mod_l, w_all, w_ab, *, n_main, tiles_per_batch, ctx_row, tm, tn, out_dtype):
    m, d = x2d.shape
    n = n_main
    assert n % tn == 0 and tm % LN_ROWS == 0
    if ctx_row is None:
        row = lambda i: i // tiles_per_batch
    else:
        row = lambda i: ctx_row
    return pl.pallas_call(
        _inproj_kernel,
        grid=(m // tm, n // tn),
        in_specs=[pl.BlockSpec((tm, d), lambda i, j: (i, 0)),
                  pl.BlockSpec((1, 1, d), lambda i, j: (row(i), 0, 0)),
                  pl.BlockSpec((1, 1, d), lambda i, j: (row(i), 0, 1)),
                  pl.BlockSpec((d, tn), lambda i, j: (0, j)),
                  pl.BlockSpec((d, HEAD_DIM), lambda i, j: (0, 0))],
        out_specs=[pl.BlockSpec((tm, tn), lambda i, j: (i, j)),
                   pl.BlockSpec((tm, HEAD_DIM), lambda i, j: (i, 0))],
        out_shape=[jax.ShapeDtypeStruct((m, n), out_dtype),
                   jax.ShapeDtypeStruct((m, HEAD_DIM), F32)],
        scratch_shapes=[pltpu.VMEM((tm, d), BF16)],
        compiler_params=_params(("arbitrary", "arbitrary"), INPROJ_VMEM_LIMIT),
        name="inproj",
    )(x2d, mod_l, mod_l, w_all, w_ab)


def _swap_quarters(x):
    lane = lax.broadcasted_iota(jnp.int32, x.shape, 1)
    fwd = pltpu.roll(x, 3 * HEAD_DIM // 4, axis=1)
    bwd = pltpu.roll(x, HEAD_DIM // 4, axis=1)
    return jnp.where((lane % (HEAD_DIM // 2)) < HEAD_DIM // 4, fwd, bwd)


def _aprep_kernel(*refs, nq, nk, rope):
    if rope:
        q_ref, k_ref, qw_ref, kw_ref, cos_ref, sin_ref, qo_ref, ko_ref = refs
        cos, sin = cos_ref[...], sin_ref[...]
    else:
        q_ref, k_ref, qw_ref, kw_ref, qo_ref, ko_ref = refs

    ones = jnp.ones((2 * HEAD_DIM, HEAD_DIM), BF16)

    def norm(x, w):
        sq = x * x
        hi = sq.astype(BF16)
        lo = (sq - hi.astype(F32)).astype(BF16)
        ss = _dot(jnp.concatenate([hi, lo], axis=1), ones)
        y = x * lax.rsqrt(ss * (1.0 / HEAD_DIM) + EPS) * w
        if rope:
            y = y * cos + _swap_quarters(y) * sin
        return y

    for h in range(nq):
        sl = slice(h * HEAD_DIM, (h + 1) * HEAD_DIM)
        y = norm(q_ref[:, sl].astype(F32), qw_ref[...])
        qo_ref[:, sl] = (y * SCORE_SCALE).astype(qo_ref.dtype)
    for h in range(nk):
        sl = slice(h * HEAD_DIM, (h + 1) * HEAD_DIM)
        ko_ref[:, sl] = norm(k_ref[:, sl].astype(F32), kw_ref[...]).astype(ko_ref.dtype)


def _aprep_call(main, q_norm, k_norm, tables, *, col_q, col_k, nq, nk, tm, tiles_per_batch):
    m = main.shape[0]
    wq, wk = nq * HEAD_DIM, nk * HEAD_DIM
    in_specs = [pl.BlockSpec((tm, wq), lambda i: (i, col_q // wq)),
                pl.BlockSpec((tm, wk), lambda i: (i, col_k // wk)),
                pl.BlockSpec((1, HEAD_DIM), lambda i: (0, 0)),
                pl.BlockSpec((1, HEAD_DIM), lambda i: (0, 0))]
    args = [main, main, q_norm.reshape(1, HEAD_DIM), k_norm.reshape(1, HEAD_DIM)]
    if tables is not None:
        in_specs += [pl.BlockSpec((tm, HEAD_DIM), lambda i: (i % tiles_per_batch, 0))] * 2
        args += list(tables)
    return pl.pallas_call(
        functools.partial(_aprep_kernel, nq=nq, nk=nk, rope=tables is not None),
        grid=(m // tm,),
        in_specs=in_specs,
        out_specs=[pl.BlockSpec((tm, wq), lambda i: (i, 0)),
                   pl.BlockSpec((tm, wk), lambda i: (i, 0))],
        out_shape=[jax.ShapeDtypeStruct((m, wq), BF16), jax.ShapeDtypeStruct((m, wk), BF16)],
        compiler_params=_params(("arbitrary",)),
        name="aprep",
    )(*args)


def _rope_tables(n_tokens):
    t = jnp.arange(n_tokens, dtype=jnp.int32)
    row = (t // GRID_W).astype(F32)
    col = (t % GRID_W).astype(F32)
    half = HEAD_DIM // 2
    inv_freq = ROPE_THETA ** (-jnp.arange(0, half, 2, dtype=F32) / half)
    ar, ac = row[:, None] * inv_freq, col[:, None] * inv_freq
    cos = jnp.concatenate([jnp.cos(ar), jnp.cos(ar), jnp.cos(ac), jnp.cos(ac)], -1)
    sin = jnp.concatenate([-jnp.sin(ar), jnp.sin(ar), -jnp.sin(ac), jnp.sin(ac)], -1)
    return cos, sin


def _ctx_attn_kernel(q_ref, k_ref, v_ref, o_ref, *, group, q_scale):
    tq = q_ref.shape[1]
    q = jnp.concatenate([q_ref[0, :, g * HEAD_DIM:(g + 1) * HEAD_DIM] for g in range(group)], axis=0)
    if q_scale is not None:
        q = q.astype(F32) * q_scale
    s = _dot_nt(q.astype(BF16), k_ref[0].astype(BF16))
    p = jnp.exp2(s - jnp.max(s, axis=-1, keepdims=True))
    o = _dot(p.astype(BF16), v_ref[0].astype(BF16)) / jnp.sum(p, axis=-1, keepdims=True)
    for g in range(group):
        o_ref[0, :, g * HEAD_DIM:(g + 1) * HEAD_DIM] = o[g * tq:(g + 1) * tq].astype(o_ref.dtype)


SOFTMAX_UNIT = 16


def _gqa_kernel(q_ref, kc_ref, vc_ref, kl_ref, vl_ref, o_ref, q4_ref, s0_ref, s1_ref, sc_ref, p_ref,
                m_ref, l_ref, a_ref, acc_ref, *, group, tq, tk):
    rows = group * tq
    half = rows // 2
    n_lat = kl_ref.shape[1] // tk
    for g in range(group):
        q4_ref[g * tq:(g + 1) * tq, :] = q_ref[0, :, g * HEAD_DIM:(g + 1) * HEAD_DIM]
    m_ref[...] = jnp.full(m_ref.shape, -jnp.inf, F32)
    l_ref[...] = jnp.zeros(l_ref.shape, F32)
    acc_ref[...] = jnp.zeros(acc_ref.shape, F32)
    halves = (slice(0, half), slice(half, rows))

    def issue(dst_ref, k, rs):
        dst_ref[rs, :] = _dot_nt(q4_ref[rs, :], k)

    def softmax(src_ref, rs, width):
        for r0 in range(rs.start, rs.stop, SOFTMAX_UNIT):
            r = slice(r0, r0 + SOFTMAX_UNIT)
            cols = [slice(j, j + HEAD_DIM) for j in range(0, width, HEAD_DIM)]
            s = [src_ref[r, c] for c in cols]
            mx = functools.reduce(jnp.maximum, s)
            m_old = m_ref[r, :]
            m_new = jnp.maximum(m_old, jnp.max(mx, axis=-1, keepdims=True))
            alpha = jnp.exp2(m_old - m_new)
            p = [jnp.exp2(x - m_new) for x in s]
            for c, x in zip(cols, p):
                p_ref[r, c] = x.astype(BF16)
            m_ref[r, :] = m_new
            a_ref[r, :] = alpha
            l_ref[r, :] = alpha * l_ref[r, :] + functools.reduce(jnp.add, p)

    def accumulate(rs, width, v):
        pv = _dot(p_ref[rs, 0:width], v)
        acc_ref[rs, :] = acc_ref[rs, :] * a_ref[rs, :] + pv

    def chunk_step(cur_ref, width, v, nxt_ref, k_next):
        for rs in halves:
            if nxt_ref is not None:
                issue(nxt_ref, k_next, rs)
            softmax(cur_ref, rs, width)
            accumulate(rs, width, v)

    lat = lambda ref, i: ref[0, pl.ds(pl.multiple_of(i * tk, tk), tk), :]
    for rs in halves:
        issue(s0_ref, lat(kl_ref, 0), rs)

    def pair(j, carry):
        i = 2 * j
        chunk_step(s0_ref, tk, lat(vl_ref, i), s1_ref, lat(kl_ref, i + 1))
        chunk_step(s1_ref, tk, lat(vl_ref, i + 1), s0_ref, lat(kl_ref, i + 2))
        return carry

    lax.fori_loop(0, n_lat // 2 - 1, pair, 0)
    chunk_step(s0_ref, tk, lat(vl_ref, n_lat - 2), s1_ref, lat(kl_ref, n_lat - 1))
    chunk_step(s1_ref, tk, lat(vl_ref, n_lat - 1), sc_ref, kc_ref[0])
    chunk_step(sc_ref, kc_ref.shape[1], vc_ref[0], None, None)

    o = acc_ref[...] / jnp.sum(l_ref[...], axis=-1, keepdims=True)
    for g in range(group):
        o_ref[0, :, g * HEAD_DIM:(g + 1) * HEAD_DIM] = o[g * tq:(g + 1) * tq].astype(o_ref.dtype)


def _gqa_call(q, k_ctx, v_ctx, v_ctx_col, k_lat, v_lat, v_lat_col, *, n_kv_heads, group, tq, tk, out_dtype):
    b, s, _ = q.shape
    ctx_len = k_ctx.shape[1]
    wq = group * HEAD_DIM
    rows = group * tq
    assert s % (2 * tk) == 0 and s // tk >= 2 and ctx_len % HEAD_DIM == 0 and ctx_len <= tk
    assert rows % (2 * SOFTMAX_UNIT) == 0
    cvc, cvl = v_ctx_col // HEAD_DIM, v_lat_col // HEAD_DIM
    kv_spec = lambda t, c: pl.BlockSpec((1, t, HEAD_DIM), lambda bi, g, qi: (bi, 0, c + g))
    stat = pltpu.VMEM((rows, HEAD_DIM), F32)
    return pl.pallas_call(
        functools.partial(_gqa_kernel, group=group, tq=tq, tk=tk),
        grid=(b, n_kv_heads, s // tq),
        in_specs=[pl.BlockSpec((1, tq, wq), lambda bi, g, qi: (bi, qi, g)),
                  kv_spec(ctx_len, 0), kv_spec(ctx_len, cvc), kv_spec(s, 0), kv_spec(s, cvl)],
        out_specs=pl.BlockSpec((1, tq, wq), lambda bi, g, qi: (bi, qi, g)),
        out_shape=jax.ShapeDtypeStruct((b, s, n_kv_heads * wq), out_dtype),
        scratch_shapes=[pltpu.VMEM((rows, HEAD_DIM), BF16), pltpu.VMEM((rows, tk), F32),
                        pltpu.VMEM((rows, tk), F32), pltpu.VMEM((rows, ctx_len), F32),
                        pltpu.VMEM((rows, tk), BF16), stat, stat, stat, stat],
        compiler_params=_params(("arbitrary", "arbitrary", "arbitrary")),
        name="gqa",
    )(q, k_ctx, v_ctx, k_lat, v_lat)


def _ctx_attn_call(q, q_col, k, k_col, v, v_col, *, n_kv_heads, group, q_scale):
    b, t, _ = q.shape
    wq = group * HEAD_DIM
    kv_spec = lambda c: pl.BlockSpec((1, t, HEAD_DIM), lambda bi, g: (bi, 0, c // HEAD_DIM + g))
    return pl.pallas_call(
        functools.partial(_ctx_attn_kernel, group=group, q_scale=q_scale),
        grid=(b, n_kv_heads),
        in_specs=[pl.BlockSpec((1, t, wq), lambda bi, g: (bi, 0, q_col // wq + g)), kv_spec(k_col), kv_spec(v_col)],
        out_specs=pl.BlockSpec((1, t, wq), lambda bi, g: (bi, 0, g)),
        out_shape=jax.ShapeDtypeStruct((b, t, n_kv_heads * wq), F32),
        compiler_params=_params(("arbitrary", "arbitrary")),
        name="ctx_attn",
    )(q, k, v)


def _nbr_window(cls, qr, kr):
    if cls == 0:
        lo, dr = max(qr - NA_WIN_H // 2, 0), kr - qr
    elif cls == 1:
        lo, dr = qr, kr - NA_WIN_H // 2 - qr
    else:
        lo, dr = min(NA_WIN_H // 2 + qr, NBR_K_ROWS - NA_WIN_H), kr - NBR_Q_ROWS - qr
    if lo <= kr < lo + NA_WIN_H:
        assert -NA_WIN_H < dr < NA_WIN_H
        return dr + NA_WIN_H - 1
    return None


def _nbr_half_offset(cls, hf):
    return (0, (0, NBR_HALF_Q)[hf], NBR_HALF_Q)[cls]


def _bias_kernel(rpb_ref, o_ref, *, n_heads):
    nrow, ncol = 2 * NA_WIN_H - 1, 2 * NA_WIN_W - 1
    base = (pl.program_id(0) * n_heads + pl.program_id(1)) * (nrow * ncol)
    shape = (GRID_W, 2 * GRID_W)
    c = lax.broadcasted_iota(jnp.int32, shape, 0)
    lane = lax.broadcasted_iota(jnp.int32, shape, 1)
    right = lane >= GRID_W
    kc = jnp.where(right, lane - GRID_W, lane)
    c0 = jnp.clip(c - NA_WIN_W // 2, 0, GRID_W - NA_WIN_W)
    col_ok = (kc >= c0) & (kc < c0 + NA_WIN_W)
    idx = kc - c + (NA_WIN_W - 1)
    cache = {}

    def tile(a_left, a_right):
        key = (a_left, a_right)
        if key not in cache:
            t = jnp.full(shape, NEG, F32)
            if key != (None, None):
                for j in range(ncol):
                    sl = NEG if a_left is None else rpb_ref[base + a_left * ncol + j]
                    sr = NEG if a_right is None else rpb_ref[base + a_right * ncol + j]
                    t = jnp.where(idx == j, jnp.where(right, sr, sl), t)
                t = jnp.where(col_ok, t * LOG2E, NEG)
            cache[key] = t
        return cache[key]

    for cls in range(3):
        for hf in range(2):
            off = _nbr_half_offset(cls, hf)
            assert all(off <= kr < off + NBR_HALF_K for q in range(hf * NBR_HALF_Q, (hf + 1) * NBR_HALF_Q)
                       for kr in range(NBR_K_ROWS) if _nbr_window(cls, q, kr) is not None)
            for qr in range(NBR_HALF_Q):
                for p in range(NBR_HALF_K // 2):
                    kr = _nbr_half_offset(cls, hf) + 2 * p
                    t = tile(_nbr_window(cls, hf * NBR_HALF_Q + qr, kr),
                             _nbr_window(cls, hf * NBR_HALF_Q + qr, kr + 1))
                    o_ref[0, 0, cls, hf, qr * GRID_W:(qr + 1) * GRID_W, p * 2 * GRID_W:(p + 1) * 2 * GRID_W] = t


def _bias_call(rpb):
    depth, n_heads = rpb.shape[:2]
    shape = (depth, n_heads, 3, 2, NBR_HALF_Q * GRID_W, NBR_HALF_K * GRID_W)
    return pl.pallas_call(
        functools.partial(_bias_kernel, n_heads=n_heads),
        grid=(depth, n_heads),
        in_specs=[pl.BlockSpec(memory_space=pltpu.SMEM)],
        out_specs=pl.BlockSpec((1, 1) + shape[2:], lambda l, h: (l, h, 0, 0, 0, 0)),
        out_shape=jax.ShapeDtypeStruct(shape, F32),
        compiler_params=_params(("arbitrary", "arbitrary")),
        name="nbr_bias",
    )(rpb.reshape(-1))


def _nbr_kernel(q_ref, k_ref, v_ref, kc_ref, vc_ref, bias_ref, o_ref, *, rows):
    g = pl.program_id(2)
    n_steps = rows // NBR_Q_ROWS
    start_row = jnp.clip(g * NBR_Q_ROWS - NA_WIN_H // 2, 0, rows - NBR_K_ROWS)
    start = pl.multiple_of(start_row * GRID_W, (NA_WIN_H // 2) * GRID_W)
    cls = jnp.where(g == 0, 0, jnp.where(g == n_steps - 1, 2, 1))
    nq, nk = NBR_HALF_Q * GRID_W, NBR_HALF_K * GRID_W
    kc, vc = kc_ref[0].astype(BF16), vc_ref[0].astype(BF16)
    scores = []
    for hf in range(2):
        off = jnp.where(cls == 0, 0, jnp.where(cls == 1, hf * NBR_HALF_Q, NBR_HALF_Q))
        st = pl.multiple_of(start + off * GRID_W, NBR_HALF_Q * GRID_W)
        q = (q_ref[0, hf * nq:(hf + 1) * nq, :].astype(F32) * SCORE_SCALE).astype(BF16)
        s_win = _dot_nt(q, k_ref[0, pl.ds(st, nk), :].astype(BF16)) + bias_ref[0, 0, cls, hf]
        scores.append((st, s_win, _dot_nt(q, kc)))
    for hf, (st, s_win, s_ctx) in enumerate(scores):
        m = jnp.maximum(jnp.max(s_win, axis=-1, keepdims=True), jnp.max(s_ctx, axis=-1, keepdims=True))
        p_win = jnp.exp2(s_win - m)
        p_ctx = jnp.exp2(s_ctx - m)
        l = jnp.sum(p_win, axis=-1, keepdims=True) + jnp.sum(p_ctx, axis=-1, keepdims=True)
        o = _dot(p_win.astype(BF16), v_ref[0, pl.ds(st, nk), :].astype(BF16)) + _dot(p_ctx.astype(BF16), vc)
        o_ref[0, hf * nq:(hf + 1) * nq, :] = (o / l).astype(o_ref.dtype)


def _nbr_call(main, main_ctx, bias, *, layer, col_q, col_k, col_v, n_heads, out_dtype):
    b, s, _ = main.shape
    ctx_len = main_ctx.shape[1]
    rows = s // GRID_W
    tq = NBR_Q_ROWS * GRID_W
    cq, ck, cv = col_q // HEAD_DIM, col_k // HEAD_DIM, col_v // HEAD_DIM
    return pl.pallas_call(
        functools.partial(_nbr_kernel, rows=rows),
        grid=(n_heads, b, rows // NBR_Q_ROWS),
        in_specs=[pl.BlockSpec((1, tq, HEAD_DIM), lambda h, bi, g: (bi, g, cq + h)),
                  pl.BlockSpec((1, s, HEAD_DIM), lambda h, bi, g: (bi, 0, ck + h)),
                  pl.BlockSpec((1, s, HEAD_DIM), lambda h, bi, g: (bi, 0, cv + h)),
                  pl.BlockSpec((1, ctx_len, HEAD_DIM), lambda h, bi, g: (bi, 0, ck + h)),
                  pl.BlockSpec((1, ctx_len, HEAD_DIM), lambda h, bi, g: (bi, 0, cv + h)),
                  pl.BlockSpec((1, 1) + bias.shape[2:], lambda h, bi, g: (layer, h, 0, 0, 0, 0))],
        out_specs=pl.BlockSpec((1, tq, HEAD_DIM), lambda h, bi, g: (bi, g, h)),
        out_shape=jax.ShapeDtypeStruct((b, s, n_heads * HEAD_DIM), out_dtype),
        compiler_params=_params(("arbitrary", "arbitrary", "arbitrary")),
        name="nbr_attn",
    )(main, main, main, main_ctx, main_ctx, bias)


def _dnprep_kernel(x_ref, prev_ref, next_ref, w_ref, q_ref, k_ref, v_ref, *, tiles_per_batch, n_heads):
    i = pl.program_id(0) % tiles_per_batch
    x = x_ref[...].astype(F32)
    tm = x.shape[0]
    row = lax.broadcasted_iota(jnp.int32, x.shape, 0)
    halo = prev_ref.shape[0]
    prev_row = jnp.where(i == 0, 0.0, prev_ref[halo - 1:halo, :].astype(F32))
    next_row = jnp.where(i == tiles_per_batch - 1, 0.0, next_ref[0:1, :].astype(F32))
    x_prev = jnp.where(row == 0, prev_row, pltpu.roll(x, 1, axis=0))
    x_next = jnp.where(row == tm - 1, next_row, pltpu.roll(x, tm - 1, axis=0))
    y = _silu(w_ref[0:1, :] * x_prev + w_ref[1:2, :] * x + w_ref[2:3, :] * x_next)
    w = n_heads * HEAD_DIM
    for h in range(n_heads):
        sl = slice(h * HEAD_DIM, (h + 1) * HEAD_DIM)
        qh = y[:, h * HEAD_DIM:(h + 1) * HEAD_DIM]
        kh = y[:, w + h * HEAD_DIM:w + (h + 1) * HEAD_DIM]
        qn = qh * lax.rsqrt(jnp.sum(qh * qh, axis=-1, keepdims=True) + EPS) * HEAD_DIM ** -0.5
        q_ref[:, sl] = qn.astype(q_ref.dtype)
        k_ref[:, sl] = (kh * lax.rsqrt(jnp.sum(kh * kh, axis=-1, keepdims=True) + EPS)).astype(k_ref.dtype)
    v_ref[...] = y[:, 2 * w:].astype(v_ref.dtype)


def _dnprep_call(main, conv_w, *, col, n_heads, tm, tiles_per_batch):
    m = main.shape[0]
    w3 = 3 * n_heads * HEAD_DIM
    halo = 16
    hb = tm // halo
    n_halo = m // halo
    out = jax.ShapeDtypeStruct((m, n_heads * HEAD_DIM), BF16)
    o_spec = pl.BlockSpec((tm, n_heads * HEAD_DIM), lambda i: (i, 0))
    return pl.pallas_call(
        functools.partial(_dnprep_kernel, tiles_per_batch=tiles_per_batch, n_heads=n_heads),
        grid=(m // tm,),
        in_specs=[pl.BlockSpec((tm, w3), lambda i: (i, col // w3)),
                  pl.BlockSpec((halo, w3), lambda i: (jnp.maximum(i * hb - 1, 0), col // w3)),
                  pl.BlockSpec((halo, w3), lambda i: (jnp.minimum((i + 1) * hb, n_halo - 1), col // w3)),
                  pl.BlockSpec((CONV_K, w3), lambda i: (0, 0))],
        out_specs=[o_spec, o_spec, o_spec],
        out_shape=[out, out, out],
        compiler_params=_params(("arbitrary",)),
        name="dn_prep",
    )(main, main, main, conv_w)


def _tri_masks(n, upper):
    i = lax.broadcasted_iota(jnp.int32, (n, n), 0)
    j = lax.broadcasted_iota(jnp.int32, (n, n), 1)
    incl = (j >= i) if upper else (j <= i)
    strict = (j > i) if upper else (j < i)
    eye = (i == j).astype(F32)
    base = 8
    same_base = (i // base) == (j // base)
    levels = []
    s = base
    while s < n:
        levels.append(((i // (2 * s)) == (j // (2 * s))) & ((i // s) != (j // s)))
        s *= 2
    return incl, strict, eye, same_base, levels


def _unit_tri_inverse(lms, masks):
    mm = lambda a, b: _dot(a.astype(BF16), b.astype(BF16))
    eyes = [m[2] for m in masks]
    l1 = [jnp.where(m[3], lm, 0.0) for lm, m in zip(lms, masks)]
    l2 = [mm(a, a) for a in l1]
    l4 = [mm(a, a) for a in l2]
    t = [mm(e - a, e + b) for e, a, b in zip(eyes, l1, l2)]
    t = [mm(a, e + b) for e, a, b in zip(eyes, t, l4)]
    for lvl in range(len(masks[0][4])):
        cs = [jnp.where(m[4][lvl], lm, 0.0) for lm, m in zip(lms, masks)]
        tc = [mm(a, b) for a, b in zip(t, cs)]
        t = [a - mm(b, a) for a, b in zip(t, tc)]
    return t


def _gdn_local_kernel(q_ref, k_ref, v_ref, ab_ref, alr_ref, dtr_ref,
                      wq_ref, u_ref, kdt_ref, qk_ref, eg_ref, *, n_heads, n_chunks):
    c = DN_CHUNK
    masks = [_tri_masks(c, upper) for upper in (False, True)]
    gates = {}
    for d in range(2):
        cum = masks[d][0].astype(F32)
        for ci in range(n_chunks):
            ab = ab_ref[0, ci * c:(ci + 1) * c, :]
            g_tok = -jnp.exp(alr_ref[...]) * _softplus(ab + dtr_ref[...])
            gc_tok = _dot(cum, g_tok, HIGHEST)
            gates[ci, d] = (jax.nn.sigmoid(ab), gc_tok, gc_tok.T)

    streams = [(ci, d, h) for ci in range(n_chunks) for d in range(2) for h in range(n_heads)]
    pre = []
    for ci, d, h in streams:
        incl, strict = masks[d][0], masks[d][1]
        rows = slice(ci * c, (ci + 1) * c)
        sl = slice(h * HEAD_DIM, (h + 1) * HEAD_DIM)
        cb, cg = d * 2 * n_heads + h, d * 2 * n_heads + n_heads + h
        beta_tok, gc_tok, gc_rows = gates[ci, d]
        q, k, v = q_ref[0, rows, sl], k_ref[0, rows, sl], v_ref[0, rows, sl]
        beta_c = beta_tok[:, cb:cb + 1]
        gc_c = gc_tok[:, cg:cg + 1]
        last = 0 if d == 1 else c - 1
        g_last = gc_c[last:last + 1, :]
        decay = jnp.where(incl, jnp.exp(jnp.where(incl, gc_c - gc_rows[cg:cg + 1, :], 0.0)), 0.0)
        kbeta = k * beta_c
        e_c = jnp.exp(gc_c)
        wq_ref[0, d, ci, c:2 * c, sl] = (q * e_c).astype(wq_ref.dtype)
        kdt_ref[0, d, ci, h] = (k * jnp.exp(g_last - gc_c)).T.astype(kdt_ref.dtype)
        eg_ref[0, d, ci, h:h + 1, :] = jnp.broadcast_to(jnp.exp(g_last), (1, HEAD_DIM))
        lhs = jnp.concatenate([kbeta, q], axis=0).astype(BF16)
        rhs = jnp.concatenate([v * beta_c, kbeta * e_c], axis=1).astype(BF16)
        pre.append((lhs, k.astype(BF16), rhs, decay))
    kq = [_dot_nt(lhs, kb) for lhs, kb, _, _ in pre]
    lms = []
    for (ci, d, h), x, (_, _, _, decay) in zip(streams, kq, pre):
        qk_ref[0, d, ci, h] = (x[c:] * decay).astype(qk_ref.dtype)
        lms.append(jnp.where(masks[d][1], x[:c] * decay, 0.0))
    ts = _unit_tri_inverse(lms, [masks[d] for _, d, _ in streams])
    uws = [_dot(t.astype(BF16), rhs) for t, (_, _, rhs, _) in zip(ts, pre)]
    for (ci, d, h), uw in zip(streams, uws):
        sl = slice(h * HEAD_DIM, (h + 1) * HEAD_DIM)
        u_ref[0, d, ci * c:(ci + 1) * c, sl] = uw[:, :HEAD_DIM]
        wq_ref[0, d, ci, 0:c, sl] = uw[:, HEAD_DIM:].astype(wq_ref.dtype)


def _gdn_local_call(qn, kn, vv, ab, gate_consts, *, n_heads, n_chunks):
    b, t, w = qn.shape
    c = DN_CHUNK
    n = t // c
    tm = n_chunks * c
    tok = pl.BlockSpec((1, tm, w), lambda bi, i: (bi, i, 0))
    const = lambda a: pl.BlockSpec(a.shape, lambda bi, i: (0,) * a.ndim)
    return pl.pallas_call(
        functools.partial(_gdn_local_kernel, n_heads=n_heads, n_chunks=n_chunks),
        grid=(b, n // n_chunks),
        in_specs=[tok, tok, tok, pl.BlockSpec((1, tm, HEAD_DIM), lambda bi, i: (bi, i, 0))]
                 + [const(a) for a in gate_consts],
        out_specs=[pl.BlockSpec((1, 2, n_chunks, 2 * c, w), lambda bi, i: (bi, 0, i, 0, 0)),
                   pl.BlockSpec((1, 2, tm, w), lambda bi, i: (bi, 0, i, 0)),
                   pl.BlockSpec((1, 2, n_chunks, n_heads, HEAD_DIM, c), lambda bi, i: (bi, 0, i, 0, 0, 0)),
                   pl.BlockSpec((1, 2, n_chunks, n_heads, c, c), lambda bi, i: (bi, 0, i, 0, 0, 0)),
                   pl.BlockSpec((1, 2, n_chunks, n_heads, HEAD_DIM), lambda bi, i: (bi, 0, i, 0, 0))],
        out_shape=[jax.ShapeDtypeStruct((b, 2, n, 2 * c, w), BF16),
                   jax.ShapeDtypeStruct((b, 2, t, w), F32),
                   jax.ShapeDtypeStruct((b, 2, n, n_heads, HEAD_DIM, c), BF16),
                   jax.ShapeDtypeStruct((b, 2, n, n_heads, c, c), BF16),
                   jax.ShapeDtypeStruct((b, 2, n, n_heads, HEAD_DIM), F32)],
        compiler_params=_params(("arbitrary", "arbitrary")),
        name="gdn_local",
    )(qn, kn, vv, ab, *gate_consts)


def _gdn_scan_kernel(wqf_ref, wqb_ref, uf_ref, ub_ref, kdtf_ref, kdtb_ref, qkf_ref, qkb_ref, egf_ref,
                     egb_ref, s0_ref, of_ref, ob_ref, sfin_ref, s_ref, *, n_heads, bsz, n_sub):
    n = pl.program_id(0)
    c = DN_CHUNK

    @pl.when(n == 0)
    def _():
        s_ref[...] = s0_ref[...]

    refs = ((wqf_ref, uf_ref, kdtf_ref, qkf_ref, egf_ref, of_ref),
            (wqb_ref, ub_ref, kdtb_ref, qkb_ref, egb_ref, ob_ref))
    streams = [(b, d, h) for b in range(bsz) for d in range(2) for h in range(n_heads)]
    cols = lambda h: slice(h * HEAD_DIM, (h + 1) * HEAD_DIM)
    for sub in range(n_sub):
        ck = lambda d: n_sub - 1 - sub if d == 1 else sub
        tok = lambda d: slice(ck(d) * c, (ck(d) + 1) * c)
        ws = [_dot(refs[d][0][b, 0, ck(d), :, cols(h)], s_ref[b, d * n_heads + h].astype(BF16))
              for b, d, h in streams]
        v_new = [(refs[d][1][b, 0, tok(d), cols(h)] - x[:c]).astype(BF16) for (b, d, h), x in zip(streams, ws)]
        intra = [_dot(refs[d][3][b, 0, ck(d), h], v) for (b, d, h), v in zip(streams, v_new)]
        upd = [_dot(refs[d][2][b, 0, ck(d), h], v) for (b, d, h), v in zip(streams, v_new)]
        for (b, d, h), x, y, z in zip(streams, ws, intra, upd):
            refs[d][5][b, tok(d), cols(h)] = x[c:] + y
            i = d * n_heads + h
            s_ref[b, i] = s_ref[b, i] * refs[d][4][b, 0, ck(d), h:h + 1, :] + z

    @pl.when(n == pl.num_programs(0) - 1)
    def _():
        sfin_ref[...] = s_ref[...]


def _gdn_scan_call(wq, u, kdt, qk, eg, s0, *, n_heads, n_sub):
    b, _, n, _, w = wq.shape
    c = DN_CHUNK
    t = n * c
    steps = n // n_sub
    assert n % n_sub == 0

    def pair(arr, blk):
        nd = len(blk)
        fwd = pl.BlockSpec((b, 1) + blk, lambda i: (0, 0, i) + (0,) * (nd - 1))
        bwd = pl.BlockSpec((b, 1) + blk, lambda i: (0, 1, steps - 1 - i) + (0,) * (nd - 1))
        return [fwd, bwd], [arr, arr]

    specs, args = [], []
    for arr, blk in ((wq, (n_sub, 2 * c, w)), (u, (n_sub * c, w)), (kdt, (n_sub, n_heads, HEAD_DIM, c)),
                     (qk, (n_sub, n_heads, c, c)), (eg, (n_sub, n_heads, HEAD_DIM))):
        sp, ar = pair(arr, blk)
        specs += sp
        args += ar
    st_spec = pl.BlockSpec(s0.shape, lambda i: (0, 0, 0, 0))
    o_shape = jax.ShapeDtypeStruct((b, t, w), F32)
    return pl.pallas_call(
        functools.partial(_gdn_scan_kernel, n_heads=n_heads, bsz=b, n_sub=n_sub),
        grid=(steps,),
        in_specs=specs + [st_spec],
        out_specs=[pl.BlockSpec((b, n_sub * c, w), lambda i: (0, i, 0)),
                   pl.BlockSpec((b, n_sub * c, w), lambda i: (0, steps - 1 - i, 0)), st_spec],
        out_shape=[o_shape, o_shape, jax.ShapeDtypeStruct(s0.shape, F32)],
        scratch_shapes=[pltpu.VMEM(s0.shape, F32)],
        compiler_params=_params(("arbitrary",)),
        name="gdn_scan",
    )(*args, s0)


def _gate_consts(a_log, dt_bias):
    n_heads = a_log.shape[1]
    z = jnp.zeros((2, n_heads), F32)
    lay = lambda p: jnp.concatenate([z, p.astype(F32)], axis=1).reshape(-1)
    al, dt = lay(a_log), lay(dt_bias)
    pad = lambda vec: jnp.pad(vec, (0, HEAD_DIM - vec.shape[0])).reshape(1, HEAD_DIM)
    return pad(al), pad(dt)


OUT_ROWS = 256


def _outproj_kernel(x_ref, gate_ref, ya_ref, za0_ref, za1_ref, yb_ref, zb_ref, ocf_ref, ocb_ref, zc_ref, on_ref,
                    w_ref, lng_ref, lnb_ref, o_ref, mix_ref, y_ref, *, alpha, wa, wb, n_c_heads):
    blocks = [slice(r0, r0 + OUT_ROWS) for r0 in range(0, x_ref.shape[0], OUT_ROWS)]

    def project(r):
        for i, za_ref in enumerate((za0_ref, za1_ref)):
            cs = slice(i * wa // 2, (i + 1) * wa // 2)
            mix_ref[r, cs] = (ya_ref[r, cs].astype(F32) * _silu(za_ref[r, :].astype(F32))).astype(BF16)
        mix_ref[r, wa:wa + wb] = (yb_ref[r, :].astype(F32) * _silu(zb_ref[r, :].astype(F32))).astype(BF16)
        for h in range(n_c_heads):
            sl = slice(h * HEAD_DIM, (h + 1) * HEAD_DIM)
            oc = ocf_ref[r, sl] + ocb_ref[r, sl]
            yc = oc * lax.rsqrt(jnp.mean(oc * oc, axis=-1, keepdims=True) + EPS) * on_ref[...]
            off = wa + wb + h * HEAD_DIM
            mix_ref[r, off:off + HEAD_DIM] = (yc * _silu(zc_ref[r, sl].astype(F32))).astype(BF16)
        y_ref[r, :] = _dot(mix_ref[r, :], w_ref[...])

    def finish(r):
        t = alpha * x_ref[r, :] + gate_ref[0] * y_ref[r, :]
        tc = t - jnp.mean(t, axis=-1, keepdims=True)
        scale = lax.rsqrt(jnp.mean(tc * tc, axis=-1, keepdims=True) + EPS)
        o_ref[r, :] = tc * scale * lng_ref[...] + lnb_ref[...]

    for r in blocks:
        project(r)
    for r in blocks:
        finish(r)


def _outproj_call(x2d, mod_l, ya, main, yb, ocf, ocb, o_norm, w_out, ln_g, ln_b, *, col_za, col_zb,
                  col_zc, alpha, tiles_per_batch, ctx_row, tm):
    m, d = x2d.shape
    wa, wb, wc = ya.shape[1], yb.shape[1], ocf.shape[1]
    if ctx_row is None:
        row = lambda i: i // tiles_per_batch
    else:
        row = lambda i: ctx_row
    rows = lambda w, cb=0: pl.BlockSpec((tm, w), lambda i: (i, cb))
    one = lambda w: pl.BlockSpec((1, w), lambda i: (0, 0))
    wz = wa // 2
    assert col_za % wz == 0 and col_zb % wb == 0 and col_zc % wc == 0 and tm % OUT_ROWS == 0
    return pl.pallas_call(
        functools.partial(_outproj_kernel, alpha=alpha, wa=wa, wb=wb, n_c_heads=wc // HEAD_DIM),
        grid=(m // tm,),
        in_specs=[rows(d), pl.BlockSpec((1, 1, d), lambda i: (row(i), 0, 2)),
                  rows(wa), rows(wz, col_za // wz), rows(wz, col_za // wz + 1), rows(wb), rows(wb, col_zb // wb),
                  rows(wc), rows(wc), rows(wc, col_zc // wc), one(HEAD_DIM),
                  pl.BlockSpec(w_out.shape, lambda i: (0, 0)), one(d), one(d)],
        out_specs=rows(d),
        out_shape=jax.ShapeDtypeStruct((m, d), F32),
        scratch_shapes=[pltpu.VMEM((tm, wa + wb + wc), BF16), pltpu.VMEM((tm, d), F32)],
        compiler_params=_params(("arbitrary",)),
        name="outproj",
    )(x2d, mod_l, ya, main, main, yb, main, ocf, ocb, main, o_norm.reshape(1, HEAD_DIM), w_out,
      ln_g.reshape(1, d), ln_b.reshape(1, d))


def kernel(x, c, ctx, c_ctx, w_mod, b_mod, w_in, q_norm, k_norm, rpb, conv_w, a_log, dt_bias, o_norm,
           w_out, ln_g, ln_b):
    bsz, seq, d = x.shape
    ctx_len = ctx.shape[1]
    depth = w_mod.shape[0]
    a_heads = d // (2 * HEAD_DIM)
    a_kv = a_heads // 4
    b_heads = d // (4 * HEAD_DIM)
    c_heads = d // (4 * HEAD_DIM)
    wa, wkv, wb, wc = a_heads * HEAD_DIM, a_kv * HEAD_DIM, b_heads * HEAD_DIM, c_heads * HEAD_DIM
    assert bsz < MOD_ROWS and seq % (NBR_Q_ROWS * GRID_W) == 0 and seq // GRID_W >= NBR_K_ROWS
    assert ctx_len % DN_CHUNK == 0 and 4 * c_heads <= 16
    alpha = (2 * depth) ** 0.25

    sizes = dict(qa=wa, ka=wkv, va=wkv, za=wa, qb=wb, kb=wb, vb=wb, zb=wb, qkvc=3 * wc, zc=wc)
    col, off = {}, 0
    for name, size in sizes.items():
        col[name] = off
        off += size
    n_main, n_ab = off, 4 * c_heads

    cc = jnp.concatenate([c, c_ctx[None], jnp.zeros((MOD_ROWS - bsz - 1, d), F32)], axis=0)
    mod = _mod_call(cc, w_mod, b_mod)
    tables = _rope_tables(seq)
    bias = _bias_call(rpb)

    tm_lat = 512
    tm_proj = 2048 if seq % 2048 == 0 else tm_lat
    tm_ctx = min(256, ctx_len)
    tm_proj_ctx = 1024 if (bsz * ctx_len) % 1024 == 0 else tm_ctx
    tn_proj = 512
    tq_attn, tk_attn = 256, 1024
    gdn_chunks, scan_chunks = 4, 2
    main_dtype = BF16
    xl = x.reshape(bsz * seq, d)
    xc = ctx.reshape(bsz * ctx_len, d)
    for l in range(depth):
        with_ctx_out = l < depth - 1
        mod_l = mod[l].reshape(MOD_ROWS, 1, 3 * d)
        w_all = w_in[l].astype(BF16)
        w_ab = jnp.pad(w_all[:, n_main:n_main + n_ab], ((0, 0), (0, HEAD_DIM - n_ab)))
        w_out_l = w_out[l].astype(BF16)
        gate_consts = _gate_consts(a_log[l], dt_bias[l])

        main_x, ab_x = _inproj_call(xc, mod_l, w_all, w_ab, n_main=n_main, tiles_per_batch=None,
                                    ctx_row=bsz, tm=tm_proj_ctx, tn=tn_proj, out_dtype=main_dtype)
        main, ab = _inproj_call(xl, mod_l, w_all, w_ab, n_main=n_main, tiles_per_batch=seq // tm_proj,
                                ctx_row=None, tm=tm_proj, tn=tn_proj, out_dtype=main_dtype)
        main3 = main.reshape(bsz, seq, n_main)
        main_x3 = main_x.reshape(bsz, ctx_len, n_main)

        qa_x, ka_x = _aprep_call(main_x, q_norm[l], k_norm[l], None, col_q=col["qa"], col_k=col["ka"],
                                 nq=a_heads, nk=a_kv, tm=tm_ctx, tiles_per_batch=ctx_len // tm_ctx)
        qa, ka = _aprep_call(main, q_norm[l], k_norm[l], tables, col_q=col["qa"], col_k=col["ka"],
                             nq=a_heads, nk=a_kv, tm=tm_lat, tiles_per_batch=seq // tm_lat)
        ka_x3 = ka_x.reshape(bsz, ctx_len, wkv)
        ya = _gqa_call(qa.reshape(bsz, seq, wa), ka_x3, main_x3, col["va"], ka.reshape(bsz, seq, wkv), main3,
                       col["va"], n_kv_heads=a_kv, group=a_heads // a_kv, tq=tq_attn, tk=tk_attn, out_dtype=BF16)

        yb = _nbr_call(main3, main_x3, bias, layer=l, col_q=col["qb"], col_k=col["kb"], col_v=col["vb"],
                       n_heads=b_heads, out_dtype=BF16)

        def delta(main2d, ab2d, t_len, tm, s0):
            qn, kn, vv = _dnprep_call(main2d, conv_w[l], col=col["qkvc"], n_heads=c_heads, tm=tm,
                                      tiles_per_batch=t_len // tm)
            r3 = lambda a: a.reshape(bsz, t_len, a.shape[-1])
            local = _gdn_local_call(r3(qn), r3(kn), r3(vv), r3(ab2d), gate_consts, n_heads=c_heads,
                                    n_chunks=min(gdn_chunks, t_len // DN_CHUNK))
            return _gdn_scan_call(*local, s0, n_heads=c_heads, n_sub=scan_chunks)

        s_zero = jnp.zeros((bsz, 2 * c_heads, HEAD_DIM, HEAD_DIM), F32)
        ocf_x, ocb_x, s_ctx = delta(main_x, ab_x, ctx_len, tm_ctx, s_zero)
        ocf, ocb, _ = delta(main, ab, seq, tm_lat, s_ctx)

        x_new = _outproj_call(xl, mod_l, ya.reshape(bsz * seq, wa), main, yb.reshape(bsz * seq, wb),
                              ocf.reshape(bsz * seq, wc), ocb.reshape(bsz * seq, wc), o_norm[l], w_out_l,
                              ln_g[l], ln_b[l], col_za=col["za"], col_zb=col["zb"], col_zc=col["zc"],
                              alpha=alpha, tiles_per_batch=seq // tm_lat, ctx_row=None, tm=tm_lat)

        if with_ctx_out:
            ya_x = _ctx_attn_call(qa_x.reshape(bsz, ctx_len, wa), 0, ka_x3, 0, main_x3, col["va"],
                                  n_kv_heads=a_kv, group=a_heads // a_kv, q_scale=None)
            yb_x = _ctx_attn_call(main_x3, col["qb"], main_x3, col["kb"], main_x3, col["vb"],
                                  n_kv_heads=b_heads, group=1, q_scale=SCORE_SCALE)
            xc = _outproj_call(xc, mod_l, ya_x.reshape(bsz * ctx_len, wa), main_x,
                               yb_x.reshape(bsz * ctx_len, wb), ocf_x.reshape(bsz * ctx_len, wc),
                               ocb_x.reshape(bsz * ctx_len, wc), o_norm[l], w_out_l, ln_g[l], ln_b[l],
                               col_za=col["za"], col_zb=col["zb"], col_zc=col["zc"], alpha=alpha,
                               tiles_per_batch=None, ctx_row=bsz, tm=tm_ctx)
        xl = x_new
    return xl.reshape(bsz, seq, d)
```

```python
import functools

import jax
import jax.numpy as jnp
from jax import lax
from jax.experimental import pallas as pl
from jax.experimental.pallas import tpu as pltpu

F32 = jnp.float32
BF16 = jnp.bfloat16

HEAD_DIM = 128
GRID_W = 64
ROPE_THETA = 10000.0
NA_WIN_H = 8
NA_WIN_W = 16
CONV_K = 3
DN_CHUNK = 64
EPS = 1e-6
NEG = -1e30
MOD_ROWS = 8
NBR_Q_ROWS = 8
NBR_K_ROWS = 16
NBR_HALF_Q = 4
NBR_HALF_K = 12
VMEM_LIMIT = 56 * 1024 * 1024
INPROJ_VMEM_LIMIT = 60 * 1024 * 1024
LOG2E = 1.4426950408889634
SCORE_SCALE = HEAD_DIM ** -0.5 * LOG2E

HIGHEST = lax.Precision.HIGHEST


def _silu(z):
    return z * jax.nn.sigmoid(z)


def _softplus(z):
    return jnp.maximum(z, 0.0) + jnp.log(1.0 + jnp.exp(-jnp.abs(z)))


def _dot(a, b, precision=None):
    return jnp.dot(a, b, preferred_element_type=F32, precision=precision)


def _dot_nt(a, b, precision=None):
    return lax.dot_general(a, b, (((1,), (1,)), ((), ())), preferred_element_type=F32,
                           precision=precision)


def _params(sem, vmem_limit=VMEM_LIMIT):
    return pltpu.CompilerParams(dimension_semantics=sem, vmem_limit_bytes=vmem_limit)


def _mod_kernel(c_ref, w_ref, b_ref, o_ref):
    s = _silu(c_ref[...]).astype(BF16)
    o_ref[0] = _dot(s, w_ref[0].astype(BF16)) + b_ref[0]


def _mod_call(cc, w_mod, b_mod):
    depth, d, n = w_mod.shape
    tn = 768
    return pl.pallas_call(
        _mod_kernel,
        grid=(depth, n // tn),
        in_specs=[pl.BlockSpec((MOD_ROWS, d), lambda l, j: (0, 0)),
                  pl.BlockSpec((1, d, tn), lambda l, j: (l, 0, j)),
                  pl.BlockSpec((1, 1, tn), lambda l, j: (l, 0, j))],
        out_specs=pl.BlockSpec((1, MOD_ROWS, tn), lambda l, j: (l, 0, j)),
        out_shape=jax.ShapeDtypeStruct((depth, MOD_ROWS, n), F32),
        compiler_params=_params(("arbitrary", "arbitrary")),
        name="mod",
    )(cc, w_mod, b_mod.reshape(depth, 1, n))


LN_ROWS = 256


def _inproj_kernel(x_ref, shift_ref, scale_ref, w_ref, wab_ref, o_ref, oab_ref, h_ref):
    first = pl.program_id(1) == 0

    @pl.when(first)
    def _():
        for r0 in range(0, x_ref.shape[0], LN_ROWS):
            r = slice(r0, r0 + LN_ROWS)
            x = x_ref[r, :]
            xc = x - jnp.mean(x, axis=-1, keepdims=True)
            y = xc * lax.rsqrt(jnp.mean(xc * xc, axis=-1, keepdims=True) + EPS)
            h_ref[r, :] = (y * (1.0 + scale_ref[0]) + shift_ref[0]).astype(BF16)
            oab_ref[r, :] = _dot(h_ref[r, :], wab_ref[...])
            o_ref[r, :] = _dot(h_ref[r, :], w_ref[...]).astype(o_ref.dtype)

    @pl.when(jnp.logical_not(first))
    def _():
        o_ref[...] = _dot(h_ref[...], w_ref[...]).astype(o_ref.dtype)


def _inproj_call(x2d, mod_l, w_all, w_ab, *, n_main, tiles_per_batch, ctx_row, tm, tn, out_dtype):
    m, d = x2d.shape
    n = n_main
    assert n % tn == 0 and tm % LN_ROWS == 0
    if ctx_row is None:
        row = lambda i: i // tiles_per_batch
    else:
        row = lambda i: ctx_row
    return pl.pallas_call(
        _inproj_kernel,
        grid=(m // tm, n // tn),
        in_specs=[pl.BlockSpec((tm, d), lambda i, j: (i, 0)),
                  pl.BlockSpec((1, 1, d), lambda i, j: (row(i), 0, 0)),
                  pl.BlockSpec((1, 1, d), lambda i, j: (row(i), 0, 1)),
                  pl.BlockSpec((d, tn), lambda i, j: (0, j)),
                  pl.BlockSpec((d, HEAD_DIM), lambda i, j: (0, 0))],
        out_specs=[pl.BlockSpec((tm, tn), lambda i, j: (i, j)),
                   pl.BlockSpec((tm, HEAD_DIM), lambda i, j: (i, 0))],
        out_shape=[jax.ShapeDtypeStruct((m, n), out_dtype),
                   jax.ShapeDtypeStruct((m, HEAD_DIM), F32)],
        scratch_shapes=[pltpu.VMEM((tm, d), BF16)],
        compiler_params=_params(("arbitrary", "arbitrary"), INPROJ_VMEM_LIMIT),
        name="inproj",
    )(x2d, mod_l, mod_l, w_all, w_ab)


def _swap_quarters(x):
    lane = lax.broadcasted_iota(jnp.int32, x.shape, 1)
    fwd = pltpu.roll(x, 3 * HEAD_DIM // 4, axis=1)
    bwd = pltpu.roll(x, HEAD_DIM // 4, axis=1)
    return jnp.where((lane % (HEAD_DIM // 2)) < HEAD_DIM // 4, fwd, bwd)


def _aprep_kernel(*refs, nq, nk, rope):
    if rope:
        q_ref, k_ref, qw_ref, kw_ref, cos_ref, sin_ref, qo_ref, ko_ref = refs
        cos, sin = cos_ref[...], sin_ref[...]
    else:
        q_ref, k_ref, qw_ref, kw_ref, qo_ref, ko_ref = refs

    ones = jnp.ones((2 * HEAD_DIM, HEAD_DIM), BF16)

    def norm(x, w):
        sq = x * x
        hi = sq.astype(BF16)
        lo = (sq - hi.astype(F32)).astype(BF16)
        ss = _dot(jnp.concatenate([hi, lo], axis=1), ones)
        y = x * lax.rsqrt(ss * (1.0 / HEAD_DIM) + EPS) * w
        if rope:
            y = y * cos + _swap_quarters(y) * sin
        return y

    for h in range(nq):
        sl = slice(h * HEAD_DIM, (h + 1) * HEAD_DIM)
        y = norm(q_ref[:, sl].astype(F32), qw_ref[...])
        qo_ref[:, sl] = (y * SCORE_SCALE).astype(qo_ref.dtype)
    for h in range(nk):
        sl = slice(h * HEAD_DIM, (h + 1) * HEAD_DIM)
        ko_ref[:, sl] = norm(k_ref[:, sl].astype(F32), kw_ref[...]).astype(ko_ref.dtype)


def _aprep_call(main, q_norm, k_norm, tables, *, col_q, col_k, nq, nk, tm, tiles_per_batch):
    m = main.shape[0]
    wq, wk = nq * HEAD_DIM, nk * HEAD_DIM
    in_specs = [pl.BlockSpec((tm, wq), lambda i: (i, col_q // wq)),
                pl.BlockSpec((tm, wk), lambda i: (i, col_k // wk)),
                pl.BlockSpec((1, HEAD_DIM), lambda i: (0, 0)),
                pl.BlockSpec((1, HEAD_DIM), lambda i: (0, 0))]
    args = [main, main, q_norm.reshape(1, HEAD_DIM), k_norm.reshape(1, HEAD_DIM)]
    if tables is not None:
        in_specs += [pl.BlockSpec((tm, HEAD_DIM), lambda i: (i % tiles_per_batch, 0))] * 2
        args += list(tables)
    return pl.pallas_call(
        functools.partial(_aprep_kernel, nq=nq, nk=nk, rope=tables is not None),
        grid=(m // tm,),
        in_specs=in_specs,
        out_specs=[pl.BlockSpec((tm, wq), lambda i: (i, 0)),
                   pl.BlockSpec((tm, wk), lambda i: (i, 0))],
        out_shape=[jax.ShapeDtypeStruct((m, wq), BF16), jax.ShapeDtypeStruct((m, wk), BF16)],
        compiler_params=_params(("arbitrary",)),
        name="aprep",
    )(*args)


def _rope_tables(n_tokens):
    t = jnp.arange(n_tokens, dtype=jnp.int32)
    row = (t // GRID_W).astype(F32)
    col = (t % GRID_W).astype(F32)
    half = HEAD_DIM // 2
    inv_freq = ROPE_THETA ** (-jnp.arange(0, half, 2, dtype=F32) / half)
    ar, ac = row[:, None] * inv_freq, col[:, None] * inv_freq
    cos = jnp.concatenate([jnp.cos(ar), jnp.cos(ar), jnp.cos(ac), jnp.cos(ac)], -1)
    sin = jnp.concatenate([-jnp.sin(ar), jnp.sin(ar), -jnp.sin(ac), jnp.sin(ac)], -1)
    return cos, sin


def _ctx_attn_kernel(q_ref, k_ref, v_ref, o_ref, *, group, q_scale):
    tq = q_ref.shape[1]
    q = jnp.concatenate([q_ref[0, :, g * HEAD_DIM:(g + 1) * HEAD_DIM] for g in range(group)], axis=0)
    if q_scale is not None:
        q = q.astype(F32) * q_scale
    s = _dot_nt(q.astype(BF16), k_ref[0].astype(BF16))
    p = jnp.exp2(s - jnp.max(s, axis=-1, keepdims=True))
    o = _dot(p.astype(BF16), v_ref[0].astype(BF16)) / jnp.sum(p, axis=-1, keepdims=True)
    for g in range(group):
        o_ref[0, :, g * HEAD_DIM:(g + 1) * HEAD_DIM] = o[g * tq:(g + 1) * tq].astype(o_ref.dtype)


SOFTMAX_UNIT = 16


def _gqa_kernel(q_ref, kc_ref, vc_ref, kl_ref, vl_ref, o_ref, q4_ref, s0_ref, s1_ref, sc_ref, p_ref,
                m_ref, l_ref, a_ref, acc_ref, *, group, tq, tk):
    rows = group * tq
    half = rows // 2
    n_lat = kl_ref.shape[1] // tk
    for g in range(group):
        q4_ref[g * tq:(g + 1) * tq, :] = q_ref[0, :, g * HEAD_DIM:(g + 1) * HEAD_DIM]
    m_ref[...] = jnp.full(m_ref.shape, -jnp.inf, F32)
    l_ref[...] = jnp.zeros(l_ref.shape, F32)
    acc_ref[...] = jnp.zeros(acc_ref.shape, F32)
    halves = (slice(0, half), slice(half, rows))

    def issue(dst_ref, k, rs):
        dst_ref[rs, :] = _dot_nt(q4_ref[rs, :], k)

    def softmax(src_ref, rs, width):
        for r0 in range(rs.start, rs.stop, SOFTMAX_UNIT):
            r = slice(r0, r0 + SOFTMAX_UNIT)
            cols = [slice(j, j + HEAD_DIM) for j in range(0, width, HEAD_DIM)]
            s = [src_ref[r, c] for c in cols]
            mx = functools.reduce(jnp.maximum, s)
            m_old = m_ref[r, :]
            m_new = jnp.maximum(m_old, jnp.max(mx, axis=-1, keepdims=True))
            alpha = jnp.exp2(m_old - m_new)
            p = [jnp.exp2(x - m_new) for x in s]
            for c, x in zip(cols, p):
                p_ref[r, c] = x.astype(BF16)
            m_ref[r, :] = m_new
            a_ref[r, :] = alpha
            l_ref[r, :] = alpha * l_ref[r, :] + functools.reduce(jnp.add, p)

    def accumulate(rs, width, v):
        pv = _dot(p_ref[rs, 0:width], v)
        acc_ref[rs, :] = acc_ref[rs, :] * a_ref[rs, :] + pv

    def chunk_step(cur_ref, width, v, nxt_ref, k_next):
        for rs in halves:
            if nxt_ref is not None:
                issue(nxt_ref, k_next, rs)
            softmax(cur_ref, rs, width)
            accumulate(rs, width, v)

    lat = lambda ref, i: ref[0, pl.ds(pl.multiple_of(i * tk, tk), tk), :]
    for rs in halves:
        issue(s0_ref, lat(kl_ref, 0), rs)

    def pair(j, carry):
        i = 2 * j
        chunk_step(s0_ref, tk, lat(vl_ref, i), s1_ref, lat(kl_ref, i + 1))
        chunk_step(s1_ref, tk, lat(vl_ref, i + 1), s0_ref, lat(kl_ref, i + 2))
        return carry

    lax.fori_loop(0, n_lat // 2 - 1, pair, 0)
    chunk_step(s0_ref, tk, lat(vl_ref, n_lat - 2), s1_ref, lat(kl_ref, n_lat - 1))
    chunk_step(s1_ref, tk, lat(vl_ref, n_lat - 1), sc_ref, kc_ref[0])
    chunk_step(sc_ref, kc_ref.shape[1], vc_ref[0], None, None)

    o = acc_ref[...] / jnp.sum(l_ref[...], axis=-1, keepdims=True)
    for g in range(group):
        o_ref[0, :, g * HEAD_DIM:(g + 1) * HEAD_DIM] = o[g * tq:(g + 1) * tq].astype(o_ref.dtype)


def _gqa_call(q, k_ctx, v_ctx, v_ctx_col, k_lat, v_lat, v_lat_col, *, n_kv_heads, group, tq, tk, out_dtype):
    b, s, _ = q.shape
    ctx_len = k_ctx.shape[1]
    wq = group * HEAD_DIM
    rows = group * tq
    assert s % (2 * tk) == 0 and s // tk >= 2 and ctx_len % HEAD_DIM == 0 and ctx_len <= tk
    assert rows % (2 * SOFTMAX_UNIT) == 0
    cvc, cvl = v_ctx_col // HEAD_DIM, v_lat_col // HEAD_DIM
    kv_spec = lambda t, c: pl.BlockSpec((1, t, HEAD_DIM), lambda bi, g, qi: (bi, 0, c + g))
    stat = pltpu.VMEM((rows, HEAD_DIM), F32)
    return pl.pallas_call(
        functools.partial(_gqa_kernel, group=group, tq=tq, tk=tk),
        grid=(b, n_kv_heads, s // tq),
        in_specs=[pl.BlockSpec((1, tq, wq), lambda bi, g, qi: (bi, qi, g)),
                  kv_spec(ctx_len, 0), kv_spec(ctx_len, cvc), kv_spec(s, 0), kv_spec(s, cvl)],
        out_specs=pl.BlockSpec((1, tq, wq), lambda bi, g, qi: (bi, qi, g)),
        out_shape=jax.ShapeDtypeStruct((b, s, n_kv_heads * wq), out_dtype),
        scratch_shapes=[pltpu.VMEM((rows, HEAD_DIM), BF16), pltpu.VMEM((rows, tk), F32),
                        pltpu.VMEM((rows, tk), F32), pltpu.VMEM((rows, ctx_len), F32),
                        pltpu.VMEM((rows, tk), BF16), stat, stat, stat, stat],
        compiler_params=_params(("arbitrary", "arbitrary", "arbitrary")),
        name="gqa",
    )(q, k_ctx, v_ctx, k_lat, v_lat)


def _ctx_attn_call(q, q_col, k, k_col, v, v_col, *, n_kv_heads, group, q_scale):
    b, t, _ = q.shape
    wq = group * HEAD_DIM
    kv_spec = lambda c: pl.BlockSpec((1, t, HEAD_DIM), lambda bi, g: (bi, 0, c // HEAD_DIM + g))
    return pl.pallas_call(
        functools.partial(_ctx_attn_kernel, group=group, q_scale=q_scale),
        grid=(b, n_kv_heads),
        in_specs=[pl.BlockSpec((1, t, wq), lambda bi, g: (bi, 0, q_col // wq + g)), kv_spec(k_col), kv_spec(v_col)],
        out_specs=pl.BlockSpec((1, t, wq), lambda bi, g: (bi, 0, g)),
        out_shape=jax.ShapeDtypeStruct((b, t, n_kv_heads * wq), F32),
        compiler_params=_params(("arbitrary", "arbitrary")),
        name="ctx_attn",
    )(q, k, v)


def _nbr_window(cls, qr, kr):
    if cls == 0:
        lo, dr = max(qr - NA_WIN_H // 2, 0), kr - qr
    elif cls == 1:
        lo, dr = qr, kr - NA_WIN_H // 2 - qr
    else:
        lo, dr = min(NA_WIN_H // 2 + qr, NBR_K_ROWS - NA_WIN_H), kr - NBR_Q_ROWS - qr
    if lo <= kr < lo + NA_WIN_H:
        assert -NA_WIN_H < dr < NA_WIN_H
        return dr + NA_WIN_H - 1
    return None


def _nbr_half_offset(cls, hf):
    return (0, (0, NBR_HALF_Q)[hf], NBR_HALF_Q)[cls]


def _bias_kernel(rpb_ref, o_ref, *, n_heads):
    nrow, ncol = 2 * NA_WIN_H - 1, 2 * NA_WIN_W - 1
    base = (pl.program_id(0) * n_heads + pl.program_id(1)) * (nrow * ncol)
    shape = (GRID_W, 2 * GRID_W)
    c = lax.broadcasted_iota(jnp.int32, shape, 0)
    lane = lax.broadcasted_iota(jnp.int32, shape, 1)
    right = lane >= GRID_W
    kc = jnp.where(right, lane - GRID_W, lane)
    c0 = jnp.clip(c - NA_WIN_W // 2, 0, GRID_W - NA_WIN_W)
    col_ok = (kc >= c0) & (kc < c0 + NA_WIN_W)
    idx = kc - c + (NA_WIN_W - 1)
    cache = {}

    def tile(a_left, a_right):
        key = (a_left, a_right)
        if key not in cache:
            t = jnp.full(shape, NEG, F32)
            if key != (None, None):
                for j in range(ncol):
                    sl = NEG if a_left is None else rpb_ref[base + a_left * ncol + j]
                    sr = NEG if a_right is None else rpb_ref[base + a_right * ncol + j]
                    t = jnp.where(idx == j, jnp.where(right, sr, sl), t)
                t = jnp.where(col_ok, t * LOG2E, NEG)
            cache[key] = t
        return cache[key]

    for cls in range(3):
        for hf in range(2):
            off = _nbr_half_offset(cls, hf)
            assert all(off <= kr < off + NBR_HALF_K for q in range(hf * NBR_HALF_Q, (hf + 1) * NBR_HALF_Q)
                       for kr in range(NBR_K_ROWS) if _nbr_window(cls, q, kr) is not None)
            for qr in range(NBR_HALF_Q):
                for p in range(NBR_HALF_K // 2):
                    kr = _nbr_half_offset(cls, hf) + 2 * p
                    t = tile(_nbr_window(cls, hf * NBR_HALF_Q + qr, kr),
                             _nbr_window(cls, hf * NBR_HALF_Q + qr, kr + 1))
                    o_ref[0, 0, cls, hf, qr * GRID_W:(qr + 1) * GRID_W, p * 2 * GRID_W:(p + 1) * 2 * GRID_W] = t


def _bias_call(rpb):
    depth, n_heads = rpb.shape[:2]
    shape = (depth, n_heads, 3, 2, NBR_HALF_Q * GRID_W, NBR_HALF_K * GRID_W)
    return pl.pallas_call(
        functools.partial(_bias_kernel, n_heads=n_heads),
        grid=(depth, n_heads),
        in_specs=[pl.BlockSpec(memory_space=pltpu.SMEM)],
        out_specs=pl.BlockSpec((1, 1) + shape[2:], lambda l, h: (l, h, 0, 0, 0, 0)),
        out_shape=jax.ShapeDtypeStruct(shape, F32),
        compiler_params=_params(("arbitrary", "arbitrary")),
        name="nbr_bias",
    )(rpb.reshape(-1))


def _nbr_kernel(q_ref, k_ref, v_ref, kc_ref, vc_ref, bias_ref, o_ref, *, rows):
    g = pl.program_id(2)
    n_steps = rows // NBR_Q_ROWS
    start_row = jnp.clip(g * NBR_Q_ROWS - NA_WIN_H // 2, 0, rows - NBR_K_ROWS)
    start = pl.multiple_of(start_row * GRID_W, (NA_WIN_H // 2) * GRID_W)
    cls = jnp.where(g == 0, 0, jnp.where(g == n_steps - 1, 2, 1))
    nq, nk = NBR_HALF_Q * GRID_W, NBR_HALF_K * GRID_W
    kc, vc = kc_ref[0].astype(BF16), vc_ref[0].astype(BF16)
    scores = []
    for hf in range(2):
        off = jnp.where(cls == 0, 0, jnp.where(cls == 1, hf * NBR_HALF_Q, NBR_HALF_Q))
        st = pl.multiple_of(start + off * GRID_W, NBR_HALF_Q * GRID_W)
        q = (q_ref[0, hf * nq:(hf + 1) * nq, :].astype(F32) * SCORE_SCALE).astype(BF16)
        s_win = _dot_nt(q, k_ref[0, pl.ds(st, nk), :].astype(BF16)) + bias_ref[0, 0, cls, hf]
        scores.append((st, s_win, _dot_nt(q, kc)))
    for hf, (st, s_win, s_ctx) in enumerate(scores):
        m = jnp.maximum(jnp.max(s_win, axis=-1, keepdims=True), jnp.max(s_ctx, axis=-1, keepdims=True))
        p_win = jnp.exp2(s_win - m)
        p_ctx = jnp.exp2(s_ctx - m)
        l = jnp.sum(p_win, axis=-1, keepdims=True) + jnp.sum(p_ctx, axis=-1, keepdims=True)
        o = _dot(p_win.astype(BF16), v_ref[0, pl.ds(st, nk), :].astype(BF16)) + _dot(p_ctx.astype(BF16), vc)
        o_ref[0, hf * nq:(hf + 1) * nq, :] = (o / l).astype(o_ref.dtype)


def _nbr_call(main, main_ctx, bias, *, layer, col_q, col_k, col_v, n_heads, out_dtype):
    b, s, _ = main.shape
    ctx_len = main_ctx.shape[1]
    rows = s // GRID_W
    tq = NBR_Q_ROWS * GRID_W
    cq, ck, cv = col_q // HEAD_DIM, col_k // HEAD_DIM, col_v // HEAD_DIM
    return pl.pallas_call(
        functools.partial(_nbr_kernel, rows=rows),
        grid=(n_heads, b, rows // NBR_Q_ROWS),
        in_specs=[pl.BlockSpec((1, tq, HEAD_DIM), lambda h, bi, g: (bi, g, cq + h)),
                  pl.BlockSpec((1, s, HEAD_DIM), lambda h, bi, g: (bi, 0, ck + h)),
                  pl.BlockSpec((1, s, HEAD_DIM), lambda h, bi, g: (bi, 0, cv + h)),
                  pl.BlockSpec((1, ctx_len, HEAD_DIM), lambda h, bi, g: (bi, 0, ck + h)),
                  pl.BlockSpec((1, ctx_len, HEAD_DIM), lambda h, bi, g: (bi, 0, cv + h)),
                  pl.BlockSpec((1, 1) + bias.shape[2:], lambda h, bi, g: (layer, h, 0, 0, 0, 0))],
        out_specs=pl.BlockSpec((1, tq, HEAD_DIM), lambda h, bi, g: (bi, g, h)),
        out_shape=jax.ShapeDtypeStruct((b, s, n_heads * HEAD_DIM), out_dtype),
        compiler_params=_params(("arbitrary", "arbitrary", "arbitrary")),
        name="nbr_attn",
    )(main, main, main, main_ctx, main_ctx, bias)


def _dnprep_kernel(x_ref, prev_ref, next_ref, w_ref, q_ref, k_ref, v_ref, *, tiles_per_batch, n_heads):
    i = pl.program_id(0) % tiles_per_batch
    x = x_ref[...].astype(F32)
    tm = x.shape[0]
    row = lax.broadcasted_iota(jnp.int32, x.shape, 0)
    halo = prev_ref.shape[0]
    prev_row = jnp.where(i == 0, 0.0, prev_ref[halo - 1:halo, :].astype(F32))
    next_row = jnp.where(i == tiles_per_batch - 1, 0.0, next_ref[0:1, :].astype(F32))
    x_prev = jnp.where(row == 0, prev_row, pltpu.roll(x, 1, axis=0))
    x_next = jnp.where(row == tm - 1, next_row, pltpu.roll(x, tm - 1, axis=0))
    y = _silu(w_ref[0:1, :] * x_prev + w_ref[1:2, :] * x + w_ref[2:3, :] * x_next)
    w = n_heads * HEAD_DIM
    for h in range(n_heads):
        sl = slice(h * HEAD_DIM, (h + 1) * HEAD_DIM)
        qh = y[:, h * HEAD_DIM:(h + 1) * HEAD_DIM]
        kh = y[:, w + h * HEAD_DIM:w + (h + 1) * HEAD_DIM]
        qn = qh * lax.rsqrt(jnp.sum(qh * qh, axis=-1, keepdims=True) + EPS) * HEAD_DIM ** -0.5
        q_ref[:, sl] = qn.astype(q_ref.dtype)
        k_ref[:, sl] = (kh * lax.rsqrt(jnp.sum(kh * kh, axis=-1, keepdims=True) + EPS)).astype(k_ref.dtype)
    v_ref[...] = y[:, 2 * w:].astype(v_ref.dtype)


def _dnprep_call(main, conv_w, *, col, n_heads, tm, tiles_per_batch):
    m = main.shape[0]
    w3 = 3 * n_heads * HEAD_DIM
    halo = 16
    hb = tm // halo
    n_halo = m // halo
    out = jax.ShapeDtypeStruct((m, n_heads * HEAD_DIM), BF16)
    o_spec = pl.BlockSpec((tm, n_heads * HEAD_DIM), lambda i: (i, 0))
    return pl.pallas_call(
        functools.partial(_dnprep_kernel, tiles_per_batch=tiles_per_batch, n_heads=n_heads),
        grid=(m // tm,),
        in_specs=[pl.BlockSpec((tm, w3), lambda i: (i, col // w3)),
                  pl.BlockSpec((halo, w3), lambda i: (jnp.maximum(i * hb - 1, 0), col // w3)),
                  pl.BlockSpec((halo, w3), lambda i: (jnp.minimum((i + 1) * hb, n_halo - 1), col // w3)),
                  pl.BlockSpec((CONV_K, w3), lambda i: (0, 0))],
        out_specs=[o_spec, o_spec, o_spec],
        out_shape=[out, out, out],
        compiler_params=_params(("arbitrary",)),
        name="dn_prep",
    )(main, main, main, conv_w)


def _tri_masks(n, upper):
    i = lax.broadcasted_iota(jnp.int32, (n, n), 0)
    j = lax.broadcasted_iota(jnp.int32, (n, n), 1)
    incl = (j >= i) if upper else (j <= i)
    strict = (j > i) if upper else (j < i)
    eye = (i == j).astype(F32)
    base = 8
    same_base = (i // base) == (j // base)
    levels = []
    s = base
    while s < n:
        levels.append(((i // (2 * s)) == (j // (2 * s))) & ((i // s) != (j // s)))
        s *= 2
    return incl, strict, eye, same_base, levels


def _unit_tri_inverse(lms, masks):
    mm = lambda a, b: _dot(a.astype(BF16), b.astype(BF16))
    eyes = [m[2] for m in masks]
    l1 = [jnp.where(m[3], lm, 0.0) for lm, m in zip(lms, masks)]
    l2 = [mm(a, a) for a in l1]
    l4 = [mm(a, a) for a in l2]
    t = [mm(e - a, e + b) for e, a, b in zip(eyes, l1, l2)]
    t = [mm(a, e + b) for e, a, b in zip(eyes, t, l4)]
    for lvl in range(len(masks[0][4])):
        cs = [jnp.where(m[4][lvl], lm, 0.0) for lm, m in zip(lms, masks)]
        tc = [mm(a, b) for a, b in zip(t, cs)]
        t = [a - mm(b, a) for a, b in zip(t, tc)]
    return t


def _gdn_local_kernel(q_ref, k_ref, v_ref, ab_ref, alr_ref, dtr_ref,
                      wq_ref, u_ref, kdt_ref, qk_ref, eg_ref, *, n_heads, n_chunks):
    c = DN_CHUNK
    masks = [_tri_masks(c, upper) for upper in (False, True)]
    gates = {}
    for d in range(2):
        cum = masks[d][0].astype(F32)
        for ci in range(n_chunks):
            ab = ab_ref[0, ci * c:(ci + 1) * c, :]
            g_tok = -jnp.exp(alr_ref[...]) * _softplus(ab + dtr_ref[...])
            gc_tok = _dot(cum, g_tok, HIGHEST)
            gates[ci, d] = (jax.nn.sigmoid(ab), gc_tok, gc_tok.T)

    streams = [(ci, d, h) for ci in range(n_chunks) for d in range(2) for h in range(n_heads)]
    pre = []
    for ci, d, h in streams:
        incl, strict = masks[d][0], masks[d][1]
        rows = slice(ci * c, (ci + 1) * c)
        sl = slice(h * HEAD_DIM, (h + 1) * HEAD_DIM)
        cb, cg = d * 2 * n_heads + h, d * 2 * n_heads + n_heads + h
        beta_tok, gc_tok, gc_rows = gates[ci, d]
        q, k, v = q_ref[0, rows, sl], k_ref[0, rows, sl], v_ref[0, rows, sl]
        beta_c = beta_tok[:, cb:cb + 1]
        gc_c = gc_tok[:, cg:cg + 1]
        last = 0 if d == 1 else c - 1
        g_last = gc_c[last:last + 1, :]
        decay = jnp.where(incl, jnp.exp(jnp.where(incl, gc_c - gc_rows[cg:cg + 1, :], 0.0)), 0.0)
        kbeta = k * beta_c
        e_c = jnp.exp(gc_c)
        wq_ref[0, d, ci, c:2 * c, sl] = (q * e_c).astype(wq_ref.dtype)
        kdt_ref[0, d, ci, h] = (k * jnp.exp(g_last - gc_c)).T.astype(kdt_ref.dtype)
        eg_ref[0, d, ci, h:h + 1, :] = jnp.broadcast_to(jnp.exp(g_last), (1, HEAD_DIM))
        lhs = jnp.concatenate([kbeta, q], axis=0).astype(BF16)
        rhs = jnp.concatenate([v * beta_c, kbeta * e_c], axis=1).astype(BF16)
        pre.append((lhs, k.astype(BF16), rhs, decay))
    kq = [_dot_nt(lhs, kb) for lhs, kb, _, _ in pre]
    lms = []
    for (ci, d, h), x, (_, _, _, decay) in zip(streams, kq, pre):
        qk_ref[0, d, ci, h] = (x[c:] * decay).astype(qk_ref.dtype)
        lms.append(jnp.where(masks[d][1], x[:c] * decay, 0.0))
    ts = _unit_tri_inverse(lms, [masks[d] for _, d, _ in streams])
    uws = [_dot(t.astype(BF16), rhs) for t, (_, _, rhs, _) in zip(ts, pre)]
    for (ci, d, h), uw in zip(streams, uws):
        sl = slice(h * HEAD_DIM, (h + 1) * HEAD_DIM)
        u_ref[0, d, ci * c:(ci + 1) * c, sl] = uw[:, :HEAD_DIM]
        wq_ref[0, d, ci, 0:c, sl] = uw[:, HEAD_DIM:].astype(wq_ref.dtype)


def _gdn_local_call(qn, kn, vv, ab, gate_consts, *, n_heads, n_chunks):
    b, t, w = qn.shape
    c = DN_CHUNK
    n = t // c
    tm = n_chunks * c
    tok = pl.BlockSpec((1, tm, w), lambda bi, i: (bi, i, 0))
    const = lambda a: pl.BlockSpec(a.shape, lambda bi, i: (0,) * a.ndim)
    return pl.pallas_call(
        functools.partial(_gdn_local_kernel, n_heads=n_heads, n_chunks=n_chunks),
        grid=(b, n // n_chunks),
        in_specs=[tok, tok, tok, pl.BlockSpec((1, tm, HEAD_DIM), lambda bi, i: (bi, i, 0))]
                 + [const(a) for a in gate_consts],
        out_specs=[pl.BlockSpec((1, 2, n_chunks, 2 * c, w), lambda bi, i: (bi, 0, i, 0, 0)),
                   pl.BlockSpec((1, 2, tm, w), lambda bi, i: (bi, 0, i, 0)),
                   pl.BlockSpec((1, 2, n_chunks, n_heads, HEAD_DIM, c), lambda bi, i: (bi, 0, i, 0, 0, 0)),
                   pl.BlockSpec((1, 2, n_chunks, n_heads, c, c), lambda bi, i: (bi, 0, i, 0, 0, 0)),
                   pl.BlockSpec((1, 2, n_chunks, n_heads, HEAD_DIM), lambda bi, i: (bi, 0, i, 0, 0))],
        out_shape=[jax.ShapeDtypeStruct((b, 2, n, 2 * c, w), BF16),
                   jax.ShapeDtypeStruct((b, 2, t, w), F32),
                   jax.ShapeDtypeStruct((b, 2, n, n_heads, HEAD_DIM, c), BF16),
                   jax.ShapeDtypeStruct((b, 2, n, n_heads, c, c), BF16),
                   jax.ShapeDtypeStruct((b, 2, n, n_heads, HEAD_DIM), F32)],
        compiler_params=_params(("arbitrary", "arbitrary")),
        name="gdn_local",
    )(qn, kn, vv, ab, *gate_consts)


def _gdn_scan_kernel(wqf_ref, wqb_ref, uf_ref, ub_ref, kdtf_ref, kdtb_ref, qkf_ref, qkb_ref, egf_ref,
                     egb_ref, s0_ref, of_ref, ob_ref, sfin_ref, s_ref, *, n_heads, bsz, n_sub):
    n = pl.program_id(0)
    c = DN_CHUNK

    @pl.when(n == 0)
    def _():
        s_ref[...] = s0_ref[...]

    refs = ((wqf_ref, uf_ref, kdtf_ref, qkf_ref, egf_ref, of_ref),
            (wqb_ref, ub_ref, kdtb_ref, qkb_ref, egb_ref, ob_ref))
    streams = [(b, d, h) for b in range(bsz) for d in range(2) for h in range(n_heads)]
    cols = lambda h: slice(h * HEAD_DIM, (h + 1) * HEAD_DIM)
    for sub in range(n_sub):
        ck = lambda d: n_sub - 1 - sub if d == 1 else sub
        tok = lambda d: slice(ck(d) * c, (ck(d) + 1) * c)
        ws = [_dot(refs[d][0][b, 0, ck(d), :, cols(h)], s_ref[b, d * n_heads + h].astype(BF16))
              for b, d, h in streams]
        v_new = [(refs[d][1][b, 0, tok(d), cols(h)] - x[:c]).astype(BF16) for (b, d, h), x in zip(streams, ws)]
        intra = [_dot(refs[d][3][b, 0, ck(d), h], v) for (b, d, h), v in zip(streams, v_new)]
        upd = [_dot(refs[d][2][b, 0, ck(d), h], v) for (b, d, h), v in zip(streams, v_new)]
        for (b, d, h), x, y, z in zip(streams, ws, intra, upd):
            refs[d][5][b, tok(d), cols(h)] = x[c:] + y
            i = d * n_heads + h
            s_ref[b, i] = s_ref[b, i] * refs[d][4][b, 0, ck(d), h:h + 1, :] + z

    @pl.when(n == pl.num_programs(0) - 1)
    def _():
        sfin_ref[...] = s_ref[...]


def _gdn_scan_call(wq, u, kdt, qk, eg, s0, *, n_heads, n_sub):
    b, _, n, _, w = wq.shape
    c = DN_CHUNK
    t = n * c
    steps = n // n_sub
    assert n % n_sub == 0

    def pair(arr, blk):
        nd = len(blk)
        fwd = pl.BlockSpec((b, 1) + blk, lambda i: (0, 0, i) + (0,) * (nd - 1))
        bwd = pl.BlockSpec((b, 1) + blk, lambda i: (0, 1, steps - 1 - i) + (0,) * (nd - 1))
        return [fwd, bwd], [arr, arr]

    specs, args = [], []
    for arr, blk in ((wq, (n_sub, 2 * c, w)), (u, (n_sub * c, w)), (kdt, (n_sub, n_heads, HEAD_DIM, c)),
                     (qk, (n_sub, n_heads, c, c)), (eg, (n_sub, n_heads, HEAD_DIM))):
        sp, ar = pair(arr, blk)
        specs += sp
        args += ar
    st_spec = pl.BlockSpec(s0.shape, lambda i: (0, 0, 0, 0))
    o_shape = jax.ShapeDtypeStruct((b, t, w), F32)
    return pl.pallas_call(
        functools.partial(_gdn_scan_kernel, n_heads=n_heads, bsz=b, n_sub=n_sub),
        grid=(steps,),
        in_specs=specs + [st_spec],
        out_specs=[pl.BlockSpec((b, n_sub * c, w), lambda i: (0, i, 0)),
                   pl.BlockSpec((b, n_sub * c, w), lambda i: (0, steps - 1 - i, 0)), st_spec],
        out_shape=[o_shape, o_shape, jax.ShapeDtypeStruct(s0.shape, F32)],
        scratch_shapes=[pltpu.VMEM(s0.shape, F32)],
        compiler_params=_params(("arbitrary",)),
        name="gdn_scan",
    )(*args, s0)


def _gate_consts(a_log, dt_bias):
    n_heads = a_log.shape[1]
    z = jnp.zeros((2, n_heads), F32)
    lay = lambda p: jnp.concatenate([z, p.astype(F32)], axis=1).reshape(-1)
    al, dt = lay(a_log), lay(dt_bias)
    pad = lambda vec: jnp.pad(vec, (0, HEAD_DIM - vec.shape[0])).reshape(1, HEAD_DIM)
    return pad(al), pad(dt)


OUT_ROWS = 256


def _outproj_kernel(x_ref, gate_ref, ya_ref, za0_ref, za1_ref, yb_ref, zb_ref, ocf_ref, ocb_ref, zc_ref, on_ref,
                    w_ref, lng_ref, lnb_ref, o_ref, mix_ref, y_ref, *, alpha, wa, wb, n_c_heads):
    blocks = [slice(r0, r0 + OUT_ROWS) for r0 in range(0, x_ref.shape[0], OUT_ROWS)]

    def project(r):
        for i, za_ref in enumerate((za0_ref, za1_ref)):
            cs = slice(i * wa // 2, (i + 1) * wa // 2)
            mix_ref[r, cs] = (ya_ref[r, cs].astype(F32) * _silu(za_ref[r, :].astype(F32))).astype(BF16)
        mix_ref[r, wa:wa + wb] = (yb_ref[r, :].astype(F32) * _silu(zb_ref[r, :].astype(F32))).astype(BF16)
        for h in range(n_c_heads):
            sl = slice(h * HEAD_DIM, (h + 1) * HEAD_DIM)
            oc = ocf_ref[r, sl] + ocb_ref[r, sl]
            yc = oc * lax.rsqrt(jnp.mean(oc * oc, axis=-1, keepdims=True) + EPS) * on_ref[...]
            off = wa + wb + h * HEAD_DIM
            mix_ref[r, off:off + HEAD_DIM] = (yc * _silu(zc_ref[r, sl].astype(F32))).astype(BF16)
        y_ref[r, :] = _dot(mix_ref[r, :], w_ref[...])

    def finish(r):
        t = alpha * x_ref[r, :] + gate_ref[0] * y_ref[r, :]
        tc = t - jnp.mean(t, axis=-1, keepdims=True)
        scale = lax.rsqrt(jnp.mean(tc * tc, axis=-1, keepdims=True) + EPS)
        o_ref[r, :] = tc * scale * lng_ref[...] + lnb_ref[...]

    for r in blocks:
        project(r)
    for r in blocks:
        finish(r)


def _outproj_call(x2d, mod_l, ya, main, yb, ocf, ocb, o_norm, w_out, ln_g, ln_b, *, col_za, col_zb,
                  col_zc, alpha, tiles_per_batch, ctx_row, tm):
    m, d = x2d.shape
    wa, wb, wc = ya.shape[1], yb.shape[1], ocf.shape[1]
    if ctx_row is None:
        row = lambda i: i // tiles_per_batch
    else:
        row = lambda i: ctx_row
    rows = lambda w, cb=0: pl.BlockSpec((tm, w), lambda i: (i, cb))
    one = lambda w: pl.BlockSpec((1, w), lambda i: (0, 0))
    wz = wa // 2
    assert col_za % wz == 0 and col_zb % wb == 0 and col_zc % wc == 0 and tm % OUT_ROWS == 0
    return pl.pallas_call(
        functools.partial(_outproj_kernel, alpha=alpha, wa=wa, wb=wb, n_c_heads=wc // HEAD_DIM),
        grid=(m // tm,),
        in_specs=[rows(d), pl.BlockSpec((1, 1, d), lambda i: (row(i), 0, 2)),
                  rows(wa), rows(wz, col_za // wz), rows(wz, col_za // wz + 1), rows(wb), rows(wb, col_zb // wb),
                  rows(wc), rows(wc), rows(wc, col_zc // wc), one(HEAD_DIM),
                  pl.BlockSpec(w_out.shape, lambda i: (0, 0)), one(d), one(d)],
        out_specs=rows(d),
        out_shape=jax.ShapeDtypeStruct((m, d), F32),
        scratch_shapes=[pltpu.VMEM((tm, wa + wb + wc), BF16), pltpu.VMEM((tm, d), F32)],
        compiler_params=_params(("arbitrary",)),
        name="outproj",
    )(x2d, mod_l, ya, main, main, yb, main, ocf, ocb, main, o_norm.reshape(1, HEAD_DIM), w_out,
      ln_g.reshape(1, d), ln_b.reshape(1, d))


def kernel(x, c, ctx, c_ctx, w_mod, b_mod, w_in, q_norm, k_norm, rpb, conv_w, a_log, dt_bias, o_norm,
           w_out, ln_g, ln_b):
    bsz, seq, d = x.shape
    ctx_len = ctx.shape[1]
    depth = w_mod.shape[0]
    a_heads = d // (2 * HEAD_DIM)
    a_kv = a_heads // 4
    b_heads = d // (4 * HEAD_DIM)
    c_heads = d // (4 * HEAD_DIM)
    wa, wkv, wb, wc = a_heads * HEAD_DIM, a_kv * HEAD_DIM, b_heads * HEAD_DIM, c_heads * HEAD_DIM
    assert bsz < MOD_ROWS and seq % (NBR_Q_ROWS * GRID_W) == 0 and seq // GRID_W >= NBR_K_ROWS
    assert ctx_len % DN_CHUNK == 0 and 4 * c_heads <= 16
    alpha = (2 * depth) ** 0.25

    sizes = dict(qa=wa, ka=wkv, va=wkv, za=wa, qb=wb, kb=wb, vb=wb, zb=wb, qkvc=3 * wc, zc=wc)
    col, off = {}, 0
    for name, size in sizes.items():
        col[name] = off
        off += size
    n_main, n_ab = off, 4 * c_heads

    cc = jnp.concatenate([c, c_ctx[None], jnp.zeros((MOD_ROWS - bsz - 1, d), F32)], axis=0)
    mod = _mod_call(cc, w_mod, b_mod)
    tables = _rope_tables(seq)
    bias = _bias_call(rpb)

    tm_lat = 512
    tm_proj = 2048 if seq % 2048 == 0 else tm_lat
    tm_ctx = min(256, ctx_len)
    tm_proj_ctx = 1024 if (bsz * ctx_len) % 1024 == 0 else tm_ctx
    tn_proj = 512
    tq_attn, tk_attn = 256, 1024
    gdn_chunks, scan_chunks = 4, 4
    main_dtype = BF16
    xl = x.reshape(bsz * seq, d)
    xc = ctx.reshape(bsz * ctx_len, d)
    for l in range(depth):
        with_ctx_out = l < depth - 1
        mod_l = mod[l].reshape(MOD_ROWS, 1, 3 * d)
        w_all = w_in[l].astype(BF16)
        w_ab = jnp.pad(w_all[:, n_main:n_main + n_ab], ((0, 0), (0, HEAD_DIM - n_ab)))
        w_out_l = w_out[l].astype(BF16)
        gate_consts = _gate_consts(a_log[l], dt_bias[l])

        main_x, ab_x = _inproj_call(xc, mod_l, w_all, w_ab, n_main=n_main, tiles_per_batch=None,
                                    ctx_row=bsz, tm=tm_proj_ctx, tn=tn_proj, out_dtype=main_dtype)
        main, ab = _inproj_call(xl, mod_l, w_all, w_ab, n_main=n_main, tiles_per_batch=seq // tm_proj,
                                ctx_row=None, tm=tm_proj, tn=tn_proj, out_dtype=main_dtype)
        main3 = main.reshape(bsz, seq, n_main)
        main_x3 = main_x.reshape(bsz, ctx_len, n_main)

        qa_x, ka_x = _aprep_call(main_x, q_norm[l], k_norm[l], None, col_q=col["qa"], col_k=col["ka"],
                                 nq=a_heads, nk=a_kv, tm=tm_ctx, tiles_per_batch=ctx_len // tm_ctx)
        qa, ka = _aprep_call(main, q_norm[l], k_norm[l], tables, col_q=col["qa"], col_k=col["ka"],
                             nq=a_heads, nk=a_kv, tm=tm_lat, tiles_per_batch=seq // tm_lat)
        ka_x3 = ka_x.reshape(bsz, ctx_len, wkv)
        ya = _gqa_call(qa.reshape(bsz, seq, wa), ka_x3, main_x3, col["va"], ka.reshape(bsz, seq, wkv), main3,
                       col["va"], n_kv_heads=a_kv, group=a_heads // a_kv, tq=tq_attn, tk=tk_attn, out_dtype=BF16)

        yb = _nbr_call(main3, main_x3, bias, layer=l, col_q=col["qb"], col_k=col["kb"], col_v=col["vb"],
                       n_heads=b_heads, out_dtype=BF16)

        def delta(main2d, ab2d, t_len, tm, s0):
            qn, kn, vv = _dnprep_call(main2d, conv_w[l], col=col["qkvc"], n_heads=c_heads, tm=tm,
                                      tiles_per_batch=t_len // tm)
            r3 = lambda a: a.reshape(bsz, t_len, a.shape[-1])
            local = _gdn_local_call(r3(qn), r3(kn), r3(vv), r3(ab2d), gate_consts, n_heads=c_heads,
                                    n_chunks=min(gdn_chunks, t_len // DN_CHUNK))
            return _gdn_scan_call(*local, s0, n_heads=c_heads, n_sub=min(scan_chunks, t_len // DN_CHUNK))

        s_zero = jnp.zeros((bsz, 2 * c_heads, HEAD_DIM, HEAD_DIM), F32)
        ocf_x, ocb_x, s_ctx = delta(main_x, ab_x, ctx_len, tm_ctx, s_zero)
        ocf, ocb, _ = delta(main, ab, seq, tm_lat, s_ctx)

        x_new = _outproj_call(xl, mod_l, ya.reshape(bsz * seq, wa), main, yb.reshape(bsz * seq, wb),
                              ocf.reshape(bsz * seq, wc), ocb.reshape(bsz * seq, wc), o_norm[l], w_out_l,
                              ln_g[l], ln_b[l], col_za=col["za"], col_zb=col["zb"], col_zc=col["zc"],
                              alpha=alpha, tiles_per_batch=seq // tm_lat, ctx_row=None, tm=tm_lat)

        if with_ctx_out:
            ya_x = _ctx_attn_call(qa_x.reshape(bsz, ctx_len, wa), 0, ka_x3, 0, main_x3, col["va"],
                                  n_kv_heads=a_kv, group=a_heads // a_kv, q_scale=None)
            yb_x = _ctx_attn_call(main_x3, col["qb"], main_x3, col["kb"], main_x3, col["vb"],
                                  n_kv_heads=b_heads, group=1, q_scale=SCORE_SCALE)
            xc = _outproj_call(xc, mod_l, ya_x.reshape(bsz * ctx_len, wa), main_x,
                               yb_x.reshape(bsz * ctx_len, wb), ocf_x.reshape(bsz * ctx_len, wc),
                               ocb_x.reshape(bsz * ctx_len, wc), o_norm[l], w_out_l, ln_g[l], ln_b[l],
                               col_za=col["za"], col_zb=col["zb"], col_zc=col["zc"], alpha=alpha,
                               tiles_per_batch=None, ctx_row=bsz, tm=tm_ctx)
        xl = x_new
    return xl.reshape(bsz, seq, d)
```
